```python
import math
import jax
import jax.numpy as jnp
from jax import lax
import numpy as np

D_MODEL = 1024
BATCH = 16
SEQ = 2048
DEPTH = 4

GRID_W = 64
CTX_LEN = 256
N_GROUPS = 4
HEADS = 4
HEAD_DIM = D_MODEL // (N_GROUPS * HEADS)
MIX_WIDTH = N_GROUPS * HEADS * HEAD_DIM
NA_WIN_R = 8
NA_WIN_C = 16
MLA_Q_LORA = 3 * D_MODEL // 16
MLA_KV_LORA = D_MODEL // 8
MLA_NOPE = HEAD_DIM
MLA_ROPE = HEAD_DIM // 2
MLA_V = HEAD_DIM
SWA_HKV = 2
SWA_GROUP = HEADS // SWA_HKV
SWA_WINDOW = 128
DIFF_DH = HEAD_DIM // 2
D_FF = 4 * D_MODEL
BLK = 128
ROPE_BASE = 10000.0
EPS = 1e-6
NEG_INF = -1e30
IN_SIZES = (
    HEADS * HEAD_DIM, HEADS * HEAD_DIM, HEADS * HEAD_DIM,
    MLA_Q_LORA, MLA_KV_LORA, MLA_ROPE,
    HEADS * HEAD_DIM, SWA_HKV * HEAD_DIM, SWA_HKV * HEAD_DIM,
    HEADS * 2 * DIFF_DH, HEADS * 2 * DIFF_DH, HEADS * HEAD_DIM,
)
IN_COLS = sum(IN_SIZES)

kernel_name = 'hybrid_parallel_heads_dit_block'


def rmsnorm(x, g):
    xf = x.astype(jnp.float32)
    y = xf * lax.rsqrt(jnp.mean(xf * xf, axis=-1, keepdims=True) + EPS)
    return (y * g.astype(jnp.float32)).astype(x.dtype)


def modulate(h, shift, scale):
    return h * (1 + scale) + shift


def split_heads(t, n):
    return t.reshape(t.shape[:-1] + (n, t.shape[-1] // n))


def split_cols(z):
    parts, start = [], 0
    for size in IN_SIZES:
        parts.append(z[..., start:start + size])
        start += size
    return parts


def softmax_f32(s):
    return jax.nn.softmax(s.astype(jnp.float32), axis=-1)


def axial_rope(row, col, dim):
    n_freq = dim // 4
    freqs = ROPE_BASE ** (-jnp.arange(n_freq, dtype=jnp.float32) / n_freq)
    ang = jnp.concatenate([row.astype(jnp.float32)[:, None] * freqs,
                           col.astype(jnp.float32)[:, None] * freqs], axis=-1)
    return jnp.cos(ang), jnp.sin(ang)


def apply_rope(x, rope):
    cos, sin = rope
    cos = cos[:, None, :].astype(x.dtype)
    sin = sin[:, None, :].astype(x.dtype)
    half = x.shape[-1] // 2
    x1, x2 = x[..., :half], x[..., half:]
    return jnp.concatenate([x1 * cos - x2 * sin, x1 * sin + x2 * cos], axis=-1)


def sweep_query_blocks(fn, q):
    B, L = q.shape[:2]
    qb = jnp.moveaxis(q.reshape((B, L // BLK, BLK) + q.shape[2:]), 1, 0)
    o = lax.map(fn, qb)
    return jnp.moveaxis(o, 0, 1).reshape((B, L) + o.shape[3:])


def attend(q, k, v, scale):
    s = jnp.einsum('bqhd,bkhd->bhqk', q, k).astype(jnp.float32) * scale
    p = softmax_f32(s).astype(v.dtype)
    return jnp.einsum('bhqk,bkhd->bqhd', p, v)


def neighbourhood_attention(zl, zc, rpb, need_ctx):
    q, k, v = (split_heads(t, HEADS) for t in zl)
    qc, kc, vc = (split_heads(t, HEADS) for t in zc)
    B, S = q.shape[:2]
    rows_n = S // GRID_W
    kr_n = min(NA_WIN_R, rows_n)
    n_loc = kr_n * GRID_W
    scale = HEAD_DIM ** -0.5
    qg = q.reshape(B, rows_n, GRID_W, HEADS, HEAD_DIM)
    kg = k.reshape(B, rows_n, GRID_W, HEADS, HEAD_DIM)
    vg = v.reshape(B, rows_n, GRID_W, HEADS, HEAD_DIM)
    col = jnp.arange(GRID_W)
    c0 = jnp.clip(col - NA_WIN_C // 2, 0, GRID_W - NA_WIN_C)
    col_ok = (col[None, :] >= c0[:, None]) & (col[None, :] < c0[:, None] + NA_WIN_C)
    dc_idx = jnp.clip(col[None, :] - col[:, None], 1 - NA_WIN_C, NA_WIN_C - 1) + (NA_WIN_C - 1)

    def row_block(r):
        r0 = jnp.clip(r - kr_n // 2, 0, rows_n - kr_n)
        q_r = lax.dynamic_index_in_dim(qg, r, axis=1, keepdims=False)
        k_r = lax.dynamic_slice_in_dim(kg, r0, kr_n, axis=1).reshape(B, n_loc, HEADS, HEAD_DIM)
        v_r = lax.dynamic_slice_in_dim(vg, r0, kr_n, axis=1).reshape(B, n_loc, HEADS, HEAD_DIM)
        dr_idx = r0 + jnp.arange(kr_n) - r + (NA_WIN_R - 1)
        bias = rpb[:, dr_idx[None, :, None], dc_idx[:, None, :]].astype(jnp.float32)
        bias = jnp.where(col_ok[None, :, None, :], bias, NEG_INF).reshape(HEADS, GRID_W, n_loc)
        s_loc = jnp.einsum('bqhd,bkhd->bhqk', q_r, k_r).astype(jnp.float32) * scale + bias
        s_ctx = jnp.einsum('bqhd,bchd->bhqc', q_r, kc).astype(jnp.float32) * scale
        p = softmax_f32(jnp.concatenate([s_loc, s_ctx], axis=-1)).astype(v.dtype)
        return (jnp.einsum('bhqk,bkhd->bqhd', p[..., :n_loc], v_r)
                + jnp.einsum('bhqc,bchd->bqhd', p[..., n_loc:], vc))

    o = lax.map(row_block, jnp.arange(rows_n))
    o_lat = jnp.moveaxis(o, 0, 1).reshape(B, S, HEADS * HEAD_DIM)
    o_ctx = attend(qc, kc, vc, scale).reshape(B, -1, HEADS * HEAD_DIM) if need_ctx else None
    return o_lat, o_ctx


def mla_project(cq, ckv, kr, gq, w_uq, gkv, w_ukv, rope):
    q = split_heads(rmsnorm(cq, gq) @ w_uq, HEADS)
    kv = split_heads(rmsnorm(ckv, gkv) @ w_ukv, HEADS)
    q_nope, q_rope = q[..., :MLA_NOPE], q[..., MLA_NOPE:]
    k_nope, v = kv[..., :MLA_NOPE], kv[..., MLA_NOPE:]
    k_rope = kr[:, :, None, :]
    if rope is not None:
        q_rope = apply_rope(q_rope, rope)
        k_rope = apply_rope(k_rope, rope)
    q = jnp.concatenate([q_nope, q_rope], axis=-1)
    k = jnp.concatenate([k_nope, jnp.broadcast_to(k_rope, k_nope.shape[:-1] + (MLA_ROPE,))], axis=-1)
    return q, k, v


def latent_attention(zl, zc, gq, w_uq, gkv, w_ukv, rope, need_ctx):
    B, S = zl[0].shape[:2]
    ql, kl, vl = mla_project(zl[0], zl[1], zl[2], gq, w_uq, gkv, w_ukv, rope)
    qc, kc, vc = mla_project(zc[0], zc[1], zc[2], gq, w_uq, gkv, w_ukv, None)
    scale = (MLA_NOPE + MLA_ROPE) ** -0.5
    k_all = jnp.concatenate([kl, kc], axis=1)
    v_all = jnp.concatenate([vl, vc], axis=1)
    o_lat = sweep_query_blocks(lambda qi: attend(qi, k_all, v_all, scale), ql).reshape(B, S, HEADS * MLA_V)
    o_ctx = attend(qc, kc, vc, scale).reshape(B, -1, HEADS * MLA_V) if need_ctx else None
    return o_lat, o_ctx


def window_attention(zl, zc, sink, rope, need_ctx):
    B, S = zl[0].shape[:2]
    C = zc[0].shape[1]
    ql = apply_rope(split_heads(zl[0], HEADS), rope)
    kl = apply_rope(split_heads(zl[1], SWA_HKV), rope)
    vl = split_heads(zl[2], SWA_HKV)
    qc = zc[0].reshape(B, C, SWA_HKV, SWA_GROUP, HEAD_DIM)
    kc = split_heads(zc[1], SWA_HKV)
    vc = split_heads(zc[2], SWA_HKV)
    scale = HEAD_DIM ** -0.5
    sink_hg = sink.astype(jnp.float32).reshape(SWA_HKV, SWA_GROUP)
    pad = ((0, 0), (BLK, BLK), (0, 0), (0, 0))
    kp = jnp.pad(kl, pad)
    vp = jnp.pad(vl, pad)

    def band_block(i):
        q_i = lax.dynamic_slice_in_dim(ql, i * BLK, BLK, axis=1).reshape(B, BLK, SWA_HKV, SWA_GROUP, HEAD_DIM)
        k_i = lax.dynamic_slice_in_dim(kp, i * BLK, 3 * BLK, axis=1)
        v_i = lax.dynamic_slice_in_dim(vp, i * BLK, 3 * BLK, axis=1)
        qpos = i * BLK + jnp.arange(BLK)
        kpos = i * BLK - BLK + jnp.arange(3 * BLK)
        valid = ((jnp.abs(kpos[None, :] - qpos[:, None]) <= SWA_WINDOW)
                 & (kpos >= 0)[None, :] & (kpos < S)[None, :])
        s_loc = jnp.einsum('bqhgd,bkhd->bhgqk', q_i, k_i).astype(jnp.float32) * scale
        s_loc = jnp.where(valid, s_loc, NEG_INF)
        s_ctx = jnp.einsum('bqhgd,bchd->bhgqc', q_i, kc).astype(jnp.float32) * scale
        s_sink = jnp.broadcast_to(sink_hg[None, :, :, None, None], (B, SWA_HKV, SWA_GROUP, BLK, 1))
        p = softmax_f32(jnp.concatenate([s_loc, s_ctx, s_sink], axis=-1)).astype(vl.dtype)
        o = (jnp.einsum('bhgqk,bkhd->bqhgd', p[..., :3 * BLK], v_i)
             + jnp.einsum('bhgqc,bchd->bqhgd', p[..., 3 * BLK:3 * BLK + C], vc))
        return o.reshape(B, BLK, HEADS * HEAD_DIM)

    o = lax.map(band_block, jnp.arange(S // BLK))
    o_lat = jnp.moveaxis(o, 0, 1).reshape(B, S, HEADS * HEAD_DIM)
    o_ctx = None
    if need_ctx:
        s = jnp.einsum('bqhgd,bkhd->bhgqk', qc, kc).astype(jnp.float32) * scale
        s_sink = jnp.broadcast_to(sink_hg[None, :, :, None, None], (B, SWA_HKV, SWA_GROUP, C, 1))
        p = softmax_f32(jnp.concatenate([s, s_sink], axis=-1))[..., :C].astype(vc.dtype)
        o_ctx = jnp.einsum('bhgqk,bkhd->bqhgd', p, vc).reshape(B, C, HEADS * HEAD_DIM)
    return o_lat, o_ctx


def differential_attention(zl, zc, lam_params, norm_g, lam_init, rope, need_ctx):
    B, S = zl[0].shape[:2]
    C = zc[0].shape[1]
    ql = apply_rope(split_heads(zl[0], 2 * HEADS), rope).reshape(B, S, HEADS, 2, DIFF_DH)
    kl = apply_rope(split_heads(zl[1], 2 * HEADS), rope).reshape(B, S, HEADS, 2, DIFF_DH)
    vl = split_heads(zl[2], HEADS)
    qc = zc[0].reshape(B, C, HEADS, 2, DIFF_DH)
    kc = zc[1].reshape(B, C, HEADS, 2, DIFF_DH)
    vc = split_heads(zc[2], HEADS)
    lp = lam_params.astype(jnp.float32)
    lam = jnp.exp(jnp.sum(lp[0] * lp[1])) - jnp.exp(jnp.sum(lp[2] * lp[3])) + lam_init
    scale = DIFF_DH ** -0.5
    k_all = jnp.concatenate([kl, kc], axis=1)
    v_all = jnp.concatenate([vl, vc], axis=1)

    def diff_attend(q, k, v):
        s = jnp.einsum('bqhtd,bkhtd->bhtqk', q, k).astype(jnp.float32) * scale
        p = softmax_f32(s)
        a = (p[:, :, 0] - lam * p[:, :, 1]).astype(v.dtype)
        o = jnp.einsum('bhqk,bkhd->bqhd', a, v)
        return rmsnorm(o, norm_g) * (1.0 - lam_init)

    o_lat = sweep_query_blocks(lambda qi: diff_attend(qi, k_all, v_all), ql).reshape(B, S, HEADS * HEAD_DIM)
    o_ctx = diff_attend(qc, kc, vc).reshape(B, C, HEADS * HEAD_DIM) if need_ctx else None
    return o_lat, o_ctx


def sq_relu_mlp(h, w_up, w_down):
    return jnp.square(jax.nn.relu(h @ w_up)) @ w_down


def setup_inputs(seed: int = 0) -> dict:
    key = jax.random.key(seed)
    ks = jax.random.split(key, 21)
    f32 = jnp.float32

    def nrm(k, shape, scale):
        return jax.random.normal(k, shape, f32) * scale

    def gain(k, shape):
        return 1.0 + 0.05 * jax.random.normal(k, shape, f32)

    return {
        'x': nrm(ks[0], (BATCH, SEQ, D_MODEL), 1.0),
        'c': nrm(ks[1], (BATCH, D_MODEL), 1.0),
        'ctx': nrm(ks[2], (BATCH, CTX_LEN, D_MODEL), 1.0),
        'c_ctx': nrm(ks[3], (D_MODEL,), 1.0),
        'w_ada': nrm(ks[4], (DEPTH, D_MODEL, 6 * D_MODEL), 0.5 * D_MODEL ** -0.5),
        'b_ada': nrm(ks[5], (DEPTH, 6 * D_MODEL), 0.02),
        'norm_attn_g': gain(ks[6], (DEPTH, D_MODEL)),
        'w_in': nrm(ks[7], (DEPTH, D_MODEL, IN_COLS), D_MODEL ** -0.5),
        'na_rpb': nrm(ks[8], (DEPTH, HEADS, 2 * NA_WIN_R - 1, 2 * NA_WIN_C - 1), 0.1),
        'mla_q_norm_g': gain(ks[9], (DEPTH, MLA_Q_LORA)),
        'mla_w_uq': nrm(ks[10], (DEPTH, MLA_Q_LORA, HEADS * (MLA_NOPE + MLA_ROPE)), MLA_Q_LORA ** -0.5),
        'mla_kv_norm_g': gain(ks[11], (DEPTH, MLA_KV_LORA)),
        'mla_w_ukv': nrm(ks[12], (DEPTH, MLA_KV_LORA, HEADS * (MLA_NOPE + MLA_V)), MLA_KV_LORA ** -0.5),
        'swa_sink': nrm(ks[13], (DEPTH, HEADS), 0.5),
        'diff_lambda': nrm(ks[14], (DEPTH, 4, DIFF_DH), 0.1),
        'diff_norm_g': gain(ks[15], (DEPTH, HEAD_DIM)),
        'w_out': nrm(ks[16], (DEPTH, MIX_WIDTH, D_MODEL), MIX_WIDTH ** -0.5),
        'norm_mlp_g': gain(ks[17], (DEPTH, D_MODEL)),
        'w_up': nrm(ks[18], (DEPTH, D_MODEL, D_FF), D_MODEL ** -0.5),
        'w_down': nrm(ks[19], (DEPTH, D_FF, D_MODEL), D_FF ** -0.5),
        'final_norm_g': gain(ks[20], (D_MODEL,)),
    }


def reference(x, c, ctx, c_ctx, w_ada, b_ada, norm_attn_g, w_in, na_rpb, mla_q_norm_g, mla_w_uq,
              mla_kv_norm_g, mla_w_ukv, swa_sink, diff_lambda, diff_norm_g, w_out, norm_mlp_g,
              w_up, w_down, final_norm_g):
    S = x.shape[1]
    t = jnp.arange(S)
    row, col = t // GRID_W, t % GRID_W
    rope64 = axial_rope(row, col, HEAD_DIM)
    rope32 = axial_rope(row, col, MLA_ROPE)
    c_act = jax.nn.silu(c)
    cc_act = jax.nn.silu(c_ctx)
    xl, xc = x, ctx
    for l in range(DEPTH):
        need_ctx = l < DEPTH - 1
        lam_init = 0.8 - 0.6 * math.exp(-0.3 * l)
        mod_l = jnp.split((c_act @ w_ada[l] + b_ada[l])[:, None, :], 6, axis=-1)
        mod_c = jnp.split(cc_act @ w_ada[l] + b_ada[l], 6, axis=-1)
        hl = modulate(rmsnorm(xl, norm_attn_g[l]), mod_l[0], mod_l[1])
        hc = modulate(rmsnorm(xc, norm_attn_g[l]), mod_c[0], mod_c[1])
        zl = split_cols(hl @ w_in[l])
        zc = split_cols(hc @ w_in[l])
        oa_l, oa_c = neighbourhood_attention(zl[0:3], zc[0:3], na_rpb[l], need_ctx)
        ob_l, ob_c = latent_attention(zl[3:6], zc[3:6], mla_q_norm_g[l], mla_w_uq[l],
                                      mla_kv_norm_g[l], mla_w_ukv[l], rope32, need_ctx)
        oc_l, oc_c = window_attention(zl[6:9], zc[6:9], swa_sink[l], rope64, need_ctx)
        od_l, od_c = differential_attention(zl[9:12], zc[9:12], diff_lambda[l], diff_norm_g[l],
                                            lam_init, rope32, need_ctx)
        mix_l = jnp.concatenate([oa_l, ob_l, oc_l, od_l], axis=-1)
        xl = xl + mod_l[2] * (mix_l @ w_out[l])
        hl = modulate(rmsnorm(xl, norm_mlp_g[l]), mod_l[3], mod_l[4])
        xl = xl + mod_l[5] * sq_relu_mlp(hl, w_up[l], w_down[l])
        if need_ctx:
            mix_c = jnp.concatenate([oa_c, ob_c, oc_c, od_c], axis=-1)
            xc = xc + mod_c[2] * (mix_c @ w_out[l])
            hc = modulate(rmsnorm(xc, norm_mlp_g[l]), mod_c[3], mod_c[4])
            xc = xc + mod_c[5] * sq_relu_mlp(hc, w_up[l], w_down[l])
    return rmsnorm(xl, final_norm_g)
```

```python
import functools
import math

import jax
import jax.numpy as jnp
import numpy as np
from jax import lax
from jax.experimental import pallas as pl
from jax.experimental.pallas import tpu as pltpu

F32 = jnp.float32
BF16 = jnp.bfloat16

D_MODEL = 1024
GRID_W = 64
HEADS = 4
HEAD_DIM = 64
NA_WIN_R = 8
NA_WIN_C = 16
MLA_Q_LORA = 192
MLA_KV_LORA = 128
MLA_NOPE = 64
MLA_ROPE = 32
SWA_HKV = 2
SWA_GROUP = 2
SWA_WINDOW = 128
DIFF_DH = 32
D_FF = 4 * D_MODEL
ROPE_BASE = 10000.0
EPS = 1e-6
NEG_INF = -1e30

LANES = 128
VMEM_LIMIT = 56 * 1024 * 1024

ZC_DQ, ZC_DK, ZC_SQ, ZC_NQ, ZC_NK, ZC_NV, ZC_DV, ZC_MV = 0, 256, 512, 768, 1024, 1280, 1536, 1792
ZC_MQ, ZC_MK, ZC_SK, ZC_SV = 2048, 2560, 3072, 3200
Z_COLS = 3328
W_R64, W_R32, W_M, W_N, W_COLS = 0, 384, 896, 1280, 2432
T_R64, T_R32, T_B12, T_MQ, T_COLS = 0, 384, 896, 1152, 1664

NA_KROWS = 10
NA_TQ = 2 * GRID_W
NA_NLOC = NA_KROWS * GRID_W


def _params(*sem):
    return pltpu.CompilerParams(dimension_semantics=sem, vmem_limit_bytes=VMEM_LIMIT)


def _dot(a, b):
    return jnp.dot(a, b, preferred_element_type=F32)


def _dot_nt(a, b):
    return lax.dot_general(a, b, (((1,), (1,)), ((), ())), preferred_element_type=F32)


def _dot_tn(a, b):
    return lax.dot_general(a, b, (((0,), (0,)), ((), ())), preferred_element_type=F32)


def _rms_scale(x, n):
    return lax.rsqrt(jnp.sum(x * x, axis=-1, keepdims=True) * (1.0 / n) + EPS)


def _ada_kernel(c_ref, w_ref, b_ref, o_ref):
    c = c_ref[...]
    act = (c * (1.0 / (1.0 + jnp.exp(-c)))).astype(BF16)
    o_ref[0] = _dot(act, w_ref[0].astype(BF16)) + b_ref[0]


def _ada_call(c_all, w_ada, b_ada):
    depth, d, n = w_ada.shape
    rows = c_all.shape[0]
    tn = 1536
    return pl.pallas_call(
        _ada_kernel,
        grid=(depth, n // tn),
        in_specs=[
            pl.BlockSpec((rows, d), lambda l, j: (0, 0)),
            pl.BlockSpec((1, d, tn), lambda l, j: (l, 0, j)),
            pl.BlockSpec((1, 1, tn), lambda l, j: (l, 0, j)),
        ],
        out_specs=pl.BlockSpec((1, rows, tn), lambda l, j: (l, 0, j)),
        out_shape=jax.ShapeDtypeStruct((depth, rows, n), F32),
        compiler_params=_params("arbitrary", "arbitrary"),
        name="ada",
    )(c_all, w_ada, b_ada.reshape(depth, 1, n))


def _rope_blocks(z, cos_ref, sin_ref, tcol, half):
    tm = z.shape[0]
    lane = lax.broadcasted_iota(jnp.int32, (tm, LANES), 1)
    first = (lane % (2 * half)) < half
    out = []
    for j in range(z.shape[1] // LANES):
        xb = z[:, j * LANES:(j + 1) * LANES]
        rot = jnp.where(first, pltpu.roll(xb, LANES - half, 1), pltpu.roll(xb, half, 1))
        c0 = tcol + j * LANES
        out.append(xb * cos_ref[:, c0:c0 + LANES] + rot * sin_ref[:, c0:c0 + LANES])
    return out


def _inproj_kernel(x_ref, mod_ref, g_ref, cos_ref, sin_ref, w_ref, gq_ref, gkv_ref,
                   wuq_ref, wukvk_ref, wukvv_ref, e_ref, o_ref):
    x = x_ref[0]
    mod = mod_ref[0]
    h = x * _rms_scale(x, D_MODEL) * g_ref[...]
    h = (h * (1.0 + mod[1:2]) + mod[0:1]).astype(BF16)

    def put(col, val):
        o_ref[0, :, col:col + val.shape[1]] = val.astype(BF16)

    r = _rope_blocks(_dot(h, w_ref[:, W_R64:W_R32]), cos_ref, sin_ref, T_R64, HEAD_DIM // 2)
    put(ZC_SQ, r[0]); put(ZC_SQ + LANES, r[1]); put(ZC_SK, r[2])
    r = _rope_blocks(_dot(h, w_ref[:, W_R32:W_M]), cos_ref, sin_ref, T_R32, DIFF_DH // 2)
    for j in range(4):
        put(ZC_DQ + j * LANES, r[j])

    zm = _dot(h, w_ref[:, W_M:W_N])
    ckv = zm[:, 0:MLA_KV_LORA]
    b12 = zm[:, MLA_KV_LORA:]
    ckv_n = (ckv * _rms_scale(ckv, MLA_KV_LORA) * gkv_ref[...]).astype(BF16)
    lane = lax.broadcasted_iota(jnp.int32, b12.shape, 1)
    cq = jnp.where(lane < MLA_Q_LORA, b12, 0.0)
    cq_n = (cq * _rms_scale(cq, MLA_Q_LORA) * gq_ref[...]).astype(BF16)
    q = _rope_blocks(_dot(cq_n, wuq_ref[...]), cos_ref, sin_ref, T_MQ, MLA_ROPE // 2)
    for j in range(HEADS):
        put(ZC_MQ + j * LANES, q[j])
    kr = _rope_blocks(b12, cos_ref, sin_ref, T_B12, MLA_ROPE // 2)
    kr = jnp.concatenate(kr, axis=1).astype(BF16)
    put(ZC_MK, _dot(ckv_n, wukvk_ref[...]) + _dot(kr, e_ref[...]))
    put(ZC_MV, _dot(ckv_n, wukvv_ref[...]))

    zn = _dot(h, w_ref[:, W_N:W_COLS])
    put(ZC_NQ, zn[:, 0:1024])
    put(ZC_SV, zn[:, 1024:1152])


def _inproj_call(x3, mod, mod_row, g, cos, sin, table_per_tile, w, gq, gkv, wuq, wukvk, wukvv, e, tm):
    nb, n, d = x3.shape
    nt = n // tm
    const = lambda t, b: (0, 0)
    tab_map = (lambda t, b: (t, 0)) if table_per_tile else const
    mod_map = (lambda t, b: (b, 0, 0)) if mod_row is None else (lambda t, b: (mod_row, 0, 0))
    return pl.pallas_call(
        _inproj_kernel,
        grid=(nt, nb),
        in_specs=[
            pl.BlockSpec((1, tm, d), lambda t, b: (b, t, 0)),
            pl.BlockSpec((1, 6, d), mod_map),
            pl.BlockSpec((1, d), const),
            pl.BlockSpec((tm, T_COLS), tab_map),
            pl.BlockSpec((tm, T_COLS), tab_map),
            pl.BlockSpec(w.shape, const, pipeline_mode=pl.Buffered(1)),
            pl.BlockSpec(gq.shape, const),
            pl.BlockSpec(gkv.shape, const),
            pl.BlockSpec(wuq.shape, const),
            pl.BlockSpec(wukvk.shape, const),
            pl.BlockSpec(wukvv.shape, const),
            pl.BlockSpec(e.shape, const),
        ],
        out_specs=pl.BlockSpec((1, tm, Z_COLS), lambda t, b: (b, t, 0)),
        out_shape=jax.ShapeDtypeStruct((nb, n, Z_COLS), BF16),
        compiler_params=_params("arbitrary", "arbitrary"),
        name="inproj",
    )(x3, mod, g, cos, sin, w, gq, gkv, wuq, wukvk, wukvv, e)


def _lane_mask(q, lo, width):
    lane = lax.broadcasted_iota(jnp.int32, q.shape, 1)
    keep = jnp.where(lane >= lo, lane, LANES) < lo + width
    return jnp.where(keep, q.astype(F32), 0.0).astype(BF16)


def _softmax_t(scores, extra=None):
    m = jnp.max(scores[0], axis=0, keepdims=True)
    for s in scores[1:]:
        m = jnp.maximum(m, jnp.max(s, axis=0, keepdims=True))
    if extra is not None:
        m = jnp.maximum(m, extra)
    ps = [jnp.exp(s - m) for s in scores]
    l = jnp.sum(ps[0], axis=0, keepdims=True)
    for p in ps[1:]:
        l = l + jnp.sum(p, axis=0, keepdims=True)
    if extra is not None:
        l = l + jnp.exp(extra - m)
    return ps, 1.0 / l


def _attend_t(q_m, ks, vs, biases=None, extra=None):
    scores = [_dot_nt(k, q_m) for k in ks]
    if biases is not None:
        scores = [s if b is None else s + b for s, b in zip(scores, biases)]
    ps, rl = _softmax_t(scores, extra)
    o = _dot_tn(vs[0], ps[0].astype(BF16))
    for v, p in zip(vs[1:], ps[1:]):
        o = o + _dot_tn(v, p.astype(BF16))
    return o * rl


def _diff_lambda(lp, lam_init):
    a = jnp.sum(lp[0:1] * lp[1:2], axis=-1, keepdims=True)
    b = jnp.sum(lp[2:3] * lp[3:4], axis=-1, keepdims=True)
    return jnp.exp(a) - jnp.exp(b) + lam_init


def _diff_attend_t(qblocks, kblocks, vblocks, lam):
    outs = []
    for h in range(HEADS):
        j = h // 2
        sm = []
        for t in range(2):
            q_m = _lane_mask(qblocks[j], ((h % 2) * 2 + t) * DIFF_DH, DIFF_DH)
            sm.append(_softmax_t([_dot_nt(kb[j], q_m) for kb in kblocks]))
        (p1, r1), (p2, r2) = sm
        r2 = r2 * lam
        o = None
        for i, vb in enumerate(vblocks):
            a = (p1[i] * r1 - p2[i] * r2).astype(BF16)
            pv = _dot_tn(vb[j], a)
            o = pv if o is None else o + pv
        o = o[(h % 2) * HEAD_DIM:(h % 2 + 1) * HEAD_DIM]
        ms = jnp.sum(o * o, axis=0, keepdims=True) * (1.0 / HEAD_DIM)
        outs.append(o * lax.rsqrt(ms + EPS))
    return jnp.concatenate(outs, axis=0)


def _mla_kernel(q_ref, k_ref, v_ref, kc_ref, vc_ref, o_ref):
    outs = []
    for h in range(HEADS):
        sl = slice(h * LANES, (h + 1) * LANES)
        vsl = slice((h // 2) * LANES, (h // 2 + 1) * LANES)
        o = _attend_t(q_ref[0, :, sl], [k_ref[0, :, sl], kc_ref[0, :, sl]],
                      [v_ref[0, :, vsl], vc_ref[0, :, vsl]])
        outs.append(o[(h % 2) * HEAD_DIM:(h % 2 + 1) * HEAD_DIM])
    o_ref[0] = jnp.concatenate(outs, axis=0).T.astype(BF16)


def _mla_call(zl, zc, tq):
    b, s, _ = zl.shape
    c = zc.shape[1]
    return pl.pallas_call(
        _mla_kernel,
        grid=(b, s // tq),
        in_specs=[
            pl.BlockSpec((1, tq, 512), lambda i, j: (i, j, ZC_MQ // 512)),
            pl.BlockSpec((1, s, 512), lambda i, j: (i, 0, ZC_MK // 512)),
            pl.BlockSpec((1, s, 256), lambda i, j: (i, 0, ZC_MV // 256)),
            pl.BlockSpec((1, c, 512), lambda i, j: (i, 0, ZC_MK // 512)),
            pl.BlockSpec((1, c, 256), lambda i, j: (i, 0, ZC_MV // 256)),
        ],
        out_specs=pl.BlockSpec((1, tq, 256), lambda i, j: (i, j, 0)),
        out_shape=jax.ShapeDtypeStruct((b, s, 256), BF16),
        compiler_params=_params("arbitrary", "arbitrary"),
        name="mla",
    )(zl, zl, zl, zc, zc)


def _diff_kernel(lam_init, q_ref, k_ref, v_ref, kc_ref, vc_ref, lp_ref, gain_ref, o_ref):
    lam = _diff_lambda(lp_ref[...], lam_init)
    blk = lambda ref, j: ref[0, :, j * LANES:(j + 1) * LANES]
    o = _diff_attend_t([blk(q_ref, 0), blk(q_ref, 1)],
                       [[blk(k_ref, 0), blk(k_ref, 1)], [blk(kc_ref, 0), blk(kc_ref, 1)]],
                       [[blk(v_ref, 0), blk(v_ref, 1)], [blk(vc_ref, 0), blk(vc_ref, 1)]], lam)
    o_ref[0] = (o.T * gain_ref[...]).astype(BF16)


def _diff_call(zl, zc, lp, gain, lam_init, tq):
    b, s, _ = zl.shape
    c = zc.shape[1]
    return pl.pallas_call(
        functools.partial(_diff_kernel, lam_init),
        grid=(b, s // tq),
        in_specs=[
            pl.BlockSpec((1, tq, 256), lambda i, j: (i, j, ZC_DQ // 256)),
            pl.BlockSpec((1, s, 256), lambda i, j: (i, 0, ZC_DK // 256)),
            pl.BlockSpec((1, s, 256), lambda i, j: (i, 0, ZC_DV // 256)),
            pl.BlockSpec((1, c, 256), lambda i, j: (i, 0, ZC_DK // 256)),
            pl.BlockSpec((1, c, 256), lambda i, j: (i, 0, ZC_DV // 256)),
            pl.BlockSpec(lp.shape, lambda i, j: (0, 0)),
            pl.BlockSpec(gain.shape, lambda i, j: (0, 0)),
        ],
        out_specs=pl.BlockSpec((1, tq, 256), lambda i, j: (i, j, 0)),
        out_shape=jax.ShapeDtypeStruct((b, s, 256), BF16),
        compiler_params=_params("arbitrary", "arbitrary"),
        name="diff",
    )(zl, zl, zl, zc, zc, lp, gain)


def _swa_kernel(tq, q_ref, k_ref, v_ref, kc_ref, vc_ref, sink_ref, o_ref):
    s_len = k_ref.shape[1]
    win = tq + 2 * SWA_WINDOW
    i = pl.program_id(1)
    start = pl.multiple_of(jnp.clip(i * tq - SWA_WINDOW, 0, s_len - win), SWA_WINDOW)
    kl = k_ref[0, pl.ds(start, win), :]
    vl = v_ref[0, pl.ds(start, win), :]
    kc = kc_ref[0]
    vc = vc_ref[0]
    kpos = start + lax.broadcasted_iota(jnp.int32, (win, tq), 0)
    qpos = i * tq + lax.broadcasted_iota(jnp.int32, (win, tq), 1)
    valid = jnp.abs(kpos - qpos) <= SWA_WINDOW
    outs = {}
    for g in range(SWA_GROUP):
        qb = q_ref[0, :, g * LANES:(g + 1) * LANES]
        for hk in range(SWA_HKV):
            q_m = _lane_mask(qb, hk * HEAD_DIM, HEAD_DIM)
            sink = jnp.full((1, 1), sink_ref[hk * SWA_GROUP + g], F32)
            s_loc = jnp.where(valid, _dot_nt(kl, q_m), NEG_INF)
            ps, rl = _softmax_t([s_loc, _dot_nt(kc, q_m)], sink)
            o = (_dot_tn(vl, ps[0].astype(BF16)) + _dot_tn(vc, ps[1].astype(BF16))) * rl
            outs[hk * SWA_GROUP + g] = o[hk * HEAD_DIM:(hk + 1) * HEAD_DIM]
    o_ref[0] = jnp.concatenate([outs[h] for h in range(HEADS)], axis=0).T.astype(BF16)


def _swa_call(zl, zc, sink, tq):
    b, s, _ = zl.shape
    c = zc.shape[1]
    return pl.pallas_call(
        functools.partial(_swa_kernel, tq),
        grid=(b, s // tq),
        in_specs=[
            pl.BlockSpec((1, tq, 256), lambda i, j: (i, j, ZC_SQ // 256)),
            pl.BlockSpec((1, s, LANES), lambda i, j: (i, 0, ZC_SK // LANES)),
            pl.BlockSpec((1, s, LANES), lambda i, j: (i, 0, ZC_SV // LANES)),
            pl.BlockSpec((1, c, LANES), lambda i, j: (i, 0, ZC_SK // LANES)),
            pl.BlockSpec((1, c, LANES), lambda i, j: (i, 0, ZC_SV // LANES)),
            pl.BlockSpec(memory_space=pltpu.SMEM),
        ],
        out_specs=pl.BlockSpec((1, tq, 256), lambda i, j: (i, j, 0)),
        out_shape=jax.ShapeDtypeStruct((b, s, 256), BF16),
        compiler_params=_params("arbitrary", "arbitrary"),
        name="swa",
    )(zl, zl, zl, zc, zc, sink)


def _na_kernel(rows_n, q_ref, k_ref, v_ref, kc_ref, vc_ref, bias_ref, o_ref):
    rb = pl.program_id(1)
    nrb = rows_n // 2
    start = pl.multiple_of(
        jnp.clip(2 * rb - NA_WIN_R // 2, 0, rows_n - NA_KROWS) * GRID_W, 2 * GRID_W)
    tid = jnp.minimum(rb, 2) + jnp.maximum(rb - (nrb - 3), 0)
    kl = k_ref[0, pl.ds(start, NA_NLOC), :]
    vl = v_ref[0, pl.ds(start, NA_NLOC), :]
    outs = []
    for h in range(HEADS):
        sl = slice((h // 2) * LANES, (h // 2 + 1) * LANES)
        q_m = _lane_mask(q_ref[0, :, sl], (h % 2) * HEAD_DIM, HEAD_DIM)
        o = _attend_t(q_m, [kl[:, sl], kc_ref[0, :, sl]], [vl[:, sl], vc_ref[0, :, sl]],
                      biases=[bias_ref[tid, h], None])
        outs.append(o[(h % 2) * HEAD_DIM:(h % 2 + 1) * HEAD_DIM])
    o_ref[0] = jnp.concatenate(outs, axis=0).T.astype(BF16)


def _na_call(zl, zc, bias):
    b, s, _ = zl.shape
    c = zc.shape[1]
    rows_n = s // GRID_W
    return pl.pallas_call(
        functools.partial(_na_kernel, rows_n),
        grid=(b, rows_n // 2),
        in_specs=[
            pl.BlockSpec((1, NA_TQ, 256), lambda i, j: (i, j, ZC_NQ // 256)),
            pl.BlockSpec((1, s, 256), lambda i, j: (i, 0, ZC_NK // 256)),
            pl.BlockSpec((1, s, 256), lambda i, j: (i, 0, ZC_NV // 256)),
            pl.BlockSpec((1, c, 256), lambda i, j: (i, 0, ZC_NK // 256)),
            pl.BlockSpec((1, c, 256), lambda i, j: (i, 0, ZC_NV // 256)),
            pl.BlockSpec(bias.shape, lambda i, j: (0, 0, 0, 0)),
        ],
        out_specs=pl.BlockSpec((1, NA_TQ, 256), lambda i, j: (i, j, 0)),
        out_shape=jax.ShapeDtypeStruct((b, s, 256), BF16),
        compiler_params=_params("arbitrary", "arbitrary"),
        name="na",
    )(zl, zl, zl, zc, zc, bias)


def _ctx_kernel(lam_init, z_ref, lp_ref, gain_ref, sink_ref, oa_ref, ob_ref, oc_ref, od_ref):
    def blk(col, j=0):
        return z_ref[0, :, col + j * LANES:col + (j + 1) * LANES]

    outs = []
    for h in range(HEADS):
        q_m = _lane_mask(blk(ZC_NQ, h // 2), (h % 2) * HEAD_DIM, HEAD_DIM)
        o = _attend_t(q_m, [blk(ZC_NK, h // 2)], [blk(ZC_NV, h // 2)])
        outs.append(o[(h % 2) * HEAD_DIM:(h % 2 + 1) * HEAD_DIM])
    oa_ref[0] = jnp.concatenate(outs, axis=0).T.astype(BF16)

    outs = []
    for h in range(HEADS):
        o = _attend_t(blk(ZC_MQ, h), [blk(ZC_MK, h)], [blk(ZC_MV, h // 2)])
        outs.append(o[(h % 2) * HEAD_DIM:(h % 2 + 1) * HEAD_DIM])
    ob_ref[0] = jnp.concatenate(outs, axis=0).T.astype(BF16)

    outs = {}
    for g in range(SWA_GROUP):
        for hk in range(SWA_HKV):
            q_m = _lane_mask(blk(ZC_SQ, g), hk * HEAD_DIM, HEAD_DIM)
            sink = jnp.full((1, 1), sink_ref[hk * SWA_GROUP + g], F32)
            o = _attend_t(q_m, [blk(ZC_SK)], [blk(ZC_SV)], extra=sink)
            outs[hk * SWA_GROUP + g] = o[hk * HEAD_DIM:(hk + 1) * HEAD_DIM]
    oc_ref[0] = jnp.concatenate([outs[h] for h in range(HEADS)], axis=0).T.astype(BF16)

    lam = _diff_lambda(lp_ref[...], lam_init)
    o = _diff_attend_t([blk(ZC_DQ, 0), blk(ZC_DQ, 1)], [[blk(ZC_DK, 0), blk(ZC_DK, 1)]],
                       [[blk(ZC_DV, 0), blk(ZC_DV, 1)]], lam)
    od_ref[0] = (o.T * gain_ref[...]).astype(BF16)


def _ctx_call(zc, lp, gain, sink, lam_init):
    b, c, _ = zc.shape
    out = jax.ShapeDtypeStruct((b, c, 256), BF16)
    ospec = pl.BlockSpec((1, c, 256), lambda i: (i, 0, 0))
    return pl.pallas_call(
        functools.partial(_ctx_kernel, lam_init),
        grid=(b,),
        in_specs=[
            pl.BlockSpec((1, c, Z_COLS), lambda i: (i, 0, 0)),
            pl.BlockSpec(lp.shape, lambda i: (0, 0)),
            pl.BlockSpec(gain.shape, lambda i: (0, 0)),
            pl.BlockSpec(memory_space=pltpu.SMEM),
        ],
        out_specs=[ospec] * 4,
        out_shape=[out] * 4,
        compiler_params=_params("arbitrary"),
        name="ctx_attn",
    )(zc, lp, gain, sink)


def _mlp_kernel(final, x_ref, ma_ref, mb_ref, mc_ref, md_ref, mod_ref, g_ref, gf_ref,
                wout_ref, wup_ref, wdn_ref, o_ref):
    x = x_ref[0]
    mod = mod_ref[0]
    attn = None
    for i, m_ref in enumerate((ma_ref, mb_ref, mc_ref, md_ref)):
        part = _dot(m_ref[0], wout_ref[i * 256:(i + 1) * 256, :])
        attn = part if attn is None else attn + part
    x = x + mod[2:3] * attn
    h = x * _rms_scale(x, D_MODEL) * g_ref[...]
    h = (h * (1.0 + mod[4:5]) + mod[3:4]).astype(BF16)
    acc = None
    ck = 1024
    for c in range(D_FF // ck):
        u = jnp.maximum(_dot(h, wup_ref[:, c * ck:(c + 1) * ck]), 0.0)
        part = _dot((u * u).astype(BF16), wdn_ref[c * ck:(c + 1) * ck, :])
        acc = part if acc is None else acc + part
    x = x + mod[5:6] * acc
    if final:
        x = x * _rms_scale(x, D_MODEL) * gf_ref[...]
    o_ref[0] = x


def _mlp_call(x3, mixes, mod, mod_row, g, gf, wout, wup, wdn, tm, final):
    nb, n, d = x3.shape
    const = lambda t, b: (0, 0)
    tok = lambda t, b: (b, t, 0)
    mod_map = (lambda t, b: (b, 0, 0)) if mod_row is None else (lambda t, b: (mod_row, 0, 0))
    single = pl.Buffered(1)
    return pl.pallas_call(
        functools.partial(_mlp_kernel, final),
        grid=(n // tm, nb),
        in_specs=[pl.BlockSpec((1, tm, d), tok)]
        + [pl.BlockSpec((1, tm, 256), tok)] * 4
        + [
            pl.BlockSpec((1, 6, d), mod_map),
            pl.BlockSpec((1, d), const),
            pl.BlockSpec((1, d), const),
            pl.BlockSpec(wout.shape, const, pipeline_mode=single),
            pl.BlockSpec(wup.shape, const, pipeline_mode=single),
            pl.BlockSpec(wdn.shape, const, pipeline_mode=single),
        ],
        out_specs=pl.BlockSpec((1, tm, d), tok),
        out_shape=jax.ShapeDtypeStruct((nb, n, d), F32),
        compiler_params=_params("arbitrary", "arbitrary"),
        name="mlp",
    )(x3, *mixes, mod, g, gf, wout, wup, wdn)


def _rope_tables(s_len):
    t = np.arange(s_len)
    row, col = t // GRID_W, t % GRID_W

    def axial(dim):
        n_freq = dim // 4
        freqs = jnp.asarray(ROPE_BASE, F32) ** (-jnp.arange(n_freq, dtype=F32) / n_freq)
        ang = jnp.concatenate([jnp.asarray(row, F32)[:, None] * freqs,
                               jnp.asarray(col, F32)[:, None] * freqs], axis=-1)
        return jnp.cos(ang), jnp.sin(ang)

    def group(cs, n_groups, scale):
        cos, sin = cs
        c = jnp.concatenate([cos, cos], axis=-1) * scale
        s = jnp.concatenate([-sin, sin], axis=-1) * scale
        return jnp.tile(c, (1, n_groups)), jnp.tile(s, (1, n_groups))

    r64, r32 = axial(HEAD_DIM), axial(MLA_ROPE)
    ones = lambda n, v=1.0: jnp.full((s_len, n), v, F32)
    zeros = lambda n: jnp.zeros((s_len, n), F32)
    mla_scale = (MLA_NOPE + MLA_ROPE) ** -0.5
    mq_c, mq_s = group(r32, 1, mla_scale)
    parts = [
        group(r64, 4, HEAD_DIM ** -0.5), group(r64, 2, 1.0),
        group(r32, 8, DIFF_DH ** -0.5), group(r32, 8, 1.0),
        (ones(MLA_Q_LORA), zeros(MLA_Q_LORA)), group(r32, 1, 1.0), (ones(32), zeros(32)),
    ] + [(jnp.concatenate([ones(MLA_NOPE, mla_scale), mq_c, ones(32, mla_scale)], axis=-1),
          jnp.concatenate([zeros(MLA_NOPE), mq_s, zeros(32)], axis=-1))] * HEADS
    cos = jnp.concatenate([p[0] for p in parts], axis=-1)
    sin = jnp.concatenate([p[1] for p in parts], axis=-1)
    return cos, sin, cos[0:1]


def _permute_w_in(w_in):
    offs = np.cumsum([0, 256, 256, 256, MLA_Q_LORA, MLA_KV_LORA, MLA_ROPE, 256, 128, 128, 256, 256, 256])
    sec = lambda i: w_in[..., offs[i]:offs[i + 1]]
    sq = sec(6)
    sq = sq.reshape(sq.shape[:-1] + (SWA_HKV, SWA_GROUP, HEAD_DIM))
    sq = jnp.swapaxes(sq, -3, -2).reshape(w_in.shape[:-1] + (256,))
    pad = jnp.zeros(w_in.shape[:-1] + (32,), w_in.dtype)
    cols = [sq, sec(7), sec(9), sec(10), sec(4), sec(3), sec(5), pad,
            sec(0) * (HEAD_DIM ** -0.5), sec(1), sec(2), sec(11), sec(8)]
    return jnp.concatenate(cols, axis=-1).astype(BF16)


def _mla_weights(w_uq, w_ukv):
    depth = w_uq.shape[0]
    uq = w_uq.reshape(depth, MLA_Q_LORA, HEADS, MLA_NOPE + MLA_ROPE)
    uq = jnp.pad(uq, ((0, 0), (0, 256 - MLA_Q_LORA), (0, 0), (0, LANES - MLA_NOPE - MLA_ROPE)))
    uq = uq.reshape(depth, 256, HEADS * LANES).astype(BF16)
    ukv = w_ukv.reshape(depth, MLA_KV_LORA, HEADS, 2, HEAD_DIM)
    uk = jnp.pad(ukv[:, :, :, 0], ((0, 0), (0, 0), (0, 0), (0, LANES - MLA_NOPE)))
    uk = uk.reshape(depth, MLA_KV_LORA, HEADS * LANES).astype(BF16)
    uv = ukv[:, :, :, 1].reshape(depth, MLA_KV_LORA, HEADS * HEAD_DIM).astype(BF16)
    e = np.zeros((256, HEADS * LANES), np.float32)
    for h in range(HEADS):
        for j in range(MLA_ROPE):
            e[MLA_Q_LORA + j, h * LANES + MLA_NOPE + j] = 1.0
    return uq, uk, uv, jnp.asarray(e, BF16)


def _na_bias_tables(rpb, rows_n):
    kr_n = min(NA_WIN_R, rows_n)
    col = np.arange(GRID_W)
    c0 = np.clip(col - NA_WIN_C // 2, 0, GRID_W - NA_WIN_C)
    col_ok = (col[None, :] >= c0[:, None]) & (col[None, :] < c0[:, None] + NA_WIN_C)
    dc_idx = np.clip(col[None, :] - col[:, None], 1 - NA_WIN_C, NA_WIN_C - 1) + (NA_WIN_C - 1)
    nrb = rows_n // 2
    tabs = []
    for rb in (0, 1, 2, nrb - 2, nrb - 1):
        start = int(np.clip(2 * rb - NA_WIN_R // 2, 0, rows_n - NA_KROWS))
        kr = start + np.arange(NA_KROWS)
        qr = 2 * rb + np.arange(2)
        r0 = np.clip(qr - kr_n // 2, 0, rows_n - kr_n)
        row_ok = (kr[:, None] >= r0[None, :]) & (kr[:, None] < r0[None, :] + kr_n)
        dr_idx = np.clip(kr[:, None] - qr[None, :] + (NA_WIN_R - 1), 0, 2 * NA_WIN_R - 2)
        dr_b = np.broadcast_to(dr_idx[:, None, :, None], (NA_KROWS, GRID_W, 2, GRID_W))
        dc_b = np.broadcast_to(dc_idx.T[None, :, None, :], (NA_KROWS, GRID_W, 2, GRID_W))
        ok = row_ok[:, None, :, None] & col_ok.T[None, :, None, :]
        tab = jnp.where(ok, rpb[:, :, dr_b, dc_b].astype(F32), NEG_INF)
        tabs.append(tab.reshape(rpb.shape[:2] + (NA_NLOC, NA_TQ)))
    return jnp.stack(tabs, axis=1)


def kernel(x, c, ctx, c_ctx, w_ada, b_ada, norm_attn_g, w_in, na_rpb, mla_q_norm_g, mla_w_uq,
           mla_kv_norm_g, mla_w_ukv, swa_sink, diff_lambda, diff_norm_g, w_out, norm_mlp_g,
           w_up, w_down, final_norm_g):
    b, s, d = x.shape
    c_len = ctx.shape[1]
    depth = w_in.shape[0]
    rows_n = s // GRID_W

    w_in_p = _permute_w_in(w_in)
    uq, uk, uv, e_mat = _mla_weights(mla_w_uq, mla_w_ukv)
    gq = jnp.pad(mla_q_norm_g, ((0, 0), (0, 256 - MLA_Q_LORA)))[:, None, :]
    gkv = mla_kv_norm_g[:, None, :]
    w_out_b, w_up_b, w_dn_b = w_out.astype(BF16), w_up.astype(BF16), w_down.astype(BF16)
    na_bias = _na_bias_tables(na_rpb, rows_n)
    cos, sin, cos_ctx = _rope_tables(s)
    tm_l, tm_c = 512, min(512, b * c_len)
    cos_c = jnp.broadcast_to(cos_ctx, (tm_c, T_COLS))
    sin_c = jnp.zeros((tm_c, T_COLS), F32)

    c_rows = -(-(b + 1) // 8) * 8
    c_all = jnp.zeros((c_rows, d), F32).at[:b].set(c).at[b].set(c_ctx)
    mod = _ada_call(c_all, w_ada, b_ada).reshape(depth, c_rows, 6, d)

    xl = x
    xc = ctx.reshape(1, b * c_len, d)
    for l in range(depth):
        need_ctx = l < depth - 1
        lam_init = 0.8 - 0.6 * math.exp(-0.3 * l)
        g_attn = norm_attn_g[l][None, :]
        g_mlp = norm_mlp_g[l][None, :]
        gf = final_norm_g[None, :]
        d_gain = (jnp.tile(diff_norm_g[l], HEADS) * (1.0 - lam_init))[None, :]
        proj = (w_in_p[l], gq[l], gkv[l], uq[l], uk[l], uv[l], e_mat)
        zl = _inproj_call(xl, mod[l], None, g_attn, cos, sin, True, *proj, tm=tm_l)
        zc = _inproj_call(xc, mod[l], b, g_attn, cos_c, sin_c, False, *proj, tm=tm_c)
        zc = zc.reshape(b, c_len, Z_COLS)
        mixes = (
            _na_call(zl, zc, na_bias[l]),
            _mla_call(zl, zc, tq=256),
            _swa_call(zl, zc, swa_sink[l], tq=256),
            _diff_call(zl, zc, diff_lambda[l], d_gain, lam_init, tq=256),
        )
        wts = (w_out_b[l], w_up_b[l], w_dn_b[l])
        xl = _mlp_call(xl, mixes, mod[l], None, g_mlp, gf, *wts, tm=tm_l, final=not need_ctx)
        if need_ctx:
            mixes_c = _ctx_call(zc, diff_lambda[l], d_gain, swa_sink[l], lam_init)
            mixes_c = [m.reshape(1, b * c_len, 256) for m in mixes_c]
            xc = _mlp_call(xc, mixes_c, mod[l], b, g_mlp, gf, *wts, tm=tm_c, final=False)
    return xl
```

```python
import functools
import math

import jax
import jax.numpy as jnp
import numpy as np
from jax import lax
from jax.experimental import pallas as pl
from jax.experimental.pallas import tpu as pltpu

F32 = jnp.float32
BF16 = jnp.bfloat16

D_MODEL = 1024
GRID_W = 64
HEADS = 4
HEAD_DIM = 64
NA_WIN_R = 8
NA_WIN_C = 16
MLA_Q_LORA = 192
MLA_KV_LORA = 128
MLA_NOPE = 64
MLA_ROPE = 32
SWA_HKV = 2
SWA_GROUP = 2
SWA_WINDOW = 128
DIFF_DH = 32
D_FF = 4 * D_MODEL
ROPE_BASE = 10000.0
EPS = 1e-6
NEG_INF = -1e30
LOG2E = math.log2(math.e)

LANES = 128
VMEM_LIMIT = 56 * 1024 * 1024

ZC_DQ, ZC_DK, ZC_SQ, ZC_NQ, ZC_NK, ZC_NV, ZC_DV, ZC_MV = 0, 256, 512, 768, 1024, 1280, 1536, 1792
ZC_MQ, ZC_MK, ZC_SK, ZC_SV = 2048, 2560, 3072, 3200
Z_COLS = 3328
W_R64, W_R32, W_M, W_N, W_COLS = 0, 384, 896, 1280, 2432
T_R64, T_R32, T_B12, T_MQ, T_COLS = 0, 384, 896, 1152, 1664

NA_KROWS = 10
NA_TQ = 2 * GRID_W
NA_NLOC = NA_KROWS * GRID_W
KEY_CHUNK = 256


def _params(*sem):
    return pltpu.CompilerParams(dimension_semantics=sem, vmem_limit_bytes=VMEM_LIMIT)


def _dot(a, b):
    return jnp.dot(a, b, preferred_element_type=F32)


def _dot_nt(a, b):
    return lax.dot_general(a, b, (((1,), (1,)), ((), ())), preferred_element_type=F32)


def _dot_tn(a, b):
    return lax.dot_general(a, b, (((0,), (0,)), ((), ())), preferred_element_type=F32)


def _rms_scale(x, n):
    return lax.rsqrt(jnp.sum(x * x, axis=-1, keepdims=True) * (1.0 / n) + EPS)


def _ada_kernel(c_ref, w_ref, b_ref, o_ref):
    c = c_ref[...]
    act = (c * (1.0 / (1.0 + jnp.exp(-c)))).astype(BF16)
    o_ref[0] = _dot(act, w_ref[0].astype(BF16)) + b_ref[0]


def _ada_call(c_all, w_ada, b_ada):
    depth, d, n = w_ada.shape
    rows = c_all.shape[0]
    tn = 1536
    return pl.pallas_call(
        _ada_kernel,
        grid=(depth, n // tn),
        in_specs=[
            pl.BlockSpec((rows, d), lambda l, j: (0, 0)),
            pl.BlockSpec((1, d, tn), lambda l, j: (l, 0, j)),
            pl.BlockSpec((1, 1, tn), lambda l, j: (l, 0, j)),
        ],
        out_specs=pl.BlockSpec((1, rows, tn), lambda l, j: (l, 0, j)),
        out_shape=jax.ShapeDtypeStruct((depth, rows, n), F32),
        compiler_params=_params("arbitrary", "arbitrary"),
        name="ada",
    )(c_all, w_ada, b_ada.reshape(depth, 1, n))


def _rope_blocks(z, cos_ref, sin_ref, tcol, half):
    tm = z.shape[0]
    lane = lax.broadcasted_iota(jnp.int32, (tm, LANES), 1)
    first = (lane % (2 * half)) < half
    out = []
    for j in range(z.shape[1] // LANES):
        xb = z[:, j * LANES:(j + 1) * LANES]
        rot = jnp.where(first, pltpu.roll(xb, LANES - half, 1), pltpu.roll(xb, half, 1))
        c0 = tcol + j * LANES
        out.append(xb * cos_ref[:, c0:c0 + LANES] + rot * sin_ref[:, c0:c0 + LANES])
    return out


def _inproj_kernel(x_ref, mod_ref, g_ref, cos_ref, sin_ref, w_ref, gq_ref, gkv_ref,
                   wuq_ref, wukvk_ref, wukvv_ref, e_ref, o_ref):
    x = x_ref[0]
    mod = mod_ref[0]
    h = x * _rms_scale(x, D_MODEL) * g_ref[...]
    h = (h * (1.0 + mod[1:2]) + mod[0:1]).astype(BF16)

    def put(col, val):
        o_ref[0, :, col:col + val.shape[1]] = val.astype(BF16)

    r = _rope_blocks(_dot(h, w_ref[:, W_R64:W_R32]), cos_ref, sin_ref, T_R64, HEAD_DIM // 2)
    put(ZC_SQ, r[0]); put(ZC_SQ + LANES, r[1]); put(ZC_SK, r[2])
    r = _rope_blocks(_dot(h, w_ref[:, W_R32:W_M]), cos_ref, sin_ref, T_R32, DIFF_DH // 2)
    for j in range(4):
        put(ZC_DQ + j * LANES, r[j])

    zm = _dot(h, w_ref[:, W_M:W_N])
    ckv = zm[:, 0:MLA_KV_LORA]
    b12 = zm[:, MLA_KV_LORA:]
    ckv_n = (ckv * _rms_scale(ckv, MLA_KV_LORA) * gkv_ref[...]).astype(BF16)
    lane = lax.broadcasted_iota(jnp.int32, b12.shape, 1)
    cq = jnp.where(lane < MLA_Q_LORA, b12, 0.0)
    cq_n = (cq * _rms_scale(cq, MLA_Q_LORA) * gq_ref[...]).astype(BF16)
    q = _rope_blocks(_dot(cq_n, wuq_ref[...]), cos_ref, sin_ref, T_MQ, MLA_ROPE // 2)
    for j in range(HEADS):
        put(ZC_MQ + j * LANES, q[j])
    kr = _rope_blocks(b12, cos_ref, sin_ref, T_B12, MLA_ROPE // 2)
    kr = jnp.concatenate(kr, axis=1).astype(BF16)
    put(ZC_MK, _dot(ckv_n, wukvk_ref[...]) + _dot(kr, e_ref[...]))
    put(ZC_MV, _dot(ckv_n, wukvv_ref[...]))

    zn = _dot(h, w_ref[:, W_N:W_COLS])
    put(ZC_NQ, zn[:, 0:256] * (HEAD_DIM ** -0.5 * LOG2E))
    put(ZC_NK, zn[:, 256:1024])
    put(ZC_SV, zn[:, 1024:1152])


def _inproj_call(x3, mod, mod_row, g, cos, sin, table_per_tile, w, gq, gkv, wuq, wukvk, wukvv, e, tm):
    nb, n, d = x3.shape
    nt = n // tm
    const = lambda t, b: (0, 0)
    tab_map = (lambda t, b: (t, 0)) if table_per_tile else const
    mod_map = (lambda t, b: (b, 0, 0)) if mod_row is None else (lambda t, b: (mod_row, 0, 0))
    return pl.pallas_call(
        _inproj_kernel,
        grid=(nt, nb),
        in_specs=[
            pl.BlockSpec((1, tm, d), lambda t, b: (b, t, 0)),
            pl.BlockSpec((1, 6, d), mod_map),
            pl.BlockSpec((1, d), const),
            pl.BlockSpec((tm, T_COLS), tab_map),
            pl.BlockSpec((tm, T_COLS), tab_map),
            pl.BlockSpec(w.shape, const, pipeline_mode=pl.Buffered(1)),
            pl.BlockSpec(gq.shape, const),
            pl.BlockSpec(gkv.shape, const),
            pl.BlockSpec(wuq.shape, const),
            pl.BlockSpec(wukvk.shape, const),
            pl.BlockSpec(wukvv.shape, const),
            pl.BlockSpec(e.shape, const),
        ],
        out_specs=pl.BlockSpec((1, tm, Z_COLS), lambda t, b: (b, t, 0)),
        out_shape=jax.ShapeDtypeStruct((nb, n, Z_COLS), BF16),
        compiler_params=_params("arbitrary", "arbitrary"),
        name="inproj",
    )(x3, mod, g, cos, sin, w, gq, gkv, wuq, wukvk, wukvv, e)


def _lane_mask(q, lo, width):
    lane = lax.broadcasted_iota(jnp.int32, q.shape, 1)
    keep = jnp.where(lane >= lo, lane, LANES) < lo + width
    return jnp.where(keep, q.astype(F32), 0.0).astype(BF16)


def _pipelined_softmax(n_units, chunks, score_of, extra_of, pv_of, s_scr, p_scr):
    outs = [None] * n_units
    m_of, l_of = {}, {}
    for t in range(n_units + 2):
        if t >= 2:
            outs[t - 2] = pv_of(t - 2, p_scr.at[t % 2]) * (1.0 / l_of[t - 2])
        mx, l = None, None
        for c, (off, rows) in enumerate(chunks):
            if t < n_units:
                s = score_of(t, c)
                s_scr[t % 2, off:off + rows, :] = s
                cm = jnp.max(s, axis=0, keepdims=True)
                mx = cm if mx is None else jnp.maximum(mx, cm)
            if 1 <= t <= n_units:
                u = t - 1
                p = jnp.exp2(s_scr[u % 2, off:off + rows, :] - m_of[u])
                cl = jnp.sum(p, axis=0, keepdims=True)
                l = cl if l is None else l + cl
                p_scr[u % 2, off:off + rows, :] = p.astype(BF16)
        if t < n_units:
            e = extra_of(t)
            m_of[t] = mx if e is None else jnp.maximum(mx, e)
        if 1 <= t <= n_units:
            e = extra_of(t - 1)
            l_of[t - 1] = l if e is None else l + jnp.exp2(e - m_of[t - 1])
    return outs


def _attend_t(q_m, k, v, extra=None):
    s = _dot_nt(k, q_m)
    m = jnp.max(s, axis=0, keepdims=True)
    if extra is not None:
        m = jnp.maximum(m, extra)
    p = jnp.exp2(s - m)
    l = jnp.sum(p, axis=0, keepdims=True)
    if extra is not None:
        l = l + jnp.exp2(extra - m)
    return _dot_tn(v, p.astype(BF16)) * (1.0 / l)


def _head_rows(o, h):
    return o[(h % 2) * HEAD_DIM:(h % 2 + 1) * HEAD_DIM]


def _diff_lambda(lp, lam_init):
    a = jnp.sum(lp[0:1] * lp[1:2], axis=-1, keepdims=True)
    b = jnp.sum(lp[2:3] * lp[3:4], axis=-1, keepdims=True)
    return jnp.exp(a) - jnp.exp(b) + lam_init


def _diff_combine(outs, lam):
    res = []
    for h in range(HEADS):
        o = _head_rows(outs[2 * h], h) - lam * _head_rows(outs[2 * h + 1], h)
        ms = jnp.sum(o * o, axis=0, keepdims=True) * (1.0 / HEAD_DIM)
        res.append(o * lax.rsqrt(ms + EPS))
    return jnp.concatenate(res, axis=0)


def _key_chunks(total, size):
    return [(o, min(size, total - o)) for o in range(0, total, size)]


def _softmax_scratch(nk, tq):
    return [pltpu.VMEM((2, nk, tq), F32), pltpu.VMEM((2, nk, tq), BF16)]


def _mla_kernel(q_ref, k_ref, v_ref, kc_ref, vc_ref, o_ref, s_scr, p_scr):
    s_len, c_len = k_ref.shape[1], kc_ref.shape[1]
    chunks = _key_chunks(s_len, KEY_CHUNK) + [(s_len + o, r) for o, r in _key_chunks(c_len, KEY_CHUNK)]
    qs = [q_ref[0, :, h * LANES:(h + 1) * LANES] for h in range(HEADS)]

    def score_of(u, c):
        off, rows = chunks[c]
        sl = slice(u * LANES, (u + 1) * LANES)
        if off < s_len:
            return _dot_nt(k_ref[0, off:off + rows, sl], qs[u])
        return _dot_nt(kc_ref[0, off - s_len:off - s_len + rows, sl], qs[u])

    def pv_of(u, p_ref):
        sl = slice((u // 2) * LANES, (u // 2 + 1) * LANES)
        return (_dot_tn(v_ref[0, :, sl], p_ref[0:s_len, :])
                + _dot_tn(vc_ref[0, :, sl], p_ref[s_len:s_len + c_len, :]))

    outs = _pipelined_softmax(HEADS, chunks, score_of, lambda u: None, pv_of, s_scr, p_scr)
    o = jnp.concatenate([_head_rows(outs[h], h) for h in range(HEADS)], axis=0)
    o_ref[0] = o.T.astype(BF16)


def _mla_call(zl, zc, tq):
    b, s, _ = zl.shape
    c = zc.shape[1]
    return pl.pallas_call(
        _mla_kernel,
        grid=(b, s // tq),
        in_specs=[
            pl.BlockSpec((1, tq, 512), lambda i, j: (i, j, ZC_MQ // 512)),
            pl.BlockSpec((1, s, 512), lambda i, j: (i, 0, ZC_MK // 512)),
            pl.BlockSpec((1, s, 256), lambda i, j: (i, 0, ZC_MV // 256)),
            pl.BlockSpec((1, c, 512), lambda i, j: (i, 0, ZC_MK // 512)),
            pl.BlockSpec((1, c, 256), lambda i, j: (i, 0, ZC_MV // 256)),
        ],
        out_specs=pl.BlockSpec((1, tq, 256), lambda i, j: (i, j, 0)),
        out_shape=jax.ShapeDtypeStruct((b, s, 256), BF16),
        scratch_shapes=_softmax_scratch(s + c, tq),
        compiler_params=_params("arbitrary", "arbitrary"),
        name="mla",
    )(zl, zl, zl, zc, zc)


def _diff_kernel(lam_init, q_ref, k_ref, v_ref, kc_ref, vc_ref, lp_ref, gain_ref, o_ref,
                 s_scr, p_scr):
    s_len, c_len = k_ref.shape[1], kc_ref.shape[1]
    chunks = _key_chunks(s_len, KEY_CHUNK) + [(s_len + o, r) for o, r in _key_chunks(c_len, KEY_CHUNK)]
    qs = [_lane_mask(q_ref[0, :, (u // 4) * LANES:(u // 4 + 1) * LANES], (u % 4) * DIFF_DH, DIFF_DH)
          for u in range(2 * HEADS)]

    def score_of(u, c):
        off, rows = chunks[c]
        sl = slice((u // 4) * LANES, (u // 4 + 1) * LANES)
        if off < s_len:
            return _dot_nt(k_ref[0, off:off + rows, sl], qs[u])
        return _dot_nt(kc_ref[0, off - s_len:off - s_len + rows, sl], qs[u])

    def pv_of(u, p_ref):
        sl = slice((u // 4) * LANES, (u // 4 + 1) * LANES)
        return (_dot_tn(v_ref[0, :, sl], p_ref[0:s_len, :])
                + _dot_tn(vc_ref[0, :, sl], p_ref[s_len:s_len + c_len, :]))

    outs = _pipelined_softmax(2 * HEADS, chunks, score_of, lambda u: None, pv_of, s_scr, p_scr)
    o = _diff_combine(outs, _diff_lambda(lp_ref[...], lam_init))
    o_ref[0] = (o.T * gain_ref[...]).astype(BF16)


def _diff_call(zl, zc, lp, gain, lam_init, tq):
    b, s, _ = zl.shape
    c = zc.shape[1]
    return pl.pallas_call(
        functools.partial(_diff_kernel, lam_init),
        grid=(b, s // tq),
        in_specs=[
            pl.BlockSpec((1, tq, 256), lambda i, j: (i, j, ZC_DQ // 256)),
            pl.BlockSpec((1, s, 256), lambda i, j: (i, 0, ZC_DK // 256)),
            pl.BlockSpec((1, s, 256), lambda i, j: (i, 0, ZC_DV // 256)),
            pl.BlockSpec((1, c, 256), lambda i, j: (i, 0, ZC_DK // 256)),
            pl.BlockSpec((1, c, 256), lambda i, j: (i, 0, ZC_DV // 256)),
            pl.BlockSpec(lp.shape, lambda i, j: (0, 0)),
            pl.BlockSpec(gain.shape, lambda i, j: (0, 0)),
        ],
        out_specs=pl.BlockSpec((1, tq, 256), lambda i, j: (i, j, 0)),
        out_shape=jax.ShapeDtypeStruct((b, s, 256), BF16),
        scratch_shapes=_softmax_scratch(s + c, tq),
        compiler_params=_params("arbitrary", "arbitrary"),
        name="diff",
    )(zl, zl, zl, zc, zc, lp, gain)


def _swa_kernel(tq, q_ref, k_ref, v_ref, kc_ref, vc_ref, sink_ref, o_ref, s_scr, p_scr, mask_scr):
    s_len, c_len = k_ref.shape[1], kc_ref.shape[1]
    win = tq + 2 * SWA_WINDOW
    i = pl.program_id(1)
    start = pl.multiple_of(jnp.clip(i * tq - SWA_WINDOW, 0, s_len - win), SWA_WINDOW)
    kpos = start + lax.broadcasted_iota(jnp.int32, (win, tq), 0)
    qpos = i * tq + lax.broadcasted_iota(jnp.int32, (win, tq), 1)
    mask_scr[...] = jnp.where(jnp.abs(kpos - qpos) <= SWA_WINDOW, 0.0, NEG_INF)
    chunks = _key_chunks(win, KEY_CHUNK) + [(win + o, r) for o, r in _key_chunks(c_len, KEY_CHUNK)]
    qs = [_lane_mask(q_ref[0, :, (u // SWA_HKV) * LANES:(u // SWA_HKV + 1) * LANES],
                     (u % SWA_HKV) * HEAD_DIM, HEAD_DIM) for u in range(HEADS)]
    head_of = lambda u: (u % SWA_HKV) * SWA_GROUP + u // SWA_HKV

    def score_of(u, c):
        off, rows = chunks[c]
        if off < win:
            k = k_ref[0, pl.ds(start + off, rows), :]
            return _dot_nt(k, qs[u]) + mask_scr[off:off + rows, :]
        return _dot_nt(kc_ref[0, off - win:off - win + rows, :], qs[u])

    def extra_of(u):
        return jnp.full((1, 1), sink_ref[head_of(u)] * LOG2E, F32)

    def pv_of(u, p_ref):
        return (_dot_tn(v_ref[0, pl.ds(start, win), :], p_ref[0:win, :])
                + _dot_tn(vc_ref[0], p_ref[win:win + c_len, :]))

    outs = _pipelined_softmax(HEADS, chunks, score_of, extra_of, pv_of, s_scr, p_scr)
    by_head = {head_of(u): outs[u][(u % SWA_HKV) * HEAD_DIM:(u % SWA_HKV + 1) * HEAD_DIM]
               for u in range(HEADS)}
    o_ref[0] = jnp.concatenate([by_head[h] for h in range(HEADS)], axis=0).T.astype(BF16)


def _swa_call(zl, zc, sink, tq):
    b, s, _ = zl.shape
    c = zc.shape[1]
    win = tq + 2 * SWA_WINDOW
    return pl.pallas_call(
        functools.partial(_swa_kernel, tq),
        grid=(b, s // tq),
        in_specs=[
            pl.BlockSpec((1, tq, 256), lambda i, j: (i, j, ZC_SQ // 256)),
            pl.BlockSpec((1, s, LANES), lambda i, j: (i, 0, ZC_SK // LANES)),
            pl.BlockSpec((1, s, LANES), lambda i, j: (i, 0, ZC_SV // LANES)),
            pl.BlockSpec((1, c, LANES), lambda i, j: (i, 0, ZC_SK // LANES)),
            pl.BlockSpec((1, c, LANES), lambda i, j: (i, 0, ZC_SV // LANES)),
            pl.BlockSpec(memory_space=pltpu.SMEM),
        ],
        out_specs=pl.BlockSpec((1, tq, 256), lambda i, j: (i, j, 0)),
        out_shape=jax.ShapeDtypeStruct((b, s, 256), BF16),
        scratch_shapes=_softmax_scratch(win + c, tq) + [pltpu.VMEM((win, tq), F32)],
        compiler_params=_params("arbitrary", "arbitrary"),
        name="swa",
    )(zl, zl, zl, zc, zc, sink)


def _na_kernel(rows_n, q_ref, k_ref, v_ref, kc_ref, vc_ref, bias_ref, o_ref, s_scr, p_scr):
    c_len = kc_ref.shape[1]
    rb = pl.program_id(1)
    nrb = rows_n // 2
    start = pl.multiple_of(
        jnp.clip(2 * rb - NA_WIN_R // 2, 0, rows_n - NA_KROWS) * GRID_W, 2 * GRID_W)
    tid = jnp.minimum(rb, 2) + jnp.maximum(rb - (nrb - 3), 0)
    chunks = _key_chunks(NA_NLOC, KEY_CHUNK) + [(NA_NLOC + o, r) for o, r in _key_chunks(c_len, KEY_CHUNK)]
    blk = lambda h: slice((h // 2) * LANES, (h // 2 + 1) * LANES)
    qs = [_lane_mask(q_ref[0, :, blk(h)], (h % 2) * HEAD_DIM, HEAD_DIM) for h in range(HEADS)]

    def score_of(u, c):
        off, rows = chunks[c]
        if off < NA_NLOC:
            k = k_ref[0, pl.ds(start + off, rows), blk(u)]
            return _dot_nt(k, qs[u]) + bias_ref[tid, u, off:off + rows, :]
        return _dot_nt(kc_ref[0, off - NA_NLOC:off - NA_NLOC + rows, blk(u)], qs[u])

    def pv_of(u, p_ref):
        return (_dot_tn(v_ref[0, pl.ds(start, NA_NLOC), blk(u)], p_ref[0:NA_NLOC, :])
                + _dot_tn(vc_ref[0, :, blk(u)], p_ref[NA_NLOC:NA_NLOC + c_len, :]))

    outs = _pipelined_softmax(HEADS, chunks, score_of, lambda u: None, pv_of, s_scr, p_scr)
    o = jnp.concatenate([_head_rows(outs[h], h) for h in range(HEADS)], axis=0)
    o_ref[0] = o.T.astype(BF16)


def _na_call(zl, zc, bias):
    b, s, _ = zl.shape
    c = zc.shape[1]
    rows_n = s // GRID_W
    return pl.pallas_call(
        functools.partial(_na_kernel, rows_n),
        grid=(b, rows_n // 2),
        in_specs=[
            pl.BlockSpec((1, NA_TQ, 256), lambda i, j: (i, j, ZC_NQ // 256)),
            pl.BlockSpec((1, s, 256), lambda i, j: (i, 0, ZC_NK // 256)),
            pl.BlockSpec((1, s, 256), lambda i, j: (i, 0, ZC_NV // 256)),
            pl.BlockSpec((1, c, 256), lambda i, j: (i, 0, ZC_NK // 256)),
            pl.BlockSpec((1, c, 256), lambda i, j: (i, 0, ZC_NV // 256)),
            pl.BlockSpec(bias.shape, lambda i, j: (0, 0, 0, 0), pipeline_mode=pl.Buffered(1)),
        ],
        out_specs=pl.BlockSpec((1, NA_TQ, 256), lambda i, j: (i, j, 0)),
        out_shape=jax.ShapeDtypeStruct((b, s, 256), BF16),
        scratch_shapes=_softmax_scratch(NA_NLOC + c, NA_TQ),
        compiler_params=_params("arbitrary", "arbitrary"),
        name="na",
    )(zl, zl, zl, zc, zc, bias)


def _ctx_kernel(lam_init, z_ref, lp_ref, gain_ref, sink_ref, oa_ref, ob_ref, oc_ref, od_ref):
    def blk(col, j=0):
        return z_ref[0, :, col + j * LANES:col + (j + 1) * LANES]

    outs = []
    for h in range(HEADS):
        q_m = _lane_mask(blk(ZC_NQ, h // 2), (h % 2) * HEAD_DIM, HEAD_DIM)
        outs.append(_head_rows(_attend_t(q_m, blk(ZC_NK, h // 2), blk(ZC_NV, h // 2)), h))
    oa_ref[0] = jnp.concatenate(outs, axis=0).T.astype(BF16)

    outs = []
    for h in range(HEADS):
        outs.append(_head_rows(_attend_t(blk(ZC_MQ, h), blk(ZC_MK, h), blk(ZC_MV, h // 2)), h))
    ob_ref[0] = jnp.concatenate(outs, axis=0).T.astype(BF16)

    outs = {}
    for g in range(SWA_GROUP):
        for hk in range(SWA_HKV):
            q_m = _lane_mask(blk(ZC_SQ, g), hk * HEAD_DIM, HEAD_DIM)
            sink = jnp.full((1, 1), sink_ref[hk * SWA_GROUP + g] * LOG2E, F32)
            o = _attend_t(q_m, blk(ZC_SK), blk(ZC_SV), extra=sink)
            outs[hk * SWA_GROUP + g] = o[hk * HEAD_DIM:(hk + 1) * HEAD_DIM]
    oc_ref[0] = jnp.concatenate([outs[h] for h in range(HEADS)], axis=0).T.astype(BF16)

    outs = []
    for u in range(2 * HEADS):
        q_m = _lane_mask(blk(ZC_DQ, u // 4), (u % 4) * DIFF_DH, DIFF_DH)
        outs.append(_attend_t(q_m, blk(ZC_DK, u // 4), blk(ZC_DV, u // 4)))
    o = _diff_combine(outs, _diff_lambda(lp_ref[...], lam_init))
    od_ref[0] = (o.T * gain_ref[...]).astype(BF16)


def _ctx_call(zc, lp, gain, sink, lam_init):
    b, c, _ = zc.shape
    out = jax.ShapeDtypeStruct((b, c, 256), BF16)
    ospec = pl.BlockSpec((1, c, 256), lambda i: (i, 0, 0))
    return pl.pallas_call(
        functools.partial(_ctx_kernel, lam_init),
        grid=(b,),
        in_specs=[
            pl.BlockSpec((1, c, Z_COLS), lambda i: (i, 0, 0)),
            pl.BlockSpec(lp.shape, lambda i: (0, 0)),
            pl.BlockSpec(gain.shape, lambda i: (0, 0)),
            pl.BlockSpec(memory_space=pltpu.SMEM),
        ],
        out_specs=[ospec] * 4,
        out_shape=[out] * 4,
        compiler_params=_params("arbitrary"),
        name="ctx_attn",
    )(zc, lp, gain, sink)


def _mlp_kernel(final, x_ref, ma_ref, mb_ref, mc_ref, md_ref, mod_ref, g_ref, gf_ref,
                wout_ref, wup_ref, wdn_ref, o_ref):
    x = x_ref[0]
    mod = mod_ref[0]
    attn = None
    for i, m_ref in enumerate((ma_ref, mb_ref, mc_ref, md_ref)):
        part = _dot(m_ref[0], wout_ref[i * 256:(i + 1) * 256, :])
        attn = part if attn is None else attn + part
    x = x + mod[2:3] * attn
    h = x * _rms_scale(x, D_MODEL) * g_ref[...]
    h = (h * (1.0 + mod[4:5]) + mod[3:4]).astype(BF16)
    acc = None
    ck = 1024
    for c in range(D_FF // ck):
        u = jnp.maximum(_dot(h, wup_ref[:, c * ck:(c + 1) * ck]), 0.0)
        part = _dot((u * u).astype(BF16), wdn_ref[c * ck:(c + 1) * ck, :])
        acc = part if acc is None else acc + part
    x = x + mod[5:6] * acc
    if final:
        x = x * _rms_scale(x, D_MODEL) * gf_ref[...]
    o_ref[0] = x


def _mlp_call(x3, mixes, mod, mod_row, g, gf, wout, wup, wdn, tm, final):
    nb, n, d = x3.shape
    const = lambda t, b: (0, 0)
    tok = lambda t, b: (b, t, 0)
    mod_map = (lambda t, b: (b, 0, 0)) if mod_row is None else (lambda t, b: (mod_row, 0, 0))
    single = pl.Buffered(1)
    return pl.pallas_call(
        functools.partial(_mlp_kernel, final),
        grid=(n // tm, nb),
        in_specs=[pl.BlockSpec((1, tm, d), tok)]
        + [pl.BlockSpec((1, tm, 256), tok)] * 4
        + [
            pl.BlockSpec((1, 6, d), mod_map),
            pl.BlockSpec((1, d), const),
            pl.BlockSpec((1, d), const),
            pl.BlockSpec(wout.shape, const, pipeline_mode=single),
            pl.BlockSpec(wup.shape, const, pipeline_mode=single),
            pl.BlockSpec(wdn.shape, const, pipeline_mode=single),
        ],
        out_specs=pl.BlockSpec((1, tm, d), tok),
        out_shape=jax.ShapeDtypeStruct((nb, n, d), F32),
        compiler_params=_params("arbitrary", "arbitrary"),
        name="mlp",
    )(x3, *mixes, mod, g, gf, wout, wup, wdn)


def _rope_tables(s_len):
    t = np.arange(s_len)
    row, col = t // GRID_W, t % GRID_W

    def axial(dim):
        n_freq = dim // 4
        freqs = jnp.asarray(ROPE_BASE, F32) ** (-jnp.arange(n_freq, dtype=F32) / n_freq)
        ang = jnp.concatenate([jnp.asarray(row, F32)[:, None] * freqs,
                               jnp.asarray(col, F32)[:, None] * freqs], axis=-1)
        return jnp.cos(ang), jnp.sin(ang)

    def group(cs, n_groups, scale):
        cos, sin = cs
        c = jnp.concatenate([cos, cos], axis=-1) * scale
        s = jnp.concatenate([-sin, sin], axis=-1) * scale
        return jnp.tile(c, (1, n_groups)), jnp.tile(s, (1, n_groups))

    r64, r32 = axial(HEAD_DIM), axial(MLA_ROPE)
    ones = lambda n, v=1.0: jnp.full((s_len, n), v, F32)
    zeros = lambda n: jnp.zeros((s_len, n), F32)
    mla_scale = (MLA_NOPE + MLA_ROPE) ** -0.5 * LOG2E
    mq_c, mq_s = group(r32, 1, mla_scale)
    parts = [
        group(r64, 4, HEAD_DIM ** -0.5 * LOG2E), group(r64, 2, 1.0),
        group(r32, 8, DIFF_DH ** -0.5 * LOG2E), group(r32, 8, 1.0),
        (ones(MLA_Q_LORA), zeros(MLA_Q_LORA)), group(r32, 1, 1.0), (ones(32), zeros(32)),
    ] + [(jnp.concatenate([ones(MLA_NOPE, mla_scale), mq_c, ones(32, mla_scale)], axis=-1),
          jnp.concatenate([zeros(MLA_NOPE), mq_s, zeros(32)], axis=-1))] * HEADS
    cos = jnp.concatenate([p[0] for p in parts], axis=-1)
    sin = jnp.concatenate([p[1] for p in parts], axis=-1)
    return cos, sin, cos[0:1]


def _permute_w_in(w_in):
    offs = np.cumsum([0, 256, 256, 256, MLA_Q_LORA, MLA_KV_LORA, MLA_ROPE, 256, 128, 128, 256, 256, 256])
    sec = lambda i: w_in[..., offs[i]:offs[i + 1]]
    sq = sec(6)
    sq = sq.reshape(sq.shape[:-1] + (SWA_HKV, SWA_GROUP, HEAD_DIM))
    sq = jnp.swapaxes(sq, -3, -2).reshape(w_in.shape[:-1] + (256,))
    pad = jnp.zeros(w_in.shape[:-1] + (32,), w_in.dtype)
    cols = [sq, sec(7), sec(9), sec(10), sec(4), sec(3), sec(5), pad,
            sec(0), sec(1), sec(2), sec(11), sec(8)]
    return jnp.concatenate(cols, axis=-1).astype(BF16)


def _mla_weights(w_uq, w_ukv):
    depth = w_uq.shape[0]
    uq = w_uq.reshape(depth, MLA_Q_LORA, HEADS, MLA_NOPE + MLA_ROPE)
    uq = jnp.pad(uq, ((0, 0), (0, 256 - MLA_Q_LORA), (0, 0), (0, LANES - MLA_NOPE - MLA_ROPE)))
    uq = uq.reshape(depth, 256, HEADS * LANES).astype(BF16)
    ukv = w_ukv.reshape(depth, MLA_KV_LORA, HEADS, 2, HEAD_DIM)
    uk = jnp.pad(ukv[:, :, :, 0], ((0, 0), (0, 0), (0, 0), (0, LANES - MLA_NOPE)))
    uk = uk.reshape(depth, MLA_KV_LORA, HEADS * LANES).astype(BF16)
    uv = ukv[:, :, :, 1].reshape(depth, MLA_KV_LORA, HEADS * HEAD_DIM).astype(BF16)
    e = np.zeros((256, HEADS * LANES), np.float32)
    for h in range(HEADS):
        for j in range(MLA_ROPE):
            e[MLA_Q_LORA + j, h * LANES + MLA_NOPE + j] = 1.0
    return uq, uk, uv, jnp.asarray(e, BF16)


def _na_bias_tables(rpb, rows_n):
    kr_n = min(NA_WIN_R, rows_n)
    col = np.arange(GRID_W)
    c0 = np.clip(col - NA_WIN_C // 2, 0, GRID_W - NA_WIN_C)
    col_ok_t = ((col[None, :] >= c0[:, None]) & (col[None, :] < c0[:, None] + NA_WIN_C)).T
    edge = GRID_W - NA_WIN_C
    ext = jnp.concatenate([jnp.repeat(rpb[..., :1], edge, axis=-1), rpb,
                           jnp.repeat(rpb[..., -1:], edge, axis=-1)], axis=-1)[..., ::-1]
    toe = jnp.stack([ext[..., GRID_W - 1 - kc:2 * GRID_W - 1 - kc] for kc in range(GRID_W)], axis=-2)
    toe = jnp.where(col_ok_t, toe.astype(F32) * LOG2E, NEG_INF)
    neg = jnp.full(toe.shape[:2] + (GRID_W, GRID_W), NEG_INF, F32)
    nrb = rows_n // 2
    tabs = []
    for rb in (0, 1, 2, nrb - 2, nrb - 1):
        start = int(np.clip(2 * rb - NA_WIN_R // 2, 0, rows_n - NA_KROWS))
        rows = []
        for j in range(NA_KROWS):
            pieces = []
            for i in range(2):
                kr, qr = start + j, 2 * rb + i
                r0 = int(np.clip(qr - kr_n // 2, 0, rows_n - kr_n))
                ok = r0 <= kr < r0 + kr_n
                pieces.append(toe[:, :, kr - qr + NA_WIN_R - 1] if ok else neg)
            rows.append(jnp.concatenate(pieces, axis=-1))
        tabs.append(jnp.concatenate(rows, axis=-2))
    return jnp.stack(tabs, axis=1)


def kernel(x, c, ctx, c_ctx, w_ada, b_ada, norm_attn_g, w_in, na_rpb, mla_q_norm_g, mla_w_uq,
           mla_kv_norm_g, mla_w_ukv, swa_sink, diff_lambda, diff_norm_g, w_out, norm_mlp_g,
           w_up, w_down, final_norm_g):
    b, s, d = x.shape
    c_len = ctx.shape[1]
    depth = w_in.shape[0]
    rows_n = s // GRID_W

    w_in_p = _permute_w_in(w_in)
    uq, uk, uv, e_mat = _mla_weights(mla_w_uq, mla_w_ukv)
    gq = jnp.pad(mla_q_norm_g, ((0, 0), (0, 256 - MLA_Q_LORA)))[:, None, :]
    gkv = mla_kv_norm_g[:, None, :]
    w_out_b, w_up_b, w_dn_b = w_out.astype(BF16), w_up.astype(BF16), w_down.astype(BF16)
    na_bias = _na_bias_tables(na_rpb, rows_n)
    cos, sin, cos_ctx = _rope_tables(s)
    tm_l, tm_c = 512, min(512, b * c_len)
    cos_c = jnp.broadcast_to(cos_ctx, (tm_c, T_COLS))
    sin_c = jnp.zeros((tm_c, T_COLS), F32)

    c_rows = -(-(b + 1) // 8) * 8
    c_all = jnp.zeros((c_rows, d), F32).at[:b].set(c).at[b].set(c_ctx)
    mod = _ada_call(c_all, w_ada, b_ada).reshape(depth, c_rows, 6, d)

    xl = x
    xc = ctx.reshape(1, b * c_len, d)
    for l in range(depth):
        need_ctx = l < depth - 1
        lam_init = 0.8 - 0.6 * math.exp(-0.3 * l)
        g_attn = norm_attn_g[l][None, :]
        g_mlp = norm_mlp_g[l][None, :]
        gf = final_norm_g[None, :]
        d_gain = (jnp.tile(diff_norm_g[l], HEADS) * (1.0 - lam_init))[None, :]
        proj = (w_in_p[l], gq[l], gkv[l], uq[l], uk[l], uv[l], e_mat)
        zl = _inproj_call(xl, mod[l], None, g_attn, cos, sin, True, *proj, tm=tm_l)
        zc = _inproj_call(xc, mod[l], b, g_attn, cos_c, sin_c, False, *proj, tm=tm_c)
        zc = zc.reshape(b, c_len, Z_COLS)
        mixes = (
            _na_call(zl, zc, na_bias[l]),
            _mla_call(zl, zc, tq=256),
            _swa_call(zl, zc, swa_sink[l], tq=256),
            _diff_call(zl, zc, diff_lambda[l], d_gain, lam_init, tq=256),
        )
        wts = (w_out_b[l], w_up_b[l], w_dn_b[l])
        xl = _mlp_call(xl, mixes, mod[l], None, g_mlp, gf, *wts, tm=tm_l, final=not need_ctx)
        if need_ctx:
            mixes_c = _ctx_call(zc, diff_lambda[l], d_gain, swa_sink[l], lam_init)
            mixes_c = [m.reshape(1, b * c_len, 256) for m in mixes_c]
            xc = _mlp_call(xc, mixes_c, mod[l], b, g_mlp, gf, *wts, tm=tm_c, final=False)
    return xl
```

```python
import functools
import math

import jax
import jax.numpy as jnp
import numpy as np
from jax import lax
from jax.experimental import pallas as pl
from jax.experimental.pallas import tpu as pltpu

F32 = jnp.float32
BF16 = jnp.bfloat16

D_MODEL = 1024
GRID_W = 64
HEADS = 4
HEAD_DIM = 64
NA_WIN_R = 8
NA_WIN_C = 16
MLA_Q_LORA = 192
MLA_KV_LORA = 128
MLA_NOPE = 64
MLA_ROPE = 32
SWA_HKV = 2
SWA_GROUP = 2
SWA_WINDOW = 128
DIFF_DH = 32
D_FF = 4 * D_MODEL
ROPE_BASE = 10000.0
EPS = 1e-6
NEG_INF = -1e30
LOG2E = math.log2(math.e)

LANES = 128
VMEM_LIMIT = 56 * 1024 * 1024

ZC_DQ, ZC_DK, ZC_SQ, ZC_NQ, ZC_NK, ZC_NV, ZC_DV, ZC_MV = 0, 256, 512, 768, 1024, 1280, 1536, 1792
ZC_MQ, ZC_MK, ZC_SK, ZC_SV = 2048, 2560, 3072, 3200
Z_COLS = 3328
W_R64, W_R32, W_M, W_N, W_COLS = 0, 384, 896, 1280, 2432
T_R64, T_R32, T_B12, T_MQ, T_COLS = 0, 384, 896, 1152, 1664

NA_KROWS = 10
NA_TQ = 2 * GRID_W
NA_NLOC = NA_KROWS * GRID_W
KEY_CHUNK = 256
UNIT_Q = 256


def _params(*sem):
    return pltpu.CompilerParams(dimension_semantics=sem, vmem_limit_bytes=VMEM_LIMIT)


def _dot(a, b):
    return jnp.dot(a, b, preferred_element_type=F32)


def _dot_nt(a, b):
    return lax.dot_general(a, b, (((1,), (1,)), ((), ())), preferred_element_type=F32)


def _dot_tn(a, b):
    return lax.dot_general(a, b, (((0,), (0,)), ((), ())), preferred_element_type=F32)


def _rms_scale(x, n):
    return lax.rsqrt(jnp.sum(x * x, axis=-1, keepdims=True) * (1.0 / n) + EPS)


def _ada_kernel(c_ref, w_ref, b_ref, o_ref):
    c = c_ref[...]
    act = (c * (1.0 / (1.0 + jnp.exp(-c)))).astype(BF16)
    o_ref[0] = _dot(act, w_ref[0].astype(BF16)) + b_ref[0]


def _ada_call(c_all, w_ada, b_ada):
    depth, d, n = w_ada.shape
    rows = c_all.shape[0]
    tn = 1536
    return pl.pallas_call(
        _ada_kernel,
        grid=(depth, n // tn),
        in_specs=[
            pl.BlockSpec((rows, d), lambda l, j: (0, 0)),
            pl.BlockSpec((1, d, tn), lambda l, j: (l, 0, j)),
            pl.BlockSpec((1, 1, tn), lambda l, j: (l, 0, j)),
        ],
        out_specs=pl.BlockSpec((1, rows, tn), lambda l, j: (l, 0, j)),
        out_shape=jax.ShapeDtypeStruct((depth, rows, n), F32),
        compiler_params=_params("arbitrary", "arbitrary"),
        name="ada",
    )(c_all, w_ada, b_ada.reshape(depth, 1, n))


def _rope_blocks(z, cos_ref, sin_ref, tcol, half):
    tm = z.shape[0]
    lane = lax.broadcasted_iota(jnp.int32, (tm, LANES), 1)
    first = (lane % (2 * half)) < half
    out = []
    for j in range(z.shape[1] // LANES):
        xb = z[:, j * LANES:(j + 1) * LANES]
        rot = jnp.where(first, pltpu.roll(xb, LANES - half, 1), pltpu.roll(xb, half, 1))
        c0 = tcol + j * LANES
        out.append(xb * cos_ref[:, c0:c0 + LANES] + rot * sin_ref[:, c0:c0 + LANES])
    return out


def _inproj_kernel(x_ref, mod_ref, g_ref, cos_ref, sin_ref, w_ref, gq_ref, gkv_ref,
                   wuq_ref, wukvk_ref, wukvv_ref, e_ref, o_ref):
    x = x_ref[0]
    mod = mod_ref[0]
    h = x * _rms_scale(x, D_MODEL) * g_ref[...]
    h = (h * (1.0 + mod[1:2]) + mod[0:1]).astype(BF16)

    def put(col, val):
        o_ref[0, :, col:col + val.shape[1]] = val.astype(BF16)

    z64 = _dot(h, w_ref[:, W_R64:W_R32])
    z32 = _dot(h, w_ref[:, W_R32:W_M])
    r = _rope_blocks(z64, cos_ref, sin_ref, T_R64, HEAD_DIM // 2)
    put(ZC_SQ, r[0]); put(ZC_SQ + LANES, r[1]); put(ZC_SK, r[2])
    zm = _dot(h, w_ref[:, W_M:W_N])
    r = _rope_blocks(z32, cos_ref, sin_ref, T_R32, DIFF_DH // 2)
    for j in range(4):
        put(ZC_DQ + j * LANES, r[j])
    zn = _dot(h, w_ref[:, W_N:W_COLS])

    ckv = zm[:, 0:MLA_KV_LORA]
    b12 = zm[:, MLA_KV_LORA:]
    ckv_n = (ckv * _rms_scale(ckv, MLA_KV_LORA) * gkv_ref[...]).astype(BF16)
    lane = lax.broadcasted_iota(jnp.int32, b12.shape, 1)
    cq = jnp.where(lane < MLA_Q_LORA, b12, 0.0)
    cq_n = (cq * _rms_scale(cq, MLA_Q_LORA) * gq_ref[...]).astype(BF16)
    zq = _dot(cq_n, wuq_ref[...])
    put(ZC_NQ, zn[:, 0:256] * (HEAD_DIM ** -0.5 * LOG2E))
    put(ZC_NK, zn[:, 256:1024])
    put(ZC_SV, zn[:, 1024:1152])
    q = _rope_blocks(zq, cos_ref, sin_ref, T_MQ, MLA_ROPE // 2)
    for j in range(HEADS):
        put(ZC_MQ + j * LANES, q[j])
    kr = _rope_blocks(b12, cos_ref, sin_ref, T_B12, MLA_ROPE // 2)
    kr = jnp.concatenate(kr, axis=1).astype(BF16)
    put(ZC_MK, _dot(ckv_n, wukvk_ref[...]) + _dot(kr, e_ref[...]))
    put(ZC_MV, _dot(ckv_n, wukvv_ref[...]))


def _inproj_call(x3, mod, mod_row, g, cos, sin, table_per_tile, w, gq, gkv, wuq, wukvk, wukvv, e, tm):
    nb, n, d = x3.shape
    nt = n // tm
    const = lambda t, b: (0, 0)
    tab_map = (lambda t, b: (t, 0)) if table_per_tile else const
    mod_map = (lambda t, b: (b, 0, 0)) if mod_row is None else (lambda t, b: (mod_row, 0, 0))
    return pl.pallas_call(
        _inproj_kernel,
        grid=(nt, nb),
        in_specs=[
            pl.BlockSpec((1, tm, d), lambda t, b: (b, t, 0)),
            pl.BlockSpec((1, 6, d), mod_map),
            pl.BlockSpec((1, d), const),
            pl.BlockSpec((tm, T_COLS), tab_map),
            pl.BlockSpec((tm, T_COLS), tab_map),
            pl.BlockSpec(w.shape, const, pipeline_mode=pl.Buffered(1)),
            pl.BlockSpec(gq.shape, const),
            pl.BlockSpec(gkv.shape, const),
            pl.BlockSpec(wuq.shape, const),
            pl.BlockSpec(wukvk.shape, const),
            pl.BlockSpec(wukvv.shape, const),
            pl.BlockSpec(e.shape, const),
        ],
        out_specs=pl.BlockSpec((1, tm, Z_COLS), lambda t, b: (b, t, 0)),
        out_shape=jax.ShapeDtypeStruct((nb, n, Z_COLS), BF16),
        compiler_params=_params("arbitrary", "arbitrary"),
        name="inproj",
    )(x3, mod, g, cos, sin, w, gq, gkv, wuq, wukvk, wukvv, e)


def _lane_mask(q, lo, width):
    lane = lax.broadcasted_iota(jnp.int32, q.shape, 1)
    keep = jnp.where(lane >= lo, lane, LANES) < lo + width
    return jnp.where(keep, q.astype(F32), 0.0).astype(BF16)


def _pipelined_softmax(n_units, chunks, score_of, extra_of, pv_of, s_scr, p_scr):
    outs = [None] * n_units
    m_of, l_of = {}, {}
    for t in range(n_units + 2):
        if t >= 2:
            outs[t - 2] = pv_of(t - 2, p_scr.at[t % 2]) * (1.0 / l_of[t - 2])
        mx, l = None, None
        if t < n_units:
            for c, (off, rows) in enumerate(chunks):
                s = score_of(t, c)
                s_scr[t % 2, off:off + rows, :] = s
                cm = jnp.max(s, axis=0, keepdims=True)
                mx = cm if mx is None else jnp.maximum(mx, cm)
        if 1 <= t <= n_units:
            u = t - 1
            for c, (off, rows) in enumerate(chunks):
                p = jnp.exp2(s_scr[u % 2, off:off + rows, :] - m_of[u])
                cl = jnp.sum(p, axis=0, keepdims=True)
                l = cl if l is None else l + cl
                p_scr[u % 2, off:off + rows, :] = p.astype(BF16)
        if t < n_units:
            e = extra_of(t)
            m_of[t] = mx if e is None else jnp.maximum(mx, e)
        if 1 <= t <= n_units:
            e = extra_of(t - 1)
            l_of[t - 1] = l if e is None else l + jnp.exp2(e - m_of[t - 1])
    return outs


def _head_rows(o, h):
    return o[(h % 2) * HEAD_DIM:(h % 2 + 1) * HEAD_DIM]


def _diff_lambda(lp, lam_init):
    a = jnp.sum(lp[0:1] * lp[1:2], axis=-1, keepdims=True)
    b = jnp.sum(lp[2:3] * lp[3:4], axis=-1, keepdims=True)
    return jnp.exp(a) - jnp.exp(b) + lam_init


def _diff_combine(outs, lam):
    res = []
    for h in range(HEADS):
        o = outs[2 * h] - lam * outs[2 * h + 1]
        ms = jnp.sum(o * o, axis=0, keepdims=True) * (1.0 / HEAD_DIM)
        res.append(o * lax.rsqrt(ms + EPS))
    return jnp.concatenate(res, axis=0)


def _key_chunks(total, size):
    return [(o, min(size, total - o)) for o in range(0, total, size)]


def _softmax_scratch(nk, tq):
    return [pltpu.VMEM((2, nk, tq), F32), pltpu.VMEM((2, nk, tq), BF16)]


def _transpose_values(vt_scr, srcs):
    for ref, off, n in srcs:
        for r in range(0, n, KEY_CHUNK):
            rows = min(KEY_CHUNK, n - r)
            vt_scr[:, off + r:off + r + rows] = ref[0, r:r + rows, :].astype(F32).T.astype(BF16)


def _mla_kernel(q_ref, k_ref, v_ref, kc_ref, vc_ref, o_ref, s_scr, p_scr, vt_scr):
    s_len, c_len = k_ref.shape[1], kc_ref.shape[1]

    @pl.when(pl.program_id(1) == 0)
    def _():
        _transpose_values(vt_scr, [(v_ref, 0, s_len), (vc_ref, s_len, c_len)])

    chunks = _key_chunks(s_len, KEY_CHUNK) + [(s_len + o, r) for o, r in _key_chunks(c_len, KEY_CHUNK)]
    units = [(qt, h) for qt in range(q_ref.shape[1] // UNIT_Q) for h in range(HEADS)]
    qs = [q_ref[0, qt * UNIT_Q:(qt + 1) * UNIT_Q, h * LANES:(h + 1) * LANES] for qt, h in units]

    def score_of(u, c):
        off, rows = chunks[c]
        h = units[u][1]
        sl = slice(h * LANES, (h + 1) * LANES)
        if off < s_len:
            return _dot_nt(k_ref[0, off:off + rows, sl], qs[u])
        return _dot_nt(kc_ref[0, off - s_len:off - s_len + rows, sl], qs[u])

    def pv_of(u, p_ref):
        h = units[u][1]
        return _dot(vt_scr[h * HEAD_DIM:(h + 1) * HEAD_DIM, :], p_ref[...])

    outs = _pipelined_softmax(len(units), chunks, score_of, lambda u: None, pv_of, s_scr, p_scr)
    for qt in range(len(units) // HEADS):
        o = jnp.concatenate(outs[qt * HEADS:(qt + 1) * HEADS], axis=0)
        o_ref[0, qt * UNIT_Q:(qt + 1) * UNIT_Q, :] = o.T.astype(BF16)


def _mla_call(zl, zc, tq):
    b, s, _ = zl.shape
    c = zc.shape[1]
    return pl.pallas_call(
        _mla_kernel,
        grid=(b, s // tq),
        in_specs=[
            pl.BlockSpec((1, tq, 512), lambda i, j: (i, j, ZC_MQ // 512)),
            pl.BlockSpec((1, s, 512), lambda i, j: (i, 0, ZC_MK // 512)),
            pl.BlockSpec((1, s, 256), lambda i, j: (i, 0, ZC_MV // 256)),
            pl.BlockSpec((1, c, 512), lambda i, j: (i, 0, ZC_MK // 512)),
            pl.BlockSpec((1, c, 256), lambda i, j: (i, 0, ZC_MV // 256)),
        ],
        out_specs=pl.BlockSpec((1, tq, 256), lambda i, j: (i, j, 0)),
        out_shape=jax.ShapeDtypeStruct((b, s, 256), BF16),
        scratch_shapes=_softmax_scratch(s + c, UNIT_Q) + [pltpu.VMEM((256, s + c), BF16)],
        compiler_params=_params("arbitrary", "arbitrary"),
        name="mla",
    )(zl, zl, zl, zc, zc)


def _diff_kernel(lam_init, q_ref, k_ref, v_ref, kc_ref, vc_ref, lp_ref, gain_ref, o_ref,
                 s_scr, p_scr, vt_scr):
    s_len, c_len = k_ref.shape[1], kc_ref.shape[1]

    @pl.when(pl.program_id(1) == 0)
    def _():
        _transpose_values(vt_scr, [(v_ref, 0, s_len), (vc_ref, s_len, c_len)])

    chunks = _key_chunks(s_len, KEY_CHUNK) + [(s_len + o, r) for o, r in _key_chunks(c_len, KEY_CHUNK)]
    n_sub = 2 * HEADS
    units = [(qt, w) for qt in range(q_ref.shape[1] // UNIT_Q) for w in range(n_sub)]
    qs = [_lane_mask(q_ref[0, qt * UNIT_Q:(qt + 1) * UNIT_Q, (w // 4) * LANES:(w // 4 + 1) * LANES],
                     (w % 4) * DIFF_DH, DIFF_DH) for qt, w in units]

    def score_of(u, c):
        off, rows = chunks[c]
        w = units[u][1]
        sl = slice((w // 4) * LANES, (w // 4 + 1) * LANES)
        if off < s_len:
            return _dot_nt(k_ref[0, off:off + rows, sl], qs[u])
        return _dot_nt(kc_ref[0, off - s_len:off - s_len + rows, sl], qs[u])

    def pv_of(u, p_ref):
        h = units[u][1] // 2
        return _dot(vt_scr[h * HEAD_DIM:(h + 1) * HEAD_DIM, :], p_ref[...])

    outs = _pipelined_softmax(len(units), chunks, score_of, lambda u: None, pv_of, s_scr, p_scr)
    lam = _diff_lambda(lp_ref[...], lam_init)
    for qt in range(len(units) // n_sub):
        o = _diff_combine(outs[qt * n_sub:(qt + 1) * n_sub], lam)
        o_ref[0, qt * UNIT_Q:(qt + 1) * UNIT_Q, :] = (o.T * gain_ref[...]).astype(BF16)


def _diff_call(zl, zc, lp, gain, lam_init, tq):
    b, s, _ = zl.shape
    c = zc.shape[1]
    return pl.pallas_call(
        functools.partial(_diff_kernel, lam_init),
        grid=(b, s // tq),
        in_specs=[
            pl.BlockSpec((1, tq, 256), lambda i, j: (i, j, ZC_DQ // 256)),
            pl.BlockSpec((1, s, 256), lambda i, j: (i, 0, ZC_DK // 256)),
            pl.BlockSpec((1, s, 256), lambda i, j: (i, 0, ZC_DV // 256)),
            pl.BlockSpec((1, c, 256), lambda i, j: (i, 0, ZC_DK // 256)),
            pl.BlockSpec((1, c, 256), lambda i, j: (i, 0, ZC_DV // 256)),
            pl.BlockSpec(lp.shape, lambda i, j: (0, 0)),
            pl.BlockSpec(gain.shape, lambda i, j: (0, 0)),
        ],
        out_specs=pl.BlockSpec((1, tq, 256), lambda i, j: (i, j, 0)),
        out_shape=jax.ShapeDtypeStruct((b, s, 256), BF16),
        scratch_shapes=_softmax_scratch(s + c, UNIT_Q) + [pltpu.VMEM((256, s + c), BF16)],
        compiler_params=_params("arbitrary", "arbitrary"),
        name="diff",
    )(zl, zl, zl, zc, zc, lp, gain)


def _swa_kernel(q_ref, k_ref, v_ref, kc_ref, vc_ref, sink_ref, o_ref, s_scr, p_scr, mask_scr):
    s_len, c_len = k_ref.shape[1], kc_ref.shape[1]
    n_qt = q_ref.shape[1] // UNIT_Q
    win = UNIT_Q + 2 * SWA_WINDOW
    chunks = _key_chunks(win, KEY_CHUNK) + [(win + o, r) for o, r in _key_chunks(c_len, KEY_CHUNK)]
    starts = []
    for qt in range(n_qt):
        q0 = (pl.program_id(1) * n_qt + qt) * UNIT_Q
        start = pl.multiple_of(jnp.clip(q0 - SWA_WINDOW, 0, s_len - win), SWA_WINDOW)
        kpos = start + lax.broadcasted_iota(jnp.int32, (win, UNIT_Q), 0)
        qpos = q0 + lax.broadcasted_iota(jnp.int32, (win, UNIT_Q), 1)
        mask_scr[qt] = jnp.where(jnp.abs(kpos - qpos) <= SWA_WINDOW, 0.0, NEG_INF)
        starts.append(start)
    units = [(qt, w) for qt in range(n_qt) for w in range(HEADS)]
    qs = [_lane_mask(q_ref[0, qt * UNIT_Q:(qt + 1) * UNIT_Q, (w // SWA_HKV) * LANES:(w // SWA_HKV + 1) * LANES],
                     (w % SWA_HKV) * HEAD_DIM, HEAD_DIM) for qt, w in units]
    head_of = lambda w: (w % SWA_HKV) * SWA_GROUP + w // SWA_HKV

    def score_of(u, c):
        off, rows = chunks[c]
        qt = units[u][0]
        if off < win:
            k = k_ref[0, pl.ds(starts[qt] + off, rows), :]
            return _dot_nt(k, qs[u]) + mask_scr[qt, off:off + rows, :]
        return _dot_nt(kc_ref[0, off - win:off - win + rows, :], qs[u])

    def extra_of(u):
        return jnp.full((1, 1), sink_ref[head_of(units[u][1])] * LOG2E, F32)

    def pv_of(u, p_ref):
        qt, w = units[u]
        o = (_dot_tn(v_ref[0, pl.ds(starts[qt], win), :], p_ref[0:win, :])
             + _dot_tn(vc_ref[0], p_ref[win:win + c_len, :]))
        return o[(w % SWA_HKV) * HEAD_DIM:(w % SWA_HKV + 1) * HEAD_DIM]

    outs = _pipelined_softmax(len(units), chunks, score_of, extra_of, pv_of, s_scr, p_scr)
    for qt in range(n_qt):
        by_head = {head_of(w): outs[qt * HEADS + w] for w in range(HEADS)}
        o = jnp.concatenate([by_head[h] for h in range(HEADS)], axis=0)
        o_ref[0, qt * UNIT_Q:(qt + 1) * UNIT_Q, :] = o.T.astype(BF16)


def _swa_call(zl, zc, sink, tq):
    b, s, _ = zl.shape
    c = zc.shape[1]
    win = UNIT_Q + 2 * SWA_WINDOW
    return pl.pallas_call(
        _swa_kernel,
        grid=(b, s // tq),
        in_specs=[
            pl.BlockSpec((1, tq, 256), lambda i, j: (i, j, ZC_SQ // 256)),
            pl.BlockSpec((1, s, LANES), lambda i, j: (i, 0, ZC_SK // LANES)),
            pl.BlockSpec((1, s, LANES), lambda i, j: (i, 0, ZC_SV // LANES)),
            pl.BlockSpec((1, c, LANES), lambda i, j: (i, 0, ZC_SK // LANES)),
            pl.BlockSpec((1, c, LANES), lambda i, j: (i, 0, ZC_SV // LANES)),
            pl.BlockSpec(memory_space=pltpu.SMEM),
        ],
        out_specs=pl.BlockSpec((1, tq, 256), lambda i, j: (i, j, 0)),
        out_shape=jax.ShapeDtypeStruct((b, s, 256), BF16),
        scratch_shapes=_softmax_scratch(win + c, UNIT_Q) + [pltpu.VMEM((tq // UNIT_Q, win, UNIT_Q), F32)],
        compiler_params=_params("arbitrary", "arbitrary"),
        name="swa",
    )(zl, zl, zl, zc, zc, sink)


def _na_kernel(rows_n, q_ref, k_ref, v_ref, kc_ref, vc_ref, bias_ref, o_ref, s_scr, p_scr):
    c_len = kc_ref.shape[1]
    n_rb = q_ref.shape[1] // NA_TQ
    nrb = rows_n // 2
    chunks = _key_chunks(NA_NLOC, KEY_CHUNK) + [(NA_NLOC + o, r) for o, r in _key_chunks(c_len, KEY_CHUNK)]
    starts, tids = [], []
    for i in range(n_rb):
        rb = pl.program_id(1) * n_rb + i
        starts.append(pl.multiple_of(
            jnp.clip(2 * rb - NA_WIN_R // 2, 0, rows_n - NA_KROWS) * GRID_W, 2 * GRID_W))
        tids.append(jnp.minimum(rb, 2) + jnp.maximum(rb - (nrb - 3), 0))
    blk = lambda h: slice((h // 2) * LANES, (h // 2 + 1) * LANES)
    units = [(i, h) for i in range(n_rb) for h in range(HEADS)]
    qs = [_lane_mask(q_ref[0, i * NA_TQ:(i + 1) * NA_TQ, blk(h)], (h % 2) * HEAD_DIM, HEAD_DIM)
          for i, h in units]

    def score_of(u, c):
        off, rows = chunks[c]
        i, h = units[u]
        if off < NA_NLOC:
            k = k_ref[0, pl.ds(starts[i] + off, rows), blk(h)]
            return _dot_nt(k, qs[u]) + bias_ref[tids[i], h, off:off + rows, :]
        return _dot_nt(kc_ref[0, off - NA_NLOC:off - NA_NLOC + rows, blk(h)], qs[u])

    def pv_of(u, p_ref):
        i, h = units[u]
        o = (_dot_tn(v_ref[0, pl.ds(starts[i], NA_NLOC), blk(h)], p_ref[0:NA_NLOC, :])
             + _dot_tn(vc_ref[0, :, blk(h)], p_ref[NA_NLOC:NA_NLOC + c_len, :]))
        return _head_rows(o, h)

    outs = _pipelined_softmax(len(units), chunks, score_of, lambda u: None, pv_of, s_scr, p_scr)
    for i in range(n_rb):
        o = jnp.concatenate(outs[i * HEADS:(i + 1) * HEADS], axis=0)
        o_ref[0, i * NA_TQ:(i + 1) * NA_TQ, :] = o.T.astype(BF16)


def _na_call(zl, zc, bias, tq):
    b, s, _ = zl.shape
    c = zc.shape[1]
    rows_n = s // GRID_W
    return pl.pallas_call(
        functools.partial(_na_kernel, rows_n),
        grid=(b, s // tq),
        in_specs=[
            pl.BlockSpec((1, tq, 256), lambda i, j: (i, j, ZC_NQ // 256)),
            pl.BlockSpec((1, s, 256), lambda i, j: (i, 0, ZC_NK // 256)),
            pl.BlockSpec((1, s, 256), lambda i, j: (i, 0, ZC_NV // 256)),
            pl.BlockSpec((1, c, 256), lambda i, j: (i, 0, ZC_NK // 256)),
            pl.BlockSpec((1, c, 256), lambda i, j: (i, 0, ZC_NV // 256)),
            pl.BlockSpec(bias.shape, lambda i, j: (0, 0, 0, 0), pipeline_mode=pl.Buffered(1)),
        ],
        out_specs=pl.BlockSpec((1, tq, 256), lambda i, j: (i, j, 0)),
        out_shape=jax.ShapeDtypeStruct((b, s, 256), BF16),
        scratch_shapes=_softmax_scratch(NA_NLOC + c, NA_TQ),
        compiler_params=_params("arbitrary", "arbitrary"),
        name="na",
    )(zl, zl, zl, zc, zc, bias)


def _ctx_kernel(lam_init, z_ref, lp_ref, gain_ref, sink_ref, oa_ref, ob_ref, oc_ref, od_ref,
                s_scr, p_scr):
    c_len = z_ref.shape[1]

    def blk(col, j=0):
        return z_ref[0, :, col + j * LANES:col + (j + 1) * LANES]

    units = []
    for h in range(HEADS):
        units.append((_lane_mask(blk(ZC_NQ, h // 2), (h % 2) * HEAD_DIM, HEAD_DIM),
                      ZC_NK + (h // 2) * LANES, ZC_NV + (h // 2) * LANES, (h % 2) * HEAD_DIM, None))
    for h in range(HEADS):
        units.append((blk(ZC_MQ, h), ZC_MK + h * LANES, ZC_MV + (h // 2) * LANES,
                      (h % 2) * HEAD_DIM, None))
    for h in range(HEADS):
        hk, g = h // SWA_GROUP, h % SWA_GROUP
        units.append((_lane_mask(blk(ZC_SQ, g), hk * HEAD_DIM, HEAD_DIM), ZC_SK, ZC_SV,
                      hk * HEAD_DIM, h))
    for w in range(2 * HEADS):
        units.append((_lane_mask(blk(ZC_DQ, w // 4), (w % 4) * DIFF_DH, DIFF_DH),
                      ZC_DK + (w // 4) * LANES, ZC_DV + (w // 4) * LANES, ((w // 2) % 2) * HEAD_DIM, None))

    def score_of(u, c):
        q_m, kcol = units[u][0], units[u][1]
        return _dot_nt(z_ref[0, :, kcol:kcol + LANES], q_m)

    def extra_of(u):
        h = units[u][4]
        return None if h is None else jnp.full((1, 1), sink_ref[h] * LOG2E, F32)

    def pv_of(u, p_ref):
        vcol, r0 = units[u][2], units[u][3]
        return _dot_tn(z_ref[0, :, vcol:vcol + LANES], p_ref[...])[r0:r0 + HEAD_DIM]

    outs = _pipelined_softmax(len(units), [(0, c_len)], score_of, extra_of, pv_of, s_scr, p_scr)
    for i, ref in enumerate((oa_ref, ob_ref, oc_ref)):
        ref[0] = jnp.concatenate(outs[i * HEADS:(i + 1) * HEADS], axis=0).T.astype(BF16)
    o = _diff_combine(outs[3 * HEADS:], _diff_lambda(lp_ref[...], lam_init))
    od_ref[0] = (o.T * gain_ref[...]).astype(BF16)


def _ctx_call(zc, lp, gain, sink, lam_init):
    b, c, _ = zc.shape
    out = jax.ShapeDtypeStruct((b, c, 256), BF16)
    ospec = pl.BlockSpec((1, c, 256), lambda i: (i, 0, 0))
    return pl.pallas_call(
        functools.partial(_ctx_kernel, lam_init),
        grid=(b,),
        in_specs=[
            pl.BlockSpec((1, c, Z_COLS), lambda i: (i, 0, 0)),
            pl.BlockSpec(lp.shape, lambda i: (0, 0)),
            pl.BlockSpec(gain.shape, lambda i: (0, 0)),
            pl.BlockSpec(memory_space=pltpu.SMEM),
        ],
        out_specs=[ospec] * 4,
        out_shape=[out] * 4,
        scratch_shapes=_softmax_scratch(c, c),
        compiler_params=_params("arbitrary"),
        name="ctx_attn",
    )(zc, lp, gain, sink)


def _mlp_kernel(final, x_ref, ma_ref, mb_ref, mc_ref, md_ref, mod_ref, g_ref, gf_ref,
                wout_ref, wup_ref, wdn_ref, o_ref):
    x = x_ref[0]
    mod = mod_ref[0]
    attn = None
    for i, m_ref in enumerate((ma_ref, mb_ref, mc_ref, md_ref)):
        part = _dot(m_ref[0], wout_ref[i * 256:(i + 1) * 256, :])
        attn = part if attn is None else attn + part
    x = x + mod[2:3] * attn
    h = x * _rms_scale(x, D_MODEL) * g_ref[...]
    h = (h * (1.0 + mod[4:5]) + mod[3:4]).astype(BF16)
    acc = None
    ck = 1024
    for c in range(D_FF // ck):
        u = jnp.maximum(_dot(h, wup_ref[:, c * ck:(c + 1) * ck]), 0.0)
        part = _dot((u * u).astype(BF16), wdn_ref[c * ck:(c + 1) * ck, :])
        acc = part if acc is None else acc + part
    x = x + mod[5:6] * acc
    if final:
        x = x * _rms_scale(x, D_MODEL) * gf_ref[...]
    o_ref[0] = x


def _mlp_call(x3, mixes, mod, mod_row, g, gf, wout, wup, wdn, tm, final):
    nb, n, d = x3.shape
    const = lambda t, b: (0, 0)
    tok = lambda t, b: (b, t, 0)
    mod_map = (lambda t, b: (b, 0, 0)) if mod_row is None else (lambda t, b: (mod_row, 0, 0))
    single = pl.Buffered(1)
    return pl.pallas_call(
        functools.partial(_mlp_kernel, final),
        grid=(n // tm, nb),
        in_specs=[pl.BlockSpec((1, tm, d), tok)]
        + [pl.BlockSpec((1, tm, 256), tok)] * 4
        + [
            pl.BlockSpec((1, 6, d), mod_map),
            pl.BlockSpec((1, d), const),
            pl.BlockSpec((1, d), const),
            pl.BlockSpec(wout.shape, const, pipeline_mode=single),
            pl.BlockSpec(wup.shape, const, pipeline_mode=single),
            pl.BlockSpec(wdn.shape, const, pipeline_mode=single),
        ],
        out_specs=pl.BlockSpec((1, tm, d), tok),
        out_shape=jax.ShapeDtypeStruct((nb, n, d), F32),
        compiler_params=_params("arbitrary", "arbitrary"),
        name="mlp",
    )(x3, *mixes, mod, g, gf, wout, wup, wdn)


def _rope_tables(s_len):
    t = np.arange(s_len)
    row, col = t // GRID_W, t % GRID_W

    def axial(dim):
        n_freq = dim // 4
        freqs = jnp.asarray(ROPE_BASE, F32) ** (-jnp.arange(n_freq, dtype=F32) / n_freq)
        ang = jnp.concatenate([jnp.asarray(row, F32)[:, None] * freqs,
                               jnp.asarray(col, F32)[:, None] * freqs], axis=-1)
        return jnp.cos(ang), jnp.sin(ang)

    def group(cs, n_groups, scale):
        cos, sin = cs
        c = jnp.concatenate([cos, cos], axis=-1) * scale
        s = jnp.concatenate([-sin, sin], axis=-1) * scale
        return jnp.tile(c, (1, n_groups)), jnp.tile(s, (1, n_groups))

    r64, r32 = axial(HEAD_DIM), axial(MLA_ROPE)
    ones = lambda n, v=1.0: jnp.full((s_len, n), v, F32)
    zeros = lambda n: jnp.zeros((s_len, n), F32)
    mla_scale = (MLA_NOPE + MLA_ROPE) ** -0.5 * LOG2E
    mq_c, mq_s = group(r32, 1, mla_scale)
    parts = [
        group(r64, 4, HEAD_DIM ** -0.5 * LOG2E), group(r64, 2, 1.0),
        group(r32, 8, DIFF_DH ** -0.5 * LOG2E), group(r32, 8, 1.0),
        (ones(MLA_Q_LORA), zeros(MLA_Q_LORA)), group(r32, 1, 1.0), (ones(32), zeros(32)),
    ] + [(jnp.concatenate([ones(MLA_NOPE, mla_scale), mq_c, ones(32, mla_scale)], axis=-1),
          jnp.concatenate([zeros(MLA_NOPE), mq_s, zeros(32)], axis=-1))] * HEADS
    cos = jnp.concatenate([p[0] for p in parts], axis=-1)
    sin = jnp.concatenate([p[1] for p in parts], axis=-1)
    return cos, sin, cos[0:1]


def _permute_w_in(w_in):
    offs = np.cumsum([0, 256, 256, 256, MLA_Q_LORA, MLA_KV_LORA, MLA_ROPE, 256, 128, 128, 256, 256, 256])
    sec = lambda i: w_in[..., offs[i]:offs[i + 1]]
    sq = sec(6)
    sq = sq.reshape(sq.shape[:-1] + (SWA_HKV, SWA_GROUP, HEAD_DIM))
    sq = jnp.swapaxes(sq, -3, -2).reshape(w_in.shape[:-1] + (256,))
    pad = jnp.zeros(w_in.shape[:-1] + (32,), w_in.dtype)
    cols = [sq, sec(7), sec(9), sec(10), sec(4), sec(3), sec(5), pad,
            sec(0), sec(1), sec(2), sec(11), sec(8)]
    return jnp.concatenate(cols, axis=-1).astype(BF16)


def _mla_weights(w_uq, w_ukv):
    depth = w_uq.shape[0]
    uq = w_uq.reshape(depth, MLA_Q_LORA, HEADS, MLA_NOPE + MLA_ROPE)
    uq = jnp.pad(uq, ((0, 0), (0, 256 - MLA_Q_LORA), (0, 0), (0, LANES - MLA_NOPE - MLA_ROPE)))
    uq = uq.reshape(depth, 256, HEADS * LANES).astype(BF16)
    ukv = w_ukv.reshape(depth, MLA_KV_LORA, HEADS, 2, HEAD_DIM)
    uk = jnp.pad(ukv[:, :, :, 0], ((0, 0), (0, 0), (0, 0), (0, LANES - MLA_NOPE)))
    uk = uk.reshape(depth, MLA_KV_LORA, HEADS * LANES).astype(BF16)
    uv = ukv[:, :, :, 1].reshape(depth, MLA_KV_LORA, HEADS * HEAD_DIM).astype(BF16)
    e = np.zeros((256, HEADS * LANES), np.float32)
    for h in range(HEADS):
        for j in range(MLA_ROPE):
            e[MLA_Q_LORA + j, h * LANES + MLA_NOPE + j] = 1.0
    return uq, uk, uv, jnp.asarray(e, BF16)


def _na_bias_tables(rpb, rows_n):
    kr_n = min(NA_WIN_R, rows_n)
    col = np.arange(GRID_W)
    c0 = np.clip(col - NA_WIN_C // 2, 0, GRID_W - NA_WIN_C)
    col_ok_t = ((col[None, :] >= c0[:, None]) & (col[None, :] < c0[:, None] + NA_WIN_C)).T
    edge = GRID_W - NA_WIN_C
    ext = jnp.concatenate([jnp.repeat(rpb[..., :1], edge, axis=-1), rpb,
                           jnp.repeat(rpb[..., -1:], edge, axis=-1)], axis=-1)[..., ::-1]
    toe = jnp.stack([ext[..., GRID_W - 1 - kc:2 * GRID_W - 1 - kc] for kc in range(GRID_W)], axis=-2)
    toe = jnp.where(col_ok_t, toe.astype(F32) * LOG2E, NEG_INF)
    neg = jnp.full(toe.shape[:2] + (GRID_W, GRID_W), NEG_INF, F32)
    nrb = rows_n // 2
    tabs = []
    for rb in (0, 1, 2, nrb - 2, nrb - 1):
        start = int(np.clip(2 * rb - NA_WIN_R // 2, 0, rows_n - NA_KROWS))
        rows = []
        for j in range(NA_KROWS):
            pieces = []
            for i in range(2):
                kr, qr = start + j, 2 * rb + i
                r0 = int(np.clip(qr - kr_n // 2, 0, rows_n - kr_n))
                ok = r0 <= kr < r0 + kr_n
                pieces.append(toe[:, :, kr - qr + NA_WIN_R - 1] if ok else neg)
            rows.append(jnp.concatenate(pieces, axis=-1))
        tabs.append(jnp.concatenate(rows, axis=-2))
    return jnp.stack(tabs, axis=1)


def kernel(x, c, ctx, c_ctx, w_ada, b_ada, norm_attn_g, w_in, na_rpb, mla_q_norm_g, mla_w_uq,
           mla_kv_norm_g, mla_w_ukv, swa_sink, diff_lambda, diff_norm_g, w_out, norm_mlp_g,
           w_up, w_down, final_norm_g):
    b, s, d = x.shape
    c_len = ctx.shape[1]
    depth = w_in.shape[0]
    rows_n = s // GRID_W

    w_in_p = _permute_w_in(w_in)
    uq, uk, uv, e_mat = _mla_weights(mla_w_uq, mla_w_ukv)
    gq = jnp.pad(mla_q_norm_g, ((0, 0), (0, 256 - MLA_Q_LORA)))[:, None, :]
    gkv = mla_kv_norm_g[:, None, :]
    w_out_b, w_up_b, w_dn_b = w_out.astype(BF16), w_up.astype(BF16), w_down.astype(BF16)
    na_bias = _na_bias_tables(na_rpb, rows_n)
    cos, sin, cos_ctx = _rope_tables(s)
    tm_l, tm_c = 512, min(512, b * c_len)
    cos_c = jnp.broadcast_to(cos_ctx, (tm_c, T_COLS))
    sin_c = jnp.zeros((tm_c, T_COLS), F32)

    c_rows = -(-(b + 1) // 8) * 8
    c_all = jnp.zeros((c_rows, d), F32).at[:b].set(c).at[b].set(c_ctx)
    mod = _ada_call(c_all, w_ada, b_ada).reshape(depth, c_rows, 6, d)

    xl = x
    xc = ctx.reshape(1, b * c_len, d)
    for l in range(depth):
        need_ctx = l < depth - 1
        lam_init = 0.8 - 0.6 * math.exp(-0.3 * l)
        g_attn = norm_attn_g[l][None, :]
        g_mlp = norm_mlp_g[l][None, :]
        gf = final_norm_g[None, :]
        d_gain = (jnp.tile(diff_norm_g[l], HEADS) * (1.0 - lam_init))[None, :]
        proj = (w_in_p[l], gq[l], gkv[l], uq[l], uk[l], uv[l], e_mat)
        zl = _inproj_call(xl, mod[l], None, g_attn, cos, sin, True, *proj, tm=tm_l)
        zc = _inproj_call(xc, mod[l], b, g_attn, cos_c, sin_c, False, *proj, tm=tm_c)
        zc = zc.reshape(b, c_len, Z_COLS)
        mixes = (
            _na_call(zl, zc, na_bias[l], tq=512),
            _mla_call(zl, zc, tq=512),
            _swa_call(zl, zc, swa_sink[l], tq=512),
            _diff_call(zl, zc, diff_lambda[l], d_gain, lam_init, tq=512),
        )
        wts = (w_out_b[l], w_up_b[l], w_dn_b[l])
        xl = _mlp_call(xl, mixes, mod[l], None, g_mlp, gf, *wts, tm=tm_l, final=not need_ctx)
        if need_ctx:
            mixes_c = _ctx_call(zc, diff_lambda[l], d_gain, swa_sink[l], lam_init)
            mixes_c = [m.reshape(1, b * c_len, 256) for m in mixes_c]
            xc = _mlp_call(xc, mixes_c, mod[l], b, g_mlp, gf, *wts, tm=tm_c, final=False)
    return xl
```

```python
import functools
import math

import jax
import jax.numpy as jnp
import numpy as np
from jax import lax
from jax.experimental import pallas as pl
from jax.experimental.pallas import tpu as pltpu

F32 = jnp.float32
BF16 = jnp.bfloat16

D_MODEL = 1024
GRID_W = 64
HEADS = 4
HEAD_DIM = 64
NA_WIN_R = 8
NA_WIN_C = 16
MLA_Q_LORA = 192
MLA_KV_LORA = 128
MLA_NOPE = 64
MLA_ROPE = 32
SWA_HKV = 2
SWA_GROUP = 2
SWA_WINDOW = 128
DIFF_DH = 32
D_FF = 4 * D_MODEL
ROPE_BASE = 10000.0
EPS = 1e-6
NEG_INF = -1e30
LOG2E = math.log2(math.e)

LANES = 128
VMEM_LIMIT = 56 * 1024 * 1024

ZC_DQ, ZC_DK, ZC_SQ, ZC_NQ, ZC_NK, ZC_NV, ZC_DV, ZC_MV = 0, 256, 512, 768, 1024, 1280, 1536, 1792
ZC_MQ, ZC_MK, ZC_SK, ZC_SV = 2048, 2560, 3072, 3200
Z_COLS = 3328
W_R64, W_R32, W_M, W_N, W_COLS = 0, 384, 896, 1280, 2432
T_R64, T_R32, T_B12, T_MQ, T_COLS = 0, 384, 896, 1152, 1664

NA_KROWS = 10
NA_TQ = 2 * GRID_W
NA_NLOC = NA_KROWS * GRID_W
KEY_CHUNK = 256
UNIT_Q = 256


def _params(*sem):
    return pltpu.CompilerParams(dimension_semantics=sem, vmem_limit_bytes=VMEM_LIMIT)


def _dot(a, b):
    return jnp.dot(a, b, preferred_element_type=F32)


def _dot_nt(a, b):
    return lax.dot_general(a, b, (((1,), (1,)), ((), ())), preferred_element_type=F32)


def _dot_tn(a, b):
    return lax.dot_general(a, b, (((0,), (0,)), ((), ())), preferred_element_type=F32)


def _rms_scale(x, n):
    return lax.rsqrt(jnp.sum(x * x, axis=-1, keepdims=True) * (1.0 / n) + EPS)


def _ada_kernel(c_ref, w_ref, b_ref, o_ref):
    c = c_ref[...]
    act = (c * (1.0 / (1.0 + jnp.exp(-c)))).astype(BF16)
    o_ref[0] = _dot(act, w_ref[0].astype(BF16)) + b_ref[0]


def _ada_call(c_all, w_ada, b_ada):
    depth, d, n = w_ada.shape
    rows = c_all.shape[0]
    tn = 1536
    return pl.pallas_call(
        _ada_kernel,
        grid=(depth, n // tn),
        in_specs=[
            pl.BlockSpec((rows, d), lambda l, j: (0, 0)),
            pl.BlockSpec((1, d, tn), lambda l, j: (l, 0, j)),
            pl.BlockSpec((1, 1, tn), lambda l, j: (l, 0, j)),
        ],
        out_specs=pl.BlockSpec((1, rows, tn), lambda l, j: (l, 0, j)),
        out_shape=jax.ShapeDtypeStruct((depth, rows, n), F32),
        compiler_params=_params("arbitrary", "arbitrary"),
        name="ada",
    )(c_all, w_ada, b_ada.reshape(depth, 1, n))


def _rope_blocks(z, cos_ref, sin_ref, tcol, half):
    tm = z.shape[0]
    lane = lax.broadcasted_iota(jnp.int32, (tm, LANES), 1)
    first = (lane % (2 * half)) < half
    out = []
    for j in range(z.shape[1] // LANES):
        xb = z[:, j * LANES:(j + 1) * LANES]
        rot = jnp.where(first, pltpu.roll(xb, LANES - half, 1), pltpu.roll(xb, half, 1))
        c0 = tcol + j * LANES
        out.append(xb * cos_ref[:, c0:c0 + LANES] + rot * sin_ref[:, c0:c0 + LANES])
    return out


def _inproj_kernel(x_ref, mod_ref, g_ref, cos_ref, sin_ref, w_ref, gq_ref, gkv_ref,
                   wuq_ref, wukvk_ref, wukvv_ref, e_ref, o_ref):
    x = x_ref[0]
    mod = mod_ref[0]
    h = x * _rms_scale(x, D_MODEL) * g_ref[...]
    h = (h * (1.0 + mod[1:2]) + mod[0:1]).astype(BF16)

    def put(col, val):
        o_ref[0, :, col:col + val.shape[1]] = val.astype(BF16)

    z64 = _dot(h, w_ref[:, W_R64:W_R32])
    z32 = _dot(h, w_ref[:, W_R32:W_M])
    r = _rope_blocks(z64, cos_ref, sin_ref, T_R64, HEAD_DIM // 2)
    put(ZC_SQ, r[0]); put(ZC_SQ + LANES, r[1]); put(ZC_SK, r[2])
    zm = _dot(h, w_ref[:, W_M:W_N])
    r = _rope_blocks(z32, cos_ref, sin_ref, T_R32, DIFF_DH // 2)
    for j in range(4):
        put(ZC_DQ + j * LANES, r[j])
    zn = _dot(h, w_ref[:, W_N:W_COLS])

    ckv = zm[:, 0:MLA_KV_LORA]
    b12 = zm[:, MLA_KV_LORA:]
    ckv_n = (ckv * _rms_scale(ckv, MLA_KV_LORA) * gkv_ref[...]).astype(BF16)
    lane = lax.broadcasted_iota(jnp.int32, b12.shape, 1)
    cq = jnp.where(lane < MLA_Q_LORA, b12, 0.0)
    cq_n = (cq * _rms_scale(cq, MLA_Q_LORA) * gq_ref[...]).astype(BF16)
    zq = _dot(cq_n, wuq_ref[...])
    put(ZC_NQ, zn[:, 0:256] * (HEAD_DIM ** -0.5 * LOG2E))
    put(ZC_NK, zn[:, 256:1024])
    put(ZC_SV, zn[:, 1024:1152])
    q = _rope_blocks(zq, cos_ref, sin_ref, T_MQ, MLA_ROPE // 2)
    for j in range(HEADS):
        put(ZC_MQ + j * LANES, q[j])
    kr = _rope_blocks(b12, cos_ref, sin_ref, T_B12, MLA_ROPE // 2)
    kr = jnp.concatenate(kr, axis=1).astype(BF16)
    put(ZC_MK, _dot(ckv_n, wukvk_ref[...]) + _dot(kr, e_ref[...]))
    put(ZC_MV, _dot(ckv_n, wukvv_ref[...]))


def _inproj_call(x3, mod, mod_row, g, cos, sin, table_per_tile, w, gq, gkv, wuq, wukvk, wukvv, e, tm):
    nb, n, d = x3.shape
    nt = n // tm
    const = lambda t, b: (0, 0)
    tab_map = (lambda t, b: (t, 0)) if table_per_tile else const
    mod_map = (lambda t, b: (b, 0, 0)) if mod_row is None else (lambda t, b: (mod_row, 0, 0))
    return pl.pallas_call(
        _inproj_kernel,
        grid=(nt, nb),
        in_specs=[
            pl.BlockSpec((1, tm, d), lambda t, b: (b, t, 0)),
            pl.BlockSpec((1, 6, d), mod_map),
            pl.BlockSpec((1, d), const),
            pl.BlockSpec((tm, T_COLS), tab_map),
            pl.BlockSpec((tm, T_COLS), tab_map),
            pl.BlockSpec(w.shape, const, pipeline_mode=pl.Buffered(1)),
            pl.BlockSpec(gq.shape, const),
            pl.BlockSpec(gkv.shape, const),
            pl.BlockSpec(wuq.shape, const),
            pl.BlockSpec(wukvk.shape, const),
            pl.BlockSpec(wukvv.shape, const),
            pl.BlockSpec(e.shape, const),
        ],
        out_specs=pl.BlockSpec((1, tm, Z_COLS), lambda t, b: (b, t, 0)),
        out_shape=jax.ShapeDtypeStruct((nb, n, Z_COLS), BF16),
        compiler_params=_params("arbitrary", "arbitrary"),
        name="inproj",
    )(x3, mod, g, cos, sin, w, gq, gkv, wuq, wukvk, wukvv, e)


def _lane_mask(q, lo, width):
    lane = lax.broadcasted_iota(jnp.int32, q.shape, 1)
    keep = jnp.where(lane >= lo, lane, LANES) < lo + width
    return jnp.where(keep, q.astype(F32), 0.0).astype(BF16)


def _pipelined_softmax(n_units, chunks, score_of, extra_of, pv_of, s_scr, p_scr):
    outs = [None] * n_units
    m_of, l_of = {}, {}
    for t in range(n_units + 2):
        if t >= 2:
            outs[t - 2] = pv_of(t - 2, p_scr.at[t % 2]) * (1.0 / l_of[t - 2])
        mx, l = None, None
        if t < n_units:
            for c, (off, rows) in enumerate(chunks):
                s = score_of(t, c)
                s_scr[t % 2, off:off + rows, :] = s
                cm = jnp.max(s, axis=0, keepdims=True)
                mx = cm if mx is None else jnp.maximum(mx, cm)
        if 1 <= t <= n_units:
            u = t - 1
            for c, (off, rows) in enumerate(chunks):
                p = jnp.exp2(s_scr[u % 2, off:off + rows, :] - m_of[u])
                cl = jnp.sum(p, axis=0, keepdims=True)
                l = cl if l is None else l + cl
                p_scr[u % 2, off:off + rows, :] = p.astype(BF16)
        if t < n_units:
            e = extra_of(t)
            m_of[t] = mx if e is None else jnp.maximum(mx, e)
        if 1 <= t <= n_units:
            e = extra_of(t - 1)
            l_of[t - 1] = l if e is None else l + jnp.exp2(e - m_of[t - 1])
    return outs


def _batched_softmax(n_units, chunks, score_of, extra_of, pv_of, s_scr, p_scr):
    m_of, l_of = [], []
    for u in range(n_units):
        mx = None
        for c, (off, rows) in enumerate(chunks):
            s = score_of(u, c)
            s_scr[u, off:off + rows, :] = s
            cm = jnp.max(s, axis=0, keepdims=True)
            mx = cm if mx is None else jnp.maximum(mx, cm)
        e = extra_of(u)
        m_of.append(mx if e is None else jnp.maximum(mx, e))
    for u in range(n_units):
        l = None
        for c, (off, rows) in enumerate(chunks):
            p = jnp.exp2(s_scr[u, off:off + rows, :] - m_of[u])
            cl = jnp.sum(p, axis=0, keepdims=True)
            l = cl if l is None else l + cl
            p_scr[u, off:off + rows, :] = p.astype(BF16)
        e = extra_of(u)
        l_of.append(l if e is None else l + jnp.exp2(e - m_of[u]))
    return [pv_of(u, p_scr.at[u]) * (1.0 / l_of[u]) for u in range(n_units)]


def _head_rows(o, h):
    return o[(h % 2) * HEAD_DIM:(h % 2 + 1) * HEAD_DIM]


def _diff_lambda(lp, lam_init):
    a = jnp.sum(lp[0:1] * lp[1:2], axis=-1, keepdims=True)
    b = jnp.sum(lp[2:3] * lp[3:4], axis=-1, keepdims=True)
    return jnp.exp(a) - jnp.exp(b) + lam_init


def _diff_combine(outs, lam):
    res = []
    for h in range(HEADS):
        o = outs[2 * h] - lam * outs[2 * h + 1]
        ms = jnp.sum(o * o, axis=0, keepdims=True) * (1.0 / HEAD_DIM)
        res.append(o * lax.rsqrt(ms + EPS))
    return jnp.concatenate(res, axis=0)


def _key_chunks(total, size):
    return [(o, min(size, total - o)) for o in range(0, total, size)]


def _softmax_scratch(nk, tq, slots=2):
    return [pltpu.VMEM((slots, nk, tq), F32), pltpu.VMEM((slots, nk, tq), BF16)]


def _transpose_values(vt_scr, srcs):
    for ref, off, n in srcs:
        for r in range(0, n, KEY_CHUNK):
            rows = min(KEY_CHUNK, n - r)
            vt_scr[:, off + r:off + r + rows] = ref[0, r:r + rows, :].astype(F32).T.astype(BF16)


def _mla_kernel(q_ref, k_ref, v_ref, kc_ref, vc_ref, o_ref, s_scr, p_scr, vt_scr):
    s_len, c_len = k_ref.shape[1], kc_ref.shape[1]

    @pl.when(pl.program_id(1) == 0)
    def _():
        _transpose_values(vt_scr, [(v_ref, 0, s_len), (vc_ref, s_len, c_len)])

    chunks = _key_chunks(s_len, KEY_CHUNK) + [(s_len + o, r) for o, r in _key_chunks(c_len, KEY_CHUNK)]
    units = [(qt, h) for qt in range(q_ref.shape[1] // UNIT_Q) for h in range(HEADS)]
    qs = [q_ref[0, qt * UNIT_Q:(qt + 1) * UNIT_Q, h * LANES:(h + 1) * LANES] for qt, h in units]

    def score_of(u, c):
        off, rows = chunks[c]
        h = units[u][1]
        sl = slice(h * LANES, (h + 1) * LANES)
        if off < s_len:
            return _dot_nt(k_ref[0, off:off + rows, sl], qs[u])
        return _dot_nt(kc_ref[0, off - s_len:off - s_len + rows, sl], qs[u])

    def pv_of(u, p_ref):
        h = units[u][1]
        return _dot(vt_scr[h * HEAD_DIM:(h + 1) * HEAD_DIM, :], p_ref[...])

    outs = _pipelined_softmax(len(units), chunks, score_of, lambda u: None, pv_of, s_scr, p_scr)
    for qt in range(len(units) // HEADS):
        o = jnp.concatenate(outs[qt * HEADS:(qt + 1) * HEADS], axis=0)
        o_ref[0, qt * UNIT_Q:(qt + 1) * UNIT_Q, :] = o.T.astype(BF16)


def _mla_call(zl, zc, tq):
    b, s, _ = zl.shape
    c = zc.shape[1]
    return pl.pallas_call(
        _mla_kernel,
        grid=(b, s // tq),
        in_specs=[
            pl.BlockSpec((1, tq, 512), lambda i, j: (i, j, ZC_MQ // 512)),
            pl.BlockSpec((1, s, 512), lambda i, j: (i, 0, ZC_MK // 512)),
            pl.BlockSpec((1, s, 256), lambda i, j: (i, 0, ZC_MV // 256)),
            pl.BlockSpec((1, c, 512), lambda i, j: (i, 0, ZC_MK // 512)),
            pl.BlockSpec((1, c, 256), lambda i, j: (i, 0, ZC_MV // 256)),
        ],
        out_specs=pl.BlockSpec((1, tq, 256), lambda i, j: (i, j, 0)),
        out_shape=jax.ShapeDtypeStruct((b, s, 256), BF16),
        scratch_shapes=_softmax_scratch(s + c, UNIT_Q) + [pltpu.VMEM((256, s + c), BF16)],
        compiler_params=_params("arbitrary", "arbitrary"),
        name="mla",
    )(zl, zl, zl, zc, zc)


def _diff_kernel(lam_init, q_ref, k_ref, v_ref, kc_ref, vc_ref, lp_ref, gain_ref, o_ref,
                 s_scr, p_scr, vt_scr):
    s_len, c_len = k_ref.shape[1], kc_ref.shape[1]

    @pl.when(pl.program_id(1) == 0)
    def _():
        _transpose_values(vt_scr, [(v_ref, 0, s_len), (vc_ref, s_len, c_len)])

    chunks = _key_chunks(s_len, KEY_CHUNK) + [(s_len + o, r) for o, r in _key_chunks(c_len, KEY_CHUNK)]
    n_sub = 2 * HEADS
    units = [(qt, w) for qt in range(q_ref.shape[1] // UNIT_Q) for w in range(n_sub)]
    qs = [_lane_mask(q_ref[0, qt * UNIT_Q:(qt + 1) * UNIT_Q, (w // 4) * LANES:(w // 4 + 1) * LANES],
                     (w % 4) * DIFF_DH, DIFF_DH) for qt, w in units]

    def score_of(u, c):
        off, rows = chunks[c]
        w = units[u][1]
        sl = slice((w // 4) * LANES, (w // 4 + 1) * LANES)
        if off < s_len:
            return _dot_nt(k_ref[0, off:off + rows, sl], qs[u])
        return _dot_nt(kc_ref[0, off - s_len:off - s_len + rows, sl], qs[u])

    def pv_of(u, p_ref):
        h = units[u][1] // 2
        return _dot(vt_scr[h * HEAD_DIM:(h + 1) * HEAD_DIM, :], p_ref[...])

    outs = _pipelined_softmax(len(units), chunks, score_of, lambda u: None, pv_of, s_scr, p_scr)
    lam = _diff_lambda(lp_ref[...], lam_init)
    for qt in range(len(units) // n_sub):
        o = _diff_combine(outs[qt * n_sub:(qt + 1) * n_sub], lam)
        o_ref[0, qt * UNIT_Q:(qt + 1) * UNIT_Q, :] = (o.T * gain_ref[...]).astype(BF16)


def _diff_call(zl, zc, lp, gain, lam_init, tq):
    b, s, _ = zl.shape
    c = zc.shape[1]
    return pl.pallas_call(
        functools.partial(_diff_kernel, lam_init),
        grid=(b, s // tq),
        in_specs=[
            pl.BlockSpec((1, tq, 256), lambda i, j: (i, j, ZC_DQ // 256)),
            pl.BlockSpec((1, s, 256), lambda i, j: (i, 0, ZC_DK // 256)),
            pl.BlockSpec((1, s, 256), lambda i, j: (i, 0, ZC_DV // 256)),
            pl.BlockSpec((1, c, 256), lambda i, j: (i, 0, ZC_DK // 256)),
            pl.BlockSpec((1, c, 256), lambda i, j: (i, 0, ZC_DV // 256)),
            pl.BlockSpec(lp.shape, lambda i, j: (0, 0)),
            pl.BlockSpec(gain.shape, lambda i, j: (0, 0)),
        ],
        out_specs=pl.BlockSpec((1, tq, 256), lambda i, j: (i, j, 0)),
        out_shape=jax.ShapeDtypeStruct((b, s, 256), BF16),
        scratch_shapes=_softmax_scratch(s + c, UNIT_Q) + [pltpu.VMEM((256, s + c), BF16)],
        compiler_params=_params("arbitrary", "arbitrary"),
        name="diff",
    )(zl, zl, zl, zc, zc, lp, gain)


def _swa_kernel(q_ref, k_ref, v_ref, kc_ref, vc_ref, sink_ref, o_ref, s_scr, p_scr, mask_scr):
    s_len, c_len = k_ref.shape[1], kc_ref.shape[1]
    n_qt = q_ref.shape[1] // UNIT_Q
    win = UNIT_Q + 2 * SWA_WINDOW
    chunks = _key_chunks(win, KEY_CHUNK) + [(win + o, r) for o, r in _key_chunks(c_len, KEY_CHUNK)]
    starts = []
    for qt in range(n_qt):
        q0 = (pl.program_id(1) * n_qt + qt) * UNIT_Q
        start = pl.multiple_of(jnp.clip(q0 - SWA_WINDOW, 0, s_len - win), SWA_WINDOW)
        kpos = start + lax.broadcasted_iota(jnp.int32, (win, UNIT_Q), 0)
        qpos = q0 + lax.broadcasted_iota(jnp.int32, (win, UNIT_Q), 1)
        mask_scr[qt] = jnp.where(jnp.abs(kpos - qpos) <= SWA_WINDOW, 0.0, NEG_INF)
        starts.append(start)
    units = [(qt, w) for qt in range(n_qt) for w in range(HEADS)]
    qs = [_lane_mask(q_ref[0, qt * UNIT_Q:(qt + 1) * UNIT_Q, (w // SWA_HKV) * LANES:(w // SWA_HKV + 1) * LANES],
                     (w % SWA_HKV) * HEAD_DIM, HEAD_DIM) for qt, w in units]
    head_of = lambda w: (w % SWA_HKV) * SWA_GROUP + w // SWA_HKV

    def score_of(u, c):
        off, rows = chunks[c]
        qt = units[u][0]
        if off < win:
            k = k_ref[0, pl.ds(starts[qt] + off, rows), :]
            return _dot_nt(k, qs[u]) + mask_scr[qt, off:off + rows, :]
        return _dot_nt(kc_ref[0, off - win:off - win + rows, :], qs[u])

    def extra_of(u):
        return jnp.full((1, 1), sink_ref[head_of(units[u][1])] * LOG2E, F32)

    def pv_of(u, p_ref):
        qt, w = units[u]
        o = (_dot_tn(v_ref[0, pl.ds(starts[qt], win), :], p_ref[0:win, :])
             + _dot_tn(vc_ref[0], p_ref[win:win + c_len, :]))
        return o[(w % SWA_HKV) * HEAD_DIM:(w % SWA_HKV + 1) * HEAD_DIM]

    outs = _batched_softmax(len(units), chunks, score_of, extra_of, pv_of, s_scr, p_scr)
    for qt in range(n_qt):
        by_head = {head_of(w): outs[qt * HEADS + w] for w in range(HEADS)}
        o = jnp.concatenate([by_head[h] for h in range(HEADS)], axis=0)
        o_ref[0, qt * UNIT_Q:(qt + 1) * UNIT_Q, :] = o.T.astype(BF16)


def _swa_call(zl, zc, sink, tq):
    b, s, _ = zl.shape
    c = zc.shape[1]
    win = UNIT_Q + 2 * SWA_WINDOW
    return pl.pallas_call(
        _swa_kernel,
        grid=(b, s // tq),
        in_specs=[
            pl.BlockSpec((1, tq, 256), lambda i, j: (i, j, ZC_SQ // 256)),
            pl.BlockSpec((1, s, LANES), lambda i, j: (i, 0, ZC_SK // LANES)),
            pl.BlockSpec((1, s, LANES), lambda i, j: (i, 0, ZC_SV // LANES)),
            pl.BlockSpec((1, c, LANES), lambda i, j: (i, 0, ZC_SK // LANES)),
            pl.BlockSpec((1, c, LANES), lambda i, j: (i, 0, ZC_SV // LANES)),
            pl.BlockSpec(memory_space=pltpu.SMEM),
        ],
        out_specs=pl.BlockSpec((1, tq, 256), lambda i, j: (i, j, 0)),
        out_shape=jax.ShapeDtypeStruct((b, s, 256), BF16),
        scratch_shapes=(_softmax_scratch(win + c, UNIT_Q, HEADS * tq // UNIT_Q)
                        + [pltpu.VMEM((tq // UNIT_Q, win, UNIT_Q), F32)]),
        compiler_params=_params("arbitrary", "arbitrary"),
        name="swa",
    )(zl, zl, zl, zc, zc, sink)


def _na_kernel(rows_n, q_ref, k_ref, v_ref, kc_ref, vc_ref, bias_ref, o_ref, s_scr, p_scr):
    c_len = kc_ref.shape[1]
    n_rb = q_ref.shape[1] // NA_TQ
    nrb = rows_n // 2
    chunks = _key_chunks(NA_NLOC, KEY_CHUNK) + [(NA_NLOC + o, r) for o, r in _key_chunks(c_len, KEY_CHUNK)]
    starts, tids = [], []
    for i in range(n_rb):
        rb = pl.program_id(1) * n_rb + i
        starts.append(pl.multiple_of(
            jnp.clip(2 * rb - NA_WIN_R // 2, 0, rows_n - NA_KROWS) * GRID_W, 2 * GRID_W))
        tids.append(jnp.minimum(rb, 2) + jnp.maximum(rb - (nrb - 3), 0))
    blk = lambda h: slice((h // 2) * LANES, (h // 2 + 1) * LANES)
    units = [(i, h) for i in range(n_rb) for h in range(HEADS)]
    qs = [_lane_mask(q_ref[0, i * NA_TQ:(i + 1) * NA_TQ, blk(h)], (h % 2) * HEAD_DIM, HEAD_DIM)
          for i, h in units]

    def score_of(u, c):
        off, rows = chunks[c]
        i, h = units[u]
        if off < NA_NLOC:
            k = k_ref[0, pl.ds(starts[i] + off, rows), blk(h)]
            return _dot_nt(k, qs[u]) + bias_ref[h, tids[i], off:off + rows, :]
        return _dot_nt(kc_ref[0, off - NA_NLOC:off - NA_NLOC + rows, blk(h)], qs[u])

    def pv_of(u, p_ref):
        i, h = units[u]
        o = (_dot_tn(v_ref[0, pl.ds(starts[i], NA_NLOC), blk(h)], p_ref[0:NA_NLOC, :])
             + _dot_tn(vc_ref[0, :, blk(h)], p_ref[NA_NLOC:NA_NLOC + c_len, :]))
        return _head_rows(o, h)

    outs = _batched_softmax(len(units), chunks, score_of, lambda u: None, pv_of, s_scr, p_scr)
    for i in range(n_rb):
        o = jnp.concatenate(outs[i * HEADS:(i + 1) * HEADS], axis=0)
        o_ref[0, i * NA_TQ:(i + 1) * NA_TQ, :] = o.T.astype(BF16)


def _na_call(zl, zc, bias, tq):
    b, s, _ = zl.shape
    c = zc.shape[1]
    rows_n = s // GRID_W
    return pl.pallas_call(
        functools.partial(_na_kernel, rows_n),
        grid=(b, s // tq),
        in_specs=[
            pl.BlockSpec((1, tq, 256), lambda i, j: (i, j, ZC_NQ // 256)),
            pl.BlockSpec((1, s, 256), lambda i, j: (i, 0, ZC_NK // 256)),
            pl.BlockSpec((1, s, 256), lambda i, j: (i, 0, ZC_NV // 256)),
            pl.BlockSpec((1, c, 256), lambda i, j: (i, 0, ZC_NK // 256)),
            pl.BlockSpec((1, c, 256), lambda i, j: (i, 0, ZC_NV // 256)),
            pl.BlockSpec(bias.shape, lambda i, j: (0, 0, 0, 0), pipeline_mode=pl.Buffered(1)),
        ],
        out_specs=pl.BlockSpec((1, tq, 256), lambda i, j: (i, j, 0)),
        out_shape=jax.ShapeDtypeStruct((b, s, 256), BF16),
        scratch_shapes=_softmax_scratch(NA_NLOC + c, NA_TQ, HEADS * tq // NA_TQ),
        compiler_params=_params("arbitrary", "arbitrary"),
        name="na",
    )(zl, zl, zl, zc, zc, bias)


def _ctx_kernel(lam_init, z_ref, lp_ref, gain_ref, sink_ref, oa_ref, ob_ref, oc_ref, od_ref,
                s_scr, p_scr):
    c_len = z_ref.shape[1]

    def blk(col, j=0):
        return z_ref[0, :, col + j * LANES:col + (j + 1) * LANES]

    units = []
    for h in range(HEADS):
        units.append((_lane_mask(blk(ZC_NQ, h // 2), (h % 2) * HEAD_DIM, HEAD_DIM),
                      ZC_NK + (h // 2) * LANES, ZC_NV + (h // 2) * LANES, (h % 2) * HEAD_DIM, None))
    for h in range(HEADS):
        units.append((blk(ZC_MQ, h), ZC_MK + h * LANES, ZC_MV + (h // 2) * LANES,
                      (h % 2) * HEAD_DIM, None))
    for h in range(HEADS):
        hk, g = h // SWA_GROUP, h % SWA_GROUP
        units.append((_lane_mask(blk(ZC_SQ, g), hk * HEAD_DIM, HEAD_DIM), ZC_SK, ZC_SV,
                      hk * HEAD_DIM, h))
    for w in range(2 * HEADS):
        units.append((_lane_mask(blk(ZC_DQ, w // 4), (w % 4) * DIFF_DH, DIFF_DH),
                      ZC_DK + (w // 4) * LANES, ZC_DV + (w // 4) * LANES, ((w // 2) % 2) * HEAD_DIM, None))

    def score_of(u, c):
        q_m, kcol = units[u][0], units[u][1]
        return _dot_nt(z_ref[0, :, kcol:kcol + LANES], q_m)

    def extra_of(u):
        h = units[u][4]
        return None if h is None else jnp.full((1, 1), sink_ref[h] * LOG2E, F32)

    def pv_of(u, p_ref):
        vcol, r0 = units[u][2], units[u][3]
        return _dot_tn(z_ref[0, :, vcol:vcol + LANES], p_ref[...])[r0:r0 + HEAD_DIM]

    outs = _batched_softmax(len(units), [(0, c_len)], score_of, extra_of, pv_of, s_scr, p_scr)
    for i, ref in enumerate((oa_ref, ob_ref, oc_ref)):
        ref[0] = jnp.concatenate(outs[i * HEADS:(i + 1) * HEADS], axis=0).T.astype(BF16)
    o = _diff_combine(outs[3 * HEADS:], _diff_lambda(lp_ref[...], lam_init))
    od_ref[0] = (o.T * gain_ref[...]).astype(BF16)


def _ctx_call(zc, lp, gain, sink, lam_init):
    b, c, _ = zc.shape
    out = jax.ShapeDtypeStruct((b, c, 256), BF16)
    ospec = pl.BlockSpec((1, c, 256), lambda i: (i, 0, 0))
    return pl.pallas_call(
        functools.partial(_ctx_kernel, lam_init),
        grid=(b,),
        in_specs=[
            pl.BlockSpec((1, c, Z_COLS), lambda i: (i, 0, 0)),
            pl.BlockSpec(lp.shape, lambda i: (0, 0)),
            pl.BlockSpec(gain.shape, lambda i: (0, 0)),
            pl.BlockSpec(memory_space=pltpu.SMEM),
        ],
        out_specs=[ospec] * 4,
        out_shape=[out] * 4,
        scratch_shapes=_softmax_scratch(c, c, 5 * HEADS),
        compiler_params=_params("arbitrary"),
        name="ctx_attn",
    )(zc, lp, gain, sink)


def _mlp_kernel(final, x_ref, ma_ref, mb_ref, mc_ref, md_ref, mod_ref, g_ref, gf_ref,
                wout_ref, wup_ref, wdn_ref, o_ref):
    x = x_ref[0]
    mod = mod_ref[0]
    attn = None
    for i, m_ref in enumerate((ma_ref, mb_ref, mc_ref, md_ref)):
        part = _dot(m_ref[0], wout_ref[i * 256:(i + 1) * 256, :])
        attn = part if attn is None else attn + part
    x = x + mod[2:3] * attn
    h = x * _rms_scale(x, D_MODEL) * g_ref[...]
    h = (h * (1.0 + mod[4:5]) + mod[3:4]).astype(BF16)
    acc = None
    ck = 1024
    for c in range(D_FF // ck):
        u = jnp.maximum(_dot(h, wup_ref[:, c * ck:(c + 1) * ck]), 0.0)
        part = _dot((u * u).astype(BF16), wdn_ref[c * ck:(c + 1) * ck, :])
        acc = part if acc is None else acc + part
    x = x + mod[5:6] * acc
    if final:
        x = x * _rms_scale(x, D_MODEL) * gf_ref[...]
    o_ref[0] = x


def _mlp_call(x3, mixes, mod, mod_row, g, gf, wout, wup, wdn, tm, final):
    nb, n, d = x3.shape
    const = lambda t, b: (0, 0)
    tok = lambda t, b: (b, t, 0)
    mod_map = (lambda t, b: (b, 0, 0)) if mod_row is None else (lambda t, b: (mod_row, 0, 0))
    single = pl.Buffered(1)
    return pl.pallas_call(
        functools.partial(_mlp_kernel, final),
        grid=(n // tm, nb),
        in_specs=[pl.BlockSpec((1, tm, d), tok)]
        + [pl.BlockSpec((1, tm, 256), tok)] * 4
        + [
            pl.BlockSpec((1, 6, d), mod_map),
            pl.BlockSpec((1, d), const),
            pl.BlockSpec((1, d), const),
            pl.BlockSpec(wout.shape, const, pipeline_mode=single),
            pl.BlockSpec(wup.shape, const, pipeline_mode=single),
            pl.BlockSpec(wdn.shape, const, pipeline_mode=single),
        ],
        out_specs=pl.BlockSpec((1, tm, d), tok),
        out_shape=jax.ShapeDtypeStruct((nb, n, d), F32),
        compiler_params=_params("arbitrary", "arbitrary"),
        name="mlp",
    )(x3, *mixes, mod, g, gf, wout, wup, wdn)


def _rope_tables(s_len):
    t = np.arange(s_len)
    row, col = t // GRID_W, t % GRID_W

    def axial(dim):
        n_freq = dim // 4
        freqs = jnp.asarray(ROPE_BASE, F32) ** (-jnp.arange(n_freq, dtype=F32) / n_freq)
        ang = jnp.concatenate([jnp.asarray(row, F32)[:, None] * freqs,
                               jnp.asarray(col, F32)[:, None] * freqs], axis=-1)
        return jnp.cos(ang), jnp.sin(ang)

    def group(cs, n_groups, scale):
        cos, sin = cs
        c = jnp.concatenate([cos, cos], axis=-1) * scale
        s = jnp.concatenate([-sin, sin], axis=-1) * scale
        return jnp.tile(c, (1, n_groups)), jnp.tile(s, (1, n_groups))

    r64, r32 = axial(HEAD_DIM), axial(MLA_ROPE)
    ones = lambda n, v=1.0: jnp.full((s_len, n), v, F32)
    zeros = lambda n: jnp.zeros((s_len, n), F32)
    mla_scale = (MLA_NOPE + MLA_ROPE) ** -0.5 * LOG2E
    mq_c, mq_s = group(r32, 1, mla_scale)
    parts = [
        group(r64, 4, HEAD_DIM ** -0.5 * LOG2E), group(r64, 2, 1.0),
        group(r32, 8, DIFF_DH ** -0.5 * LOG2E), group(r32, 8, 1.0),
        (ones(MLA_Q_LORA), zeros(MLA_Q_LORA)), group(r32, 1, 1.0), (ones(32), zeros(32)),
    ] + [(jnp.concatenate([ones(MLA_NOPE, mla_scale), mq_c, ones(32, mla_scale)], axis=-1),
          jnp.concatenate([zeros(MLA_NOPE), mq_s, zeros(32)], axis=-1))] * HEADS
    cos = jnp.concatenate([p[0] for p in parts], axis=-1)
    sin = jnp.concatenate([p[1] for p in parts], axis=-1)
    return cos, sin, cos[0:1]


def _permute_w_in(w_in):
    offs = np.cumsum([0, 256, 256, 256, MLA_Q_LORA, MLA_KV_LORA, MLA_ROPE, 256, 128, 128, 256, 256, 256])
    sec = lambda i: w_in[..., offs[i]:offs[i + 1]]
    sq = sec(6)
    sq = sq.reshape(sq.shape[:-1] + (SWA_HKV, SWA_GROUP, HEAD_DIM))
    sq = jnp.swapaxes(sq, -3, -2).reshape(w_in.shape[:-1] + (256,))
    pad = jnp.zeros(w_in.shape[:-1] + (32,), w_in.dtype)
    cols = [sq, sec(7), sec(9), sec(10), sec(4), sec(3), sec(5), pad,
            sec(0), sec(1), sec(2), sec(11), sec(8)]
    return jnp.concatenate(cols, axis=-1).astype(BF16)


def _mla_weights(w_uq, w_ukv):
    depth = w_uq.shape[0]
    uq = w_uq.reshape(depth, MLA_Q_LORA, HEADS, MLA_NOPE + MLA_ROPE)
    uq = jnp.pad(uq, ((0, 0), (0, 256 - MLA_Q_LORA), (0, 0), (0, LANES - MLA_NOPE - MLA_ROPE)))
    uq = uq.reshape(depth, 256, HEADS * LANES).astype(BF16)
    ukv = w_ukv.reshape(depth, MLA_KV_LORA, HEADS, 2, HEAD_DIM)
    uk = jnp.pad(ukv[:, :, :, 0], ((0, 0), (0, 0), (0, 0), (0, LANES - MLA_NOPE)))
    uk = uk.reshape(depth, MLA_KV_LORA, HEADS * LANES).astype(BF16)
    uv = ukv[:, :, :, 1].reshape(depth, MLA_KV_LORA, HEADS * HEAD_DIM).astype(BF16)
    e = np.zeros((256, HEADS * LANES), np.float32)
    for h in range(HEADS):
        for j in range(MLA_ROPE):
            e[MLA_Q_LORA + j, h * LANES + MLA_NOPE + j] = 1.0
    return uq, uk, uv, jnp.asarray(e, BF16)


def _na_bias_tables(rpb, rows_n):
    kr_n = min(NA_WIN_R, rows_n)
    col = np.arange(GRID_W)
    c0 = np.clip(col - NA_WIN_C // 2, 0, GRID_W - NA_WIN_C)
    col_ok_t = ((col[None, :] >= c0[:, None]) & (col[None, :] < c0[:, None] + NA_WIN_C)).T
    edge = GRID_W - NA_WIN_C
    ext = jnp.concatenate([jnp.repeat(rpb[..., :1], edge, axis=-1), rpb,
                           jnp.repeat(rpb[..., -1:], edge, axis=-1)], axis=-1)
    toe = jnp.stack([ext[..., GRID_W - 1 - qc:2 * GRID_W - 1 - qc] for qc in range(GRID_W)], axis=-1)
    toe = jnp.where(col_ok_t, toe.astype(F32) * LOG2E, NEG_INF)
    neg = jnp.full(toe.shape[:2] + (GRID_W, GRID_W), NEG_INF, F32)
    nrb = rows_n // 2
    blocks = []
    for rb in (0, 1, 2, nrb - 2, nrb - 1):
        start = int(np.clip(2 * rb - NA_WIN_R // 2, 0, rows_n - NA_KROWS))
        for j in range(NA_KROWS):
            for i in range(2):
                kr, qr = start + j, 2 * rb + i
                r0 = int(np.clip(qr - kr_n // 2, 0, rows_n - kr_n))
                ok = r0 <= kr < r0 + kr_n
                blocks.append(toe[:, :, kr - qr + NA_WIN_R - 1] if ok else neg)
    tab = jnp.stack(blocks, axis=2)
    tab = tab.reshape(tab.shape[:2] + (5, NA_KROWS, 2, GRID_W, GRID_W))
    tab = jnp.swapaxes(tab, -3, -2)
    return tab.reshape(tab.shape[:3] + (NA_NLOC, NA_TQ))


def kernel(x, c, ctx, c_ctx, w_ada, b_ada, norm_attn_g, w_in, na_rpb, mla_q_norm_g, mla_w_uq,
           mla_kv_norm_g, mla_w_ukv, swa_sink, diff_lambda, diff_norm_g, w_out, norm_mlp_g,
           w_up, w_down, final_norm_g):
    b, s, d = x.shape
    c_len = ctx.shape[1]
    depth = w_in.shape[0]
    rows_n = s // GRID_W

    w_in_p = _permute_w_in(w_in)
    uq, uk, uv, e_mat = _mla_weights(mla_w_uq, mla_w_ukv)
    gq = jnp.pad(mla_q_norm_g, ((0, 0), (0, 256 - MLA_Q_LORA)))[:, None, :]
    gkv = mla_kv_norm_g[:, None, :]
    w_out_b, w_up_b, w_dn_b = w_out.astype(BF16), w_up.astype(BF16), w_down.astype(BF16)
    na_bias = _na_bias_tables(na_rpb, rows_n)
    cos, sin, cos_ctx = _rope_tables(s)
    tm_l, tm_c = 512, min(512, b * c_len)
    cos_c = jnp.broadcast_to(cos_ctx, (tm_c, T_COLS))
    sin_c = jnp.zeros((tm_c, T_COLS), F32)

    c_rows = -(-(b + 1) // 8) * 8
    c_all = jnp.zeros((c_rows, d), F32).at[:b].set(c).at[b].set(c_ctx)
    mod = _ada_call(c_all, w_ada, b_ada).reshape(depth, c_rows, 6, d)

    xl = x
    xc = ctx.reshape(1, b * c_len, d)
    for l in range(depth):
        need_ctx = l < depth - 1
        lam_init = 0.8 - 0.6 * math.exp(-0.3 * l)
        g_attn = norm_attn_g[l][None, :]
        g_mlp = norm_mlp_g[l][None, :]
        gf = final_norm_g[None, :]
        d_gain = (jnp.tile(diff_norm_g[l], HEADS) * (1.0 - lam_init))[None, :]
        proj = (w_in_p[l], gq[l], gkv[l], uq[l], uk[l], uv[l], e_mat)
        zl = _inproj_call(xl, mod[l], None, g_attn, cos, sin, True, *proj, tm=tm_l)
        zc = _inproj_call(xc, mod[l], b, g_attn, cos_c, sin_c, False, *proj, tm=tm_c)
        zc = zc.reshape(b, c_len, Z_COLS)
        mixes = (
            _na_call(zl, zc, na_bias[l], tq=512),
            _mla_call(zl, zc, tq=512),
            _swa_call(zl, zc, swa_sink[l], tq=512),
            _diff_call(zl, zc, diff_lambda[l], d_gain, lam_init, tq=512),
        )
        wts = (w_out_b[l], w_up_b[l], w_dn_b[l])
        xl = _mlp_call(xl, mixes, mod[l], None, g_mlp, gf, *wts, tm=tm_l, final=not need_ctx)
        if need_ctx:
            mixes_c = _ctx_call(zc, diff_lambda[l], d_gain, swa_sink[l], lam_init)
            mixes_c = [m.reshape(1, b * c_len, 256) for m in mixes_c]
            xc = _mlp_call(xc, mixes_c, mod[l], b, g_mlp, gf, *wts, tm=tm_c, final=False)
    return xl
```

```python
import functools
import math

import jax
import jax.numpy as jnp
import numpy as np
from jax import lax
from jax.experimental import pallas as pl
from jax.experimental.pallas import tpu as pltpu

F32 = jnp.float32
BF16 = jnp.bfloat16

D_MODEL = 1024
GRID_W = 64
HEADS = 4
HEAD_DIM = 64
NA_WIN_R = 8
NA_WIN_C = 16
MLA_Q_LORA = 192
MLA_KV_LORA = 128
MLA_NOPE = 64
MLA_ROPE = 32
SWA_HKV = 2
SWA_GROUP = 2
SWA_WINDOW = 128
DIFF_DH = 32
D_FF = 4 * D_MODEL
ROPE_BASE = 10000.0
EPS = 1e-6
NEG_INF = -1e30
LOG2E = math.log2(math.e)

LANES = 128
VMEM_LIMIT = 56 * 1024 * 1024

ZC_DQ, ZC_DK, ZC_SQ, ZC_NQ, ZC_NK, ZC_NV, ZC_DV, ZC_MV = 0, 256, 512, 768, 1024, 1280, 1536, 1792
ZC_MQ, ZC_MK, ZC_SK, ZC_SV = 2048, 2560, 3072, 3200
Z_COLS = 3328
W_R64, W_R32, W_M, W_N, W_COLS = 0, 384, 896, 1280, 2432
T_R64, T_R32, T_B12, T_MQ, T_COLS = 0, 384, 896, 1152, 1664

NA_KROWS = 10
NA_TQ = 2 * GRID_W
NA_NLOC = NA_KROWS * GRID_W
KEY_CHUNK = 256
UNIT_Q = 256


def _params(*sem):
    return pltpu.CompilerParams(dimension_semantics=sem, vmem_limit_bytes=VMEM_LIMIT)


def _dot(a, b):
    return jnp.dot(a, b, preferred_element_type=F32)


def _dot_nt(a, b):
    return lax.dot_general(a, b, (((1,), (1,)), ((), ())), preferred_element_type=F32)


def _dot_tn(a, b):
    return lax.dot_general(a, b, (((0,), (0,)), ((), ())), preferred_element_type=F32)


def _rms_scale(x, n):
    return lax.rsqrt(jnp.sum(x * x, axis=-1, keepdims=True) * (1.0 / n) + EPS)


def _ada_kernel(c_ref, w_ref, b_ref, o_ref):
    c = c_ref[...]
    act = (c * (1.0 / (1.0 + jnp.exp(-c)))).astype(BF16)
    o_ref[0] = _dot(act, w_ref[0].astype(BF16)) + b_ref[0]


def _ada_call(c_all, w_ada, b_ada):
    depth, d, n = w_ada.shape
    rows = c_all.shape[0]
    tn = 1536
    return pl.pallas_call(
        _ada_kernel,
        grid=(depth, n // tn),
        in_specs=[
            pl.BlockSpec((rows, d), lambda l, j: (0, 0)),
            pl.BlockSpec((1, d, tn), lambda l, j: (l, 0, j)),
            pl.BlockSpec((1, 1, tn), lambda l, j: (l, 0, j)),
        ],
        out_specs=pl.BlockSpec((1, rows, tn), lambda l, j: (l, 0, j)),
        out_shape=jax.ShapeDtypeStruct((depth, rows, n), F32),
        compiler_params=_params("arbitrary", "arbitrary"),
        name="ada",
    )(c_all, w_ada, b_ada.reshape(depth, 1, n))


def _rope_blocks(z, cos_ref, sin_ref, tcol, half):
    tm = z.shape[0]
    lane = lax.broadcasted_iota(jnp.int32, (tm, LANES), 1)
    first = (lane % (2 * half)) < half
    out = []
    for j in range(z.shape[1] // LANES):
        xb = z[:, j * LANES:(j + 1) * LANES]
        rot = jnp.where(first, pltpu.roll(xb, LANES - half, 1), pltpu.roll(xb, half, 1))
        c0 = tcol + j * LANES
        out.append(xb * cos_ref[:, c0:c0 + LANES] + rot * sin_ref[:, c0:c0 + LANES])
    return out


def _inproj_kernel(x_ref, mod_ref, g_ref, cos_ref, sin_ref, w_ref, gq_ref, gkv_ref,
                   wuq_ref, wukvk_ref, wukvv_ref, e_ref, o_ref):
    x = x_ref[0]
    mod = mod_ref[0]
    h = x * _rms_scale(x, D_MODEL) * g_ref[...]
    h = (h * (1.0 + mod[1:2]) + mod[0:1]).astype(BF16)

    def put(col, val):
        o_ref[0, :, col:col + val.shape[1]] = val.astype(BF16)

    z64 = _dot(h, w_ref[:, W_R64:W_R32])
    z32 = _dot(h, w_ref[:, W_R32:W_M])
    r = _rope_blocks(z64, cos_ref, sin_ref, T_R64, HEAD_DIM // 2)
    put(ZC_SQ, r[0]); put(ZC_SQ + LANES, r[1]); put(ZC_SK, r[2])
    zm = _dot(h, w_ref[:, W_M:W_N])
    r = _rope_blocks(z32, cos_ref, sin_ref, T_R32, DIFF_DH // 2)
    for j in range(4):
        put(ZC_DQ + j * LANES, r[j])
    zn = _dot(h, w_ref[:, W_N:W_COLS])

    ckv = zm[:, 0:MLA_KV_LORA]
    b12 = zm[:, MLA_KV_LORA:]
    ckv_n = (ckv * _rms_scale(ckv, MLA_KV_LORA) * gkv_ref[...]).astype(BF16)
    lane = lax.broadcasted_iota(jnp.int32, b12.shape, 1)
    cq = jnp.where(lane < MLA_Q_LORA, b12, 0.0)
    cq_n = (cq * _rms_scale(cq, MLA_Q_LORA) * gq_ref[...]).astype(BF16)
    zq = _dot(cq_n, wuq_ref[...])
    put(ZC_NQ, zn[:, 0:256] * (HEAD_DIM ** -0.5 * LOG2E))
    put(ZC_NK, zn[:, 256:1024])
    put(ZC_SV, zn[:, 1024:1152])
    q = _rope_blocks(zq, cos_ref, sin_ref, T_MQ, MLA_ROPE // 2)
    for j in range(HEADS):
        put(ZC_MQ + j * LANES, q[j])
    kr = _rope_blocks(b12, cos_ref, sin_ref, T_B12, MLA_ROPE // 2)
    kr = jnp.concatenate(kr, axis=1).astype(BF16)
    put(ZC_MK, _dot(ckv_n, wukvk_ref[...]) + _dot(kr, e_ref[...]))
    put(ZC_MV, _dot(ckv_n, wukvv_ref[...]))


def _inproj_call(x3, mod, mod_row, g, cos, sin, table_per_tile, w, gq, gkv, wuq, wukvk, wukvv, e, tm):
    nb, n, d = x3.shape
    nt = n // tm
    const = lambda t, b: (0, 0)
    tab_map = (lambda t, b: (t, 0)) if table_per_tile else const
    mod_map = (lambda t, b: (b, 0, 0)) if mod_row is None else (lambda t, b: (mod_row, 0, 0))
    return pl.pallas_call(
        _inproj_kernel,
        grid=(nt, nb),
        in_specs=[
            pl.BlockSpec((1, tm, d), lambda t, b: (b, t, 0)),
            pl.BlockSpec((1, 6, d), mod_map),
            pl.BlockSpec((1, d), const),
            pl.BlockSpec((tm, T_COLS), tab_map),
            pl.BlockSpec((tm, T_COLS), tab_map),
            pl.BlockSpec(w.shape, const, pipeline_mode=pl.Buffered(1)),
            pl.BlockSpec(gq.shape, const),
            pl.BlockSpec(gkv.shape, const),
            pl.BlockSpec(wuq.shape, const),
            pl.BlockSpec(wukvk.shape, const),
            pl.BlockSpec(wukvv.shape, const),
            pl.BlockSpec(e.shape, const),
        ],
        out_specs=pl.BlockSpec((1, tm, Z_COLS), lambda t, b: (b, t, 0)),
        out_shape=jax.ShapeDtypeStruct((nb, n, Z_COLS), BF16),
        compiler_params=_params("arbitrary", "arbitrary"),
        name="inproj",
    )(x3, mod, g, cos, sin, w, gq, gkv, wuq, wukvk, wukvv, e)


def _lane_mask(q, lo, width):
    lane = lax.broadcasted_iota(jnp.int32, q.shape, 1)
    keep = jnp.where(lane >= lo, lane, LANES) < lo + width
    return jnp.where(keep, q.astype(F32), 0.0).astype(BF16)


def _localmax_softmax(n_units, chunks, score_of, extra_of, pv_of, e_scr):
    outs = [None] * n_units
    mloc, mfin = {}, {}
    for t in range(n_units + 1):
        acc = None
        if t < n_units:
            mloc[t] = []
        for c, (off, rows) in enumerate(chunks):
            if t < n_units:
                s = score_of(t, c)
                mc = jnp.max(s, axis=0, keepdims=True)
                e_scr[t % 2, off:off + rows, :] = jnp.exp2(s - mc).astype(BF16)
                mloc[t].append(mc)
            if t >= 1:
                u = t - 1
                pv = pv_of(u, c, e_scr[u % 2, off:off + rows, :]) * jnp.exp2(mloc[u][c] - mfin[u])
                acc = pv if acc is None else acc + pv
        if t < n_units:
            m = mloc[t][0]
            for mc in mloc[t][1:]:
                m = jnp.maximum(m, mc)
            x = extra_of(t)
            mfin[t] = m if x is None else jnp.maximum(m, x)
        if t >= 1:
            l = acc[HEAD_DIM:HEAD_DIM + 1]
            x = extra_of(t - 1)
            if x is not None:
                l = l + jnp.exp2(x - mfin[t - 1])
            outs[t - 1] = acc[0:HEAD_DIM] * (1.0 / l)
    return outs


def _localmax_batched(n_units, chunks, score_of, extra_of, pv_of, e_scr):
    mloc, lloc, mfin = [], [], []
    for u in range(n_units):
        ms, ls = [], []
        for c, (off, rows) in enumerate(chunks):
            s = score_of(u, c)
            mc = jnp.max(s, axis=0, keepdims=True)
            e = jnp.exp2(s - mc)
            ls.append(jnp.sum(e, axis=0, keepdims=True))
            e_scr[u, off:off + rows, :] = e.astype(BF16)
            ms.append(mc)
        m = ms[0]
        for mc in ms[1:]:
            m = jnp.maximum(m, mc)
        x = extra_of(u)
        mloc.append(ms); lloc.append(ls); mfin.append(m if x is None else jnp.maximum(m, x))
    outs = []
    for u in range(n_units):
        acc = l = None
        for c, (off, rows) in enumerate(chunks):
            alpha = jnp.exp2(mloc[u][c] - mfin[u])
            pv = pv_of(u, c, e_scr[u, off:off + rows, :]) * alpha
            acc = pv if acc is None else acc + pv
            l = lloc[u][c] * alpha if l is None else l + lloc[u][c] * alpha
        x = extra_of(u)
        if x is not None:
            l = l + jnp.exp2(x - mfin[u])
        outs.append(acc * (1.0 / l))
    return outs


def _head_rows(o, h):
    return o[(h % 2) * HEAD_DIM:(h % 2 + 1) * HEAD_DIM]


def _diff_lambda(lp, lam_init):
    a = jnp.sum(lp[0:1] * lp[1:2], axis=-1, keepdims=True)
    b = jnp.sum(lp[2:3] * lp[3:4], axis=-1, keepdims=True)
    return jnp.exp(a) - jnp.exp(b) + lam_init


def _diff_combine(outs, lam):
    res = []
    for h in range(HEADS):
        o = outs[2 * h] - lam * outs[2 * h + 1]
        ms = jnp.sum(o * o, axis=0, keepdims=True) * (1.0 / HEAD_DIM)
        res.append(o * lax.rsqrt(ms + EPS))
    return jnp.concatenate(res, axis=0)


def _key_chunks(total, size):
    return [(o, min(size, total - o)) for o in range(0, total, size)]


def _exp_scratch(nk, tq, slots=2):
    return [pltpu.VMEM((slots, nk, tq), BF16)]


VT_ROWS = HEAD_DIM + 16


def _transpose_values(vt_scr, srcs):
    for ref, off, n in srcs:
        for r in range(0, n, KEY_CHUNK):
            rows = min(KEY_CHUNK, n - r)
            vt = ref[0, r:r + rows, :].astype(F32).T
            for h in range(HEADS):
                vt_scr[h * VT_ROWS:h * VT_ROWS + HEAD_DIM, off + r:off + r + rows] = (
                    vt[h * HEAD_DIM:(h + 1) * HEAD_DIM].astype(BF16))
                vt_scr[h * VT_ROWS + HEAD_DIM:(h + 1) * VT_ROWS, off + r:off + r + rows] = (
                    jnp.ones((VT_ROWS - HEAD_DIM, rows), BF16))


def _mla_kernel(q_ref, k_ref, v_ref, kc_ref, vc_ref, o_ref, e_scr, vt_scr):
    s_len, c_len = k_ref.shape[1], kc_ref.shape[1]

    @pl.when(pl.program_id(1) == 0)
    def _():
        _transpose_values(vt_scr, [(v_ref, 0, s_len), (vc_ref, s_len, c_len)])

    chunks = _key_chunks(s_len, KEY_CHUNK) + [(s_len + o, r) for o, r in _key_chunks(c_len, KEY_CHUNK)]
    units = [(qt, h) for qt in range(q_ref.shape[1] // UNIT_Q) for h in range(HEADS)]
    qs = [q_ref[0, qt * UNIT_Q:(qt + 1) * UNIT_Q, h * LANES:(h + 1) * LANES] for qt, h in units]

    def score_of(u, c):
        off, rows = chunks[c]
        h = units[u][1]
        sl = slice(h * LANES, (h + 1) * LANES)
        if off < s_len:
            return _dot_nt(k_ref[0, off:off + rows, sl], qs[u])
        return _dot_nt(kc_ref[0, off - s_len:off - s_len + rows, sl], qs[u])

    def pv_of(u, c, e):
        h = units[u][1]
        off, rows = chunks[c]
        return _dot(vt_scr[h * VT_ROWS:(h + 1) * VT_ROWS, off:off + rows], e)

    outs = _localmax_softmax(len(units), chunks, score_of, lambda u: None, pv_of, e_scr)
    for qt in range(len(units) // HEADS):
        o = jnp.concatenate(outs[qt * HEADS:(qt + 1) * HEADS], axis=0)
        o_ref[0, qt * UNIT_Q:(qt + 1) * UNIT_Q, :] = o.T.astype(BF16)


def _mla_call(zl, zc, tq):
    b, s, _ = zl.shape
    c = zc.shape[1]
    return pl.pallas_call(
        _mla_kernel,
        grid=(b, s // tq),
        in_specs=[
            pl.BlockSpec((1, tq, 512), lambda i, j: (i, j, ZC_MQ // 512)),
            pl.BlockSpec((1, s, 512), lambda i, j: (i, 0, ZC_MK // 512)),
            pl.BlockSpec((1, s, 256), lambda i, j: (i, 0, ZC_MV // 256)),
            pl.BlockSpec((1, c, 512), lambda i, j: (i, 0, ZC_MK // 512)),
            pl.BlockSpec((1, c, 256), lambda i, j: (i, 0, ZC_MV // 256)),
        ],
        out_specs=pl.BlockSpec((1, tq, 256), lambda i, j: (i, j, 0)),
        out_shape=jax.ShapeDtypeStruct((b, s, 256), BF16),
        scratch_shapes=_exp_scratch(s + c, UNIT_Q) + [pltpu.VMEM((HEADS * VT_ROWS, s + c), BF16)],
        compiler_params=_params("arbitrary", "arbitrary"),
        name="mla",
    )(zl, zl, zl, zc, zc)


def _diff_kernel(lam_init, q_ref, k_ref, v_ref, kc_ref, vc_ref, lp_ref, gain_ref, o_ref,
                 e_scr, vt_scr):
    s_len, c_len = k_ref.shape[1], kc_ref.shape[1]

    @pl.when(pl.program_id(1) == 0)
    def _():
        _transpose_values(vt_scr, [(v_ref, 0, s_len), (vc_ref, s_len, c_len)])

    chunks = _key_chunks(s_len, KEY_CHUNK) + [(s_len + o, r) for o, r in _key_chunks(c_len, KEY_CHUNK)]
    n_sub = 2 * HEADS
    units = [(qt, w) for qt in range(q_ref.shape[1] // UNIT_Q) for w in range(n_sub)]
    qs = [_lane_mask(q_ref[0, qt * UNIT_Q:(qt + 1) * UNIT_Q, (w // 4) * LANES:(w // 4 + 1) * LANES],
                     (w % 4) * DIFF_DH, DIFF_DH) for qt, w in units]

    def score_of(u, c):
        off, rows = chunks[c]
        w = units[u][1]
        sl = slice((w // 4) * LANES, (w // 4 + 1) * LANES)
        if off < s_len:
            return _dot_nt(k_ref[0, off:off + rows, sl], qs[u])
        return _dot_nt(kc_ref[0, off - s_len:off - s_len + rows, sl], qs[u])

    def pv_of(u, c, e):
        h = units[u][1] // 2
        off, rows = chunks[c]
        return _dot(vt_scr[h * VT_ROWS:(h + 1) * VT_ROWS, off:off + rows], e)

    outs = _localmax_softmax(len(units), chunks, score_of, lambda u: None, pv_of, e_scr)
    lam = _diff_lambda(lp_ref[...], lam_init)
    for qt in range(len(units) // n_sub):
        o = _diff_combine(outs[qt * n_sub:(qt + 1) * n_sub], lam)
        o_ref[0, qt * UNIT_Q:(qt + 1) * UNIT_Q, :] = (o.T * gain_ref[...]).astype(BF16)


def _diff_call(zl, zc, lp, gain, lam_init, tq):
    b, s, _ = zl.shape
    c = zc.shape[1]
    return pl.pallas_call(
        functools.partial(_diff_kernel, lam_init),
        grid=(b, s // tq),
        in_specs=[
            pl.BlockSpec((1, tq, 256), lambda i, j: (i, j, ZC_DQ // 256)),
            pl.BlockSpec((1, s, 256), lambda i, j: (i, 0, ZC_DK // 256)),
            pl.BlockSpec((1, s, 256), lambda i, j: (i, 0, ZC_DV // 256)),
            pl.BlockSpec((1, c, 256), lambda i, j: (i, 0, ZC_DK // 256)),
            pl.BlockSpec((1, c, 256), lambda i, j: (i, 0, ZC_DV // 256)),
            pl.BlockSpec(lp.shape, lambda i, j: (0, 0)),
            pl.BlockSpec(gain.shape, lambda i, j: (0, 0)),
        ],
        out_specs=pl.BlockSpec((1, tq, 256), lambda i, j: (i, j, 0)),
        out_shape=jax.ShapeDtypeStruct((b, s, 256), BF16),
        scratch_shapes=_exp_scratch(s + c, UNIT_Q) + [pltpu.VMEM((HEADS * VT_ROWS, s + c), BF16)],
        compiler_params=_params("arbitrary", "arbitrary"),
        name="diff",
    )(zl, zl, zl, zc, zc, lp, gain)


def _swa_kernel(q_ref, k_ref, v_ref, kc_ref, vc_ref, sink_ref, o_ref, e_scr, mask_scr):
    s_len, c_len = k_ref.shape[1], kc_ref.shape[1]
    n_qt = q_ref.shape[1] // UNIT_Q
    win = UNIT_Q + 2 * SWA_WINDOW
    chunks = _key_chunks(win, KEY_CHUNK) + [(win + o, r) for o, r in _key_chunks(c_len, KEY_CHUNK)]
    starts = []
    for qt in range(n_qt):
        q0 = (pl.program_id(1) * n_qt + qt) * UNIT_Q
        start = pl.multiple_of(jnp.clip(q0 - SWA_WINDOW, 0, s_len - win), SWA_WINDOW)
        kpos = start + lax.broadcasted_iota(jnp.int32, (win, UNIT_Q), 0)
        qpos = q0 + lax.broadcasted_iota(jnp.int32, (win, UNIT_Q), 1)
        mask_scr[qt] = jnp.where(jnp.abs(kpos - qpos) <= SWA_WINDOW, 0.0, NEG_INF)
        starts.append(start)
    units = [(qt, w) for qt in range(n_qt) for w in range(HEADS)]
    qs = [_lane_mask(q_ref[0, qt * UNIT_Q:(qt + 1) * UNIT_Q, (w // SWA_HKV) * LANES:(w // SWA_HKV + 1) * LANES],
                     (w % SWA_HKV) * HEAD_DIM, HEAD_DIM) for qt, w in units]
    head_of = lambda w: (w % SWA_HKV) * SWA_GROUP + w // SWA_HKV

    def score_of(u, c):
        off, rows = chunks[c]
        qt = units[u][0]
        if off < win:
            k = k_ref[0, pl.ds(starts[qt] + off, rows), :]
            return _dot_nt(k, qs[u]) + mask_scr[qt, off:off + rows, :]
        return _dot_nt(kc_ref[0, off - win:off - win + rows, :], qs[u])

    def extra_of(u):
        return jnp.full((1, 1), sink_ref[head_of(units[u][1])] * LOG2E, F32)

    def pv_of(u, c, e):
        qt, w = units[u]
        off, rows = chunks[c]
        if off < win:
            o = _dot_tn(v_ref[0, pl.ds(starts[qt] + off, rows), :], e)
        else:
            o = _dot_tn(vc_ref[0, off - win:off - win + rows, :], e)
        return o[(w % SWA_HKV) * HEAD_DIM:(w % SWA_HKV + 1) * HEAD_DIM]

    outs = _localmax_batched(len(units), chunks, score_of, extra_of, pv_of, e_scr)
    for qt in range(n_qt):
        by_head = {head_of(w): outs[qt * HEADS + w] for w in range(HEADS)}
        o = jnp.concatenate([by_head[h] for h in range(HEADS)], axis=0)
        o_ref[0, qt * UNIT_Q:(qt + 1) * UNIT_Q, :] = o.T.astype(BF16)


def _swa_call(zl, zc, sink, tq):
    b, s, _ = zl.shape
    c = zc.shape[1]
    win = UNIT_Q + 2 * SWA_WINDOW
    return pl.pallas_call(
        _swa_kernel,
        grid=(b, s // tq),
        in_specs=[
            pl.BlockSpec((1, tq, 256), lambda i, j: (i, j, ZC_SQ // 256)),
            pl.BlockSpec((1, s, LANES), lambda i, j: (i, 0, ZC_SK // LANES)),
            pl.BlockSpec((1, s, LANES), lambda i, j: (i, 0, ZC_SV // LANES)),
            pl.BlockSpec((1, c, LANES), lambda i, j: (i, 0, ZC_SK // LANES)),
            pl.BlockSpec((1, c, LANES), lambda i, j: (i, 0, ZC_SV // LANES)),
            pl.BlockSpec(memory_space=pltpu.SMEM),
        ],
        out_specs=pl.BlockSpec((1, tq, 256), lambda i, j: (i, j, 0)),
        out_shape=jax.ShapeDtypeStruct((b, s, 256), BF16),
        scratch_shapes=(_exp_scratch(win + c, UNIT_Q, HEADS * tq // UNIT_Q)
                        + [pltpu.VMEM((tq // UNIT_Q, win, UNIT_Q), F32)]),
        compiler_params=_params("arbitrary", "arbitrary"),
        name="swa",
    )(zl, zl, zl, zc, zc, sink)


def _na_kernel(rows_n, q_ref, k_ref, v_ref, kc_ref, vc_ref, bias_ref, o_ref, e_scr):
    c_len = kc_ref.shape[1]
    n_rb = q_ref.shape[1] // NA_TQ
    nrb = rows_n // 2
    chunks = _key_chunks(NA_NLOC, KEY_CHUNK) + [(NA_NLOC + o, r) for o, r in _key_chunks(c_len, KEY_CHUNK)]
    starts, tids = [], []
    for i in range(n_rb):
        rb = pl.program_id(1) * n_rb + i
        starts.append(pl.multiple_of(
            jnp.clip(2 * rb - NA_WIN_R // 2, 0, rows_n - NA_KROWS) * GRID_W, 2 * GRID_W))
        tids.append(jnp.minimum(rb, 2) + jnp.maximum(rb - (nrb - 3), 0))
    blk = lambda h: slice((h // 2) * LANES, (h // 2 + 1) * LANES)
    units = [(i, h) for i in range(n_rb) for h in range(HEADS)]
    qs = [_lane_mask(q_ref[0, i * NA_TQ:(i + 1) * NA_TQ, blk(h)], (h % 2) * HEAD_DIM, HEAD_DIM)
          for i, h in units]

    def score_of(u, c):
        off, rows = chunks[c]
        i, h = units[u]
        if off < NA_NLOC:
            k = k_ref[0, pl.ds(starts[i] + off, rows), blk(h)]
            return _dot_nt(k, qs[u]) + bias_ref[h, tids[i], off:off + rows, :]
        return _dot_nt(kc_ref[0, off - NA_NLOC:off - NA_NLOC + rows, blk(h)], qs[u])

    def pv_of(u, c, e):
        i, h = units[u]
        off, rows = chunks[c]
        if off < NA_NLOC:
            return _head_rows(_dot_tn(v_ref[0, pl.ds(starts[i] + off, rows), blk(h)], e), h)
        return _head_rows(_dot_tn(vc_ref[0, off - NA_NLOC:off - NA_NLOC + rows, blk(h)], e), h)

    outs = _localmax_batched(len(units), chunks, score_of, lambda u: None, pv_of, e_scr)
    for i in range(n_rb):
        o = jnp.concatenate(outs[i * HEADS:(i + 1) * HEADS], axis=0)
        o_ref[0, i * NA_TQ:(i + 1) * NA_TQ, :] = o.T.astype(BF16)


def _na_call(zl, zc, bias, tq):
    b, s, _ = zl.shape
    c = zc.shape[1]
    rows_n = s // GRID_W
    return pl.pallas_call(
        functools.partial(_na_kernel, rows_n),
        grid=(b, s // tq),
        in_specs=[
            pl.BlockSpec((1, tq, 256), lambda i, j: (i, j, ZC_NQ // 256)),
            pl.BlockSpec((1, s, 256), lambda i, j: (i, 0, ZC_NK // 256)),
            pl.BlockSpec((1, s, 256), lambda i, j: (i, 0, ZC_NV // 256)),
            pl.BlockSpec((1, c, 256), lambda i, j: (i, 0, ZC_NK // 256)),
            pl.BlockSpec((1, c, 256), lambda i, j: (i, 0, ZC_NV // 256)),
            pl.BlockSpec(bias.shape, lambda i, j: (0, 0, 0, 0), pipeline_mode=pl.Buffered(1)),
        ],
        out_specs=pl.BlockSpec((1, tq, 256), lambda i, j: (i, j, 0)),
        out_shape=jax.ShapeDtypeStruct((b, s, 256), BF16),
        scratch_shapes=_exp_scratch(NA_NLOC + c, NA_TQ, HEADS * tq // NA_TQ),
        compiler_params=_params("arbitrary", "arbitrary"),
        name="na",
    )(zl, zl, zl, zc, zc, bias)


def _ctx_kernel(lam_init, z_ref, lp_ref, gain_ref, sink_ref, oa_ref, ob_ref, oc_ref, od_ref,
                e_scr):
    c_len = z_ref.shape[1]

    def blk(col, j=0):
        return z_ref[0, :, col + j * LANES:col + (j + 1) * LANES]

    units = []
    for h in range(HEADS):
        units.append((_lane_mask(blk(ZC_NQ, h // 2), (h % 2) * HEAD_DIM, HEAD_DIM),
                      ZC_NK + (h // 2) * LANES, ZC_NV + (h // 2) * LANES, (h % 2) * HEAD_DIM, None))
    for h in range(HEADS):
        units.append((blk(ZC_MQ, h), ZC_MK + h * LANES, ZC_MV + (h // 2) * LANES,
                      (h % 2) * HEAD_DIM, None))
    for h in range(HEADS):
        hk, g = h // SWA_GROUP, h % SWA_GROUP
        units.append((_lane_mask(blk(ZC_SQ, g), hk * HEAD_DIM, HEAD_DIM), ZC_SK, ZC_SV,
                      hk * HEAD_DIM, h))
    for w in range(2 * HEADS):
        units.append((_lane_mask(blk(ZC_DQ, w // 4), (w % 4) * DIFF_DH, DIFF_DH),
                      ZC_DK + (w // 4) * LANES, ZC_DV + (w // 4) * LANES, ((w // 2) % 2) * HEAD_DIM, None))

    def score_of(u, c):
        q_m, kcol = units[u][0], units[u][1]
        return _dot_nt(z_ref[0, :, kcol:kcol + LANES], q_m)

    def extra_of(u):
        h = units[u][4]
        return None if h is None else jnp.full((1, 1), sink_ref[h] * LOG2E, F32)

    def pv_of(u, c, e):
        vcol, r0 = units[u][2], units[u][3]
        return _dot_tn(z_ref[0, :, vcol:vcol + LANES], e)[r0:r0 + HEAD_DIM]

    outs = _localmax_batched(len(units), [(0, c_len)], score_of, extra_of, pv_of, e_scr)
    for i, ref in enumerate((oa_ref, ob_ref, oc_ref)):
        ref[0] = jnp.concatenate(outs[i * HEADS:(i + 1) * HEADS], axis=0).T.astype(BF16)
    o = _diff_combine(outs[3 * HEADS:], _diff_lambda(lp_ref[...], lam_init))
    od_ref[0] = (o.T * gain_ref[...]).astype(BF16)


def _ctx_call(zc, lp, gain, sink, lam_init):
    b, c, _ = zc.shape
    out = jax.ShapeDtypeStruct((b, c, 256), BF16)
    ospec = pl.BlockSpec((1, c, 256), lambda i: (i, 0, 0))
    return pl.pallas_call(
        functools.partial(_ctx_kernel, lam_init),
        grid=(b,),
        in_specs=[
            pl.BlockSpec((1, c, Z_COLS), lambda i: (i, 0, 0)),
            pl.BlockSpec(lp.shape, lambda i: (0, 0)),
            pl.BlockSpec(gain.shape, lambda i: (0, 0)),
            pl.BlockSpec(memory_space=pltpu.SMEM),
        ],
        out_specs=[ospec] * 4,
        out_shape=[out] * 4,
        scratch_shapes=_exp_scratch(c, c, 5 * HEADS),
        compiler_params=_params("arbitrary"),
        name="ctx_attn",
    )(zc, lp, gain, sink)


def _mlp_kernel(final, x_ref, ma_ref, mb_ref, mc_ref, md_ref, mod_ref, g_ref, gf_ref,
                wout_ref, wup_ref, wdn_ref, o_ref):
    x = x_ref[0]
    mod = mod_ref[0]
    attn = None
    for i, m_ref in enumerate((ma_ref, mb_ref, mc_ref, md_ref)):
        part = _dot(m_ref[0], wout_ref[i * 256:(i + 1) * 256, :])
        attn = part if attn is None else attn + part
    x = x + mod[2:3] * attn
    h = x * _rms_scale(x, D_MODEL) * g_ref[...]
    h = (h * (1.0 + mod[4:5]) + mod[3:4]).astype(BF16)
    acc = None
    ck = 1024
    for c in range(D_FF // ck):
        u = jnp.maximum(_dot(h, wup_ref[:, c * ck:(c + 1) * ck]), 0.0)
        part = _dot((u * u).astype(BF16), wdn_ref[c * ck:(c + 1) * ck, :])
        acc = part if acc is None else acc + part
    x = x + mod[5:6] * acc
    if final:
        x = x * _rms_scale(x, D_MODEL) * gf_ref[...]
    o_ref[0] = x


def _mlp_call(x3, mixes, mod, mod_row, g, gf, wout, wup, wdn, tm, final):
    nb, n, d = x3.shape
    const = lambda t, b: (0, 0)
    tok = lambda t, b: (b, t, 0)
    mod_map = (lambda t, b: (b, 0, 0)) if mod_row is None else (lambda t, b: (mod_row, 0, 0))
    single = pl.Buffered(1)
    return pl.pallas_call(
        functools.partial(_mlp_kernel, final),
        grid=(n // tm, nb),
        in_specs=[pl.BlockSpec((1, tm, d), tok)]
        + [pl.BlockSpec((1, tm, 256), tok)] * 4
        + [
            pl.BlockSpec((1, 6, d), mod_map),
            pl.BlockSpec((1, d), const),
            pl.BlockSpec((1, d), const),
            pl.BlockSpec(wout.shape, const, pipeline_mode=single),
            pl.BlockSpec(wup.shape, const, pipeline_mode=single),
            pl.BlockSpec(wdn.shape, const, pipeline_mode=single),
        ],
        out_specs=pl.BlockSpec((1, tm, d), tok),
        out_shape=jax.ShapeDtypeStruct((nb, n, d), F32),
        compiler_params=_params("arbitrary", "arbitrary"),
        name="mlp",
    )(x3, *mixes, mod, g, gf, wout, wup, wdn)


def _rope_tables(s_len):
    t = np.arange(s_len)
    row, col = t // GRID_W, t % GRID_W

    def axial(dim):
        n_freq = dim // 4
        freqs = jnp.asarray(ROPE_BASE, F32) ** (-jnp.arange(n_freq, dtype=F32) / n_freq)
        ang = jnp.concatenate([jnp.asarray(row, F32)[:, None] * freqs,
                               jnp.asarray(col, F32)[:, None] * freqs], axis=-1)
        return jnp.cos(ang), jnp.sin(ang)

    def group(cs, n_groups, scale):
        cos, sin = cs
        c = jnp.concatenate([cos, cos], axis=-1) * scale
        s = jnp.concatenate([-sin, sin], axis=-1) * scale
        return jnp.tile(c, (1, n_groups)), jnp.tile(s, (1, n_groups))

    r64, r32 = axial(HEAD_DIM), axial(MLA_ROPE)
    ones = lambda n, v=1.0: jnp.full((s_len, n), v, F32)
    zeros = lambda n: jnp.zeros((s_len, n), F32)
    mla_scale = (MLA_NOPE + MLA_ROPE) ** -0.5 * LOG2E
    mq_c, mq_s = group(r32, 1, mla_scale)
    parts = [
        group(r64, 4, HEAD_DIM ** -0.5 * LOG2E), group(r64, 2, 1.0),
        group(r32, 8, DIFF_DH ** -0.5 * LOG2E), group(r32, 8, 1.0),
        (ones(MLA_Q_LORA), zeros(MLA_Q_LORA)), group(r32, 1, 1.0), (ones(32), zeros(32)),
    ] + [(jnp.concatenate([ones(MLA_NOPE, mla_scale), mq_c, ones(32, mla_scale)], axis=-1),
          jnp.concatenate([zeros(MLA_NOPE), mq_s, zeros(32)], axis=-1))] * HEADS
    cos = jnp.concatenate([p[0] for p in parts], axis=-1)
    sin = jnp.concatenate([p[1] for p in parts], axis=-1)
    return cos, sin, cos[0:1]


def _permute_w_in(w_in):
    offs = np.cumsum([0, 256, 256, 256, MLA_Q_LORA, MLA_KV_LORA, MLA_ROPE, 256, 128, 128, 256, 256, 256])
    sec = lambda i: w_in[..., offs[i]:offs[i + 1]]
    sq = sec(6)
    sq = sq.reshape(sq.shape[:-1] + (SWA_HKV, SWA_GROUP, HEAD_DIM))
    sq = jnp.swapaxes(sq, -3, -2).reshape(w_in.shape[:-1] + (256,))
    pad = jnp.zeros(w_in.shape[:-1] + (32,), w_in.dtype)
    cols = [sq, sec(7), sec(9), sec(10), sec(4), sec(3), sec(5), pad,
            sec(0), sec(1), sec(2), sec(11), sec(8)]
    return jnp.concatenate(cols, axis=-1).astype(BF16)


def _mla_weights(w_uq, w_ukv):
    depth = w_uq.shape[0]
    uq = w_uq.reshape(depth, MLA_Q_LORA, HEADS, MLA_NOPE + MLA_ROPE)
    uq = jnp.pad(uq, ((0, 0), (0, 256 - MLA_Q_LORA), (0, 0), (0, LANES - MLA_NOPE - MLA_ROPE)))
    uq = uq.reshape(depth, 256, HEADS * LANES).astype(BF16)
    ukv = w_ukv.reshape(depth, MLA_KV_LORA, HEADS, 2, HEAD_DIM)
    uk = jnp.pad(ukv[:, :, :, 0], ((0, 0), (0, 0), (0, 0), (0, LANES - MLA_NOPE)))
    uk = uk.reshape(depth, MLA_KV_LORA, HEADS * LANES).astype(BF16)
    uv = ukv[:, :, :, 1].reshape(depth, MLA_KV_LORA, HEADS * HEAD_DIM).astype(BF16)
    e = np.zeros((256, HEADS * LANES), np.float32)
    for h in range(HEADS):
        for j in range(MLA_ROPE):
            e[MLA_Q_LORA + j, h * LANES + MLA_NOPE + j] = 1.0
    return uq, uk, uv, jnp.asarray(e, BF16)


def _na_bias_tables(rpb, rows_n):
    kr_n = min(NA_WIN_R, rows_n)
    col = np.arange(GRID_W)
    c0 = np.clip(col - NA_WIN_C // 2, 0, GRID_W - NA_WIN_C)
    col_ok_t = ((col[None, :] >= c0[:, None]) & (col[None, :] < c0[:, None] + NA_WIN_C)).T
    edge = GRID_W - NA_WIN_C
    ext = jnp.concatenate([jnp.repeat(rpb[..., :1], edge, axis=-1), rpb,
                           jnp.repeat(rpb[..., -1:], edge, axis=-1)], axis=-1)
    toe = jnp.stack([ext[..., GRID_W - 1 - qc:2 * GRID_W - 1 - qc] for qc in range(GRID_W)], axis=-1)
    toe = jnp.where(col_ok_t, toe.astype(F32) * LOG2E, NEG_INF)
    neg = jnp.full(toe.shape[:2] + (GRID_W, GRID_W), NEG_INF, F32)
    nrb = rows_n // 2
    pairs = {}

    def pair(drs):
        if drs not in pairs:
            pairs[drs] = jnp.concatenate([neg if d is None else toe[:, :, d] for d in drs], axis=-1)
        return pairs[drs]

    blocks = []
    for rb in (0, 1, 2, nrb - 2, nrb - 1):
        start = int(np.clip(2 * rb - NA_WIN_R // 2, 0, rows_n - NA_KROWS))
        for j in range(NA_KROWS):
            drs = []
            for i in range(2):
                kr, qr = start + j, 2 * rb + i
                r0 = int(np.clip(qr - kr_n // 2, 0, rows_n - kr_n))
                drs.append(kr - qr + NA_WIN_R - 1 if r0 <= kr < r0 + kr_n else None)
            blocks.append(pair(tuple(drs)))
    tab = jnp.stack(blocks, axis=2)
    return tab.reshape(tab.shape[:2] + (5, NA_NLOC, NA_TQ))


def kernel(x, c, ctx, c_ctx, w_ada, b_ada, norm_attn_g, w_in, na_rpb, mla_q_norm_g, mla_w_uq,
           mla_kv_norm_g, mla_w_ukv, swa_sink, diff_lambda, diff_norm_g, w_out, norm_mlp_g,
           w_up, w_down, final_norm_g):
    b, s, d = x.shape
    c_len = ctx.shape[1]
    depth = w_in.shape[0]
    rows_n = s // GRID_W

    w_in_p = _permute_w_in(w_in)
    uq, uk, uv, e_mat = _mla_weights(mla_w_uq, mla_w_ukv)
    gq = jnp.pad(mla_q_norm_g, ((0, 0), (0, 256 - MLA_Q_LORA)))[:, None, :]
    gkv = mla_kv_norm_g[:, None, :]
    w_out_b, w_up_b, w_dn_b = w_out.astype(BF16), w_up.astype(BF16), w_down.astype(BF16)
    na_bias = _na_bias_tables(na_rpb, rows_n)
    cos, sin, cos_ctx = _rope_tables(s)
    tm_l, tm_c = 512, min(512, b * c_len)
    cos_c = jnp.broadcast_to(cos_ctx, (tm_c, T_COLS))
    sin_c = jnp.zeros((tm_c, T_COLS), F32)

    c_rows = -(-(b + 1) // 8) * 8
    c_all = jnp.zeros((c_rows, d), F32).at[:b].set(c).at[b].set(c_ctx)
    mod = _ada_call(c_all, w_ada, b_ada).reshape(depth, c_rows, 6, d)

    xl = x
    xc = ctx.reshape(1, b * c_len, d)
    for l in range(depth):
        need_ctx = l < depth - 1
        lam_init = 0.8 - 0.6 * math.exp(-0.3 * l)
        g_attn = norm_attn_g[l][None, :]
        g_mlp = norm_mlp_g[l][None, :]
        gf = final_norm_g[None, :]
        d_gain = (jnp.tile(diff_norm_g[l], HEADS) * (1.0 - lam_init))[None, :]
        proj = (w_in_p[l], gq[l], gkv[l], uq[l], uk[l], uv[l], e_mat)
        zl = _inproj_call(xl, mod[l], None, g_attn, cos, sin, True, *proj, tm=tm_l)
        zc = _inproj_call(xc, mod[l], b, g_attn, cos_c, sin_c, False, *proj, tm=tm_c)
        zc = zc.reshape(b, c_len, Z_COLS)
        mixes = (
            _na_call(zl, zc, na_bias[l], tq=512),
            _mla_call(zl, zc, tq=512),
            _swa_call(zl, zc, swa_sink[l], tq=512),
            _diff_call(zl, zc, diff_lambda[l], d_gain, lam_init, tq=512),
        )
        wts = (w_out_b[l], w_up_b[l], w_dn_b[l])
        xl = _mlp_call(xl, mixes, mod[l], None, g_mlp, gf, *wts, tm=tm_l, final=not need_ctx)
        if need_ctx:
            mixes_c = _ctx_call(zc, diff_lambda[l], d_gain, swa_sink[l], lam_init)
            mixes_c = [m.reshape(1, b * c_len, 256) for m in mixes_c]
            xc = _mlp_call(xc, mixes_c, mod[l], b, g_mlp, gf, *wts, tm=tm_c, final=False)
    return xl
```

```python
import functools
import math

import jax
import jax.numpy as jnp
import numpy as np
from jax import lax
from jax.experimental import pallas as pl
from jax.experimental.pallas import tpu as pltpu

F32 = jnp.float32
BF16 = jnp.bfloat16

D_MODEL = 1024
GRID_W = 64
HEADS = 4
HEAD_DIM = 64
NA_WIN_R = 8
NA_WIN_C = 16
MLA_Q_LORA = 192
MLA_KV_LORA = 128
MLA_NOPE = 64
MLA_ROPE = 32
SWA_HKV = 2
SWA_GROUP = 2
SWA_WINDOW = 128
DIFF_DH = 32
D_FF = 4 * D_MODEL
ROPE_BASE = 10000.0
EPS = 1e-6
NEG_INF = -1e30
LOG2E = math.log2(math.e)

LANES = 128
VMEM_LIMIT = 56 * 1024 * 1024

ZC_DQ, ZC_DK, ZC_SQ, ZC_NQ, ZC_NK, ZC_NV, ZC_DV, ZC_MV = 0, 256, 512, 768, 1024, 1280, 1536, 1792
ZC_MQ, ZC_MK, ZC_SK, ZC_SV = 2048, 2560, 3072, 3200
Z_COLS = 3328
W_R64, W_R32, W_M, W_N, W_COLS = 0, 384, 896, 1280, 2432
T_R64, T_R32, T_B12, T_MQ, T_COLS = 0, 384, 896, 1152, 1664

NA_KROWS = 10
NA_TQ = 2 * GRID_W
NA_NLOC = NA_KROWS * GRID_W
KEY_CHUNK = 256
UNIT_Q = 256


def _params(*sem):
    return pltpu.CompilerParams(dimension_semantics=sem, vmem_limit_bytes=VMEM_LIMIT)


def _dot(a, b):
    return jnp.dot(a, b, preferred_element_type=F32)


def _dot_nt(a, b):
    return lax.dot_general(a, b, (((1,), (1,)), ((), ())), preferred_element_type=F32)


def _dot_tn(a, b):
    return lax.dot_general(a, b, (((0,), (0,)), ((), ())), preferred_element_type=F32)


def _rms_scale(x, n):
    return lax.rsqrt(jnp.sum(x * x, axis=-1, keepdims=True) * (1.0 / n) + EPS)


def _ada_kernel(c_ref, w_ref, b_ref, o_ref):
    c = c_ref[...]
    act = (c * (1.0 / (1.0 + jnp.exp(-c)))).astype(BF16)
    o_ref[0] = _dot(act, w_ref[0].astype(BF16)) + b_ref[0]


def _ada_call(c_all, w_ada, b_ada):
    depth, d, n = w_ada.shape
    rows = c_all.shape[0]
    tn = 1536
    return pl.pallas_call(
        _ada_kernel,
        grid=(depth, n // tn),
        in_specs=[
            pl.BlockSpec((rows, d), lambda l, j: (0, 0)),
            pl.BlockSpec((1, d, tn), lambda l, j: (l, 0, j)),
            pl.BlockSpec((1, 1, tn), lambda l, j: (l, 0, j)),
        ],
        out_specs=pl.BlockSpec((1, rows, tn), lambda l, j: (l, 0, j)),
        out_shape=jax.ShapeDtypeStruct((depth, rows, n), F32),
        compiler_params=_params("arbitrary", "arbitrary"),
        name="ada",
    )(c_all, w_ada, b_ada.reshape(depth, 1, n))


def _rope_blocks(z, cos_ref, sin_ref, tcol, half):
    tm = z.shape[0]
    lane = lax.broadcasted_iota(jnp.int32, (tm, LANES), 1)
    first = (lane % (2 * half)) < half
    out = []
    for j in range(z.shape[1] // LANES):
        xb = z[:, j * LANES:(j + 1) * LANES]
        rot = jnp.where(first, pltpu.roll(xb, LANES - half, 1), pltpu.roll(xb, half, 1))
        c0 = tcol + j * LANES
        out.append(xb * cos_ref[:, c0:c0 + LANES] + rot * sin_ref[:, c0:c0 + LANES])
    return out


def _inproj_kernel(x_ref, mod_ref, g_ref, cos_ref, sin_ref, w_ref, gq_ref, gkv_ref,
                   wuq_ref, wukvk_ref, wukvv_ref, e_ref, o_ref):
    x = x_ref[0]
    mod = mod_ref[0]
    h = x * _rms_scale(x, D_MODEL) * g_ref[...]
    h = (h * (1.0 + mod[1:2]) + mod[0:1]).astype(BF16)

    def put(col, val):
        o_ref[0, :, col:col + val.shape[1]] = val.astype(BF16)

    z64 = _dot(h, w_ref[:, W_R64:W_R32])
    z32 = _dot(h, w_ref[:, W_R32:W_M])
    r = _rope_blocks(z64, cos_ref, sin_ref, T_R64, HEAD_DIM // 2)
    put(ZC_SQ, r[0]); put(ZC_SQ + LANES, r[1]); put(ZC_SK, r[2])
    zm = _dot(h, w_ref[:, W_M:W_N])
    r = _rope_blocks(z32, cos_ref, sin_ref, T_R32, DIFF_DH // 2)
    for j in range(4):
        put(ZC_DQ + j * LANES, r[j])
    zn = _dot(h, w_ref[:, W_N:W_COLS])

    ckv = zm[:, 0:MLA_KV_LORA]
    b12 = zm[:, MLA_KV_LORA:]
    ckv_n = (ckv * _rms_scale(ckv, MLA_KV_LORA) * gkv_ref[...]).astype(BF16)
    lane = lax.broadcasted_iota(jnp.int32, b12.shape, 1)
    cq = jnp.where(lane < MLA_Q_LORA, b12, 0.0)
    cq_n = (cq * _rms_scale(cq, MLA_Q_LORA) * gq_ref[...]).astype(BF16)
    zq = _dot(cq_n, wuq_ref[...])
    put(ZC_NQ, zn[:, 0:256] * (HEAD_DIM ** -0.5 * LOG2E))
    put(ZC_NK, zn[:, 256:1024])
    put(ZC_SV, zn[:, 1024:1152])
    q = _rope_blocks(zq, cos_ref, sin_ref, T_MQ, MLA_ROPE // 2)
    for j in range(HEADS):
        put(ZC_MQ + j * LANES, q[j])
    kr = _rope_blocks(b12, cos_ref, sin_ref, T_B12, MLA_ROPE // 2)
    kr = jnp.concatenate(kr, axis=1).astype(BF16)
    put(ZC_MK, _dot(ckv_n, wukvk_ref[...]) + _dot(kr, e_ref[...]))
    put(ZC_MV, _dot(ckv_n, wukvv_ref[...]))


def _inproj_call(x3, mod, mod_row, g, cos, sin, table_per_tile, w, gq, gkv, wuq, wukvk, wukvv, e, tm):
    nb, n, d = x3.shape
    nt = n // tm
    const = lambda t, b: (0, 0)
    tab_map = (lambda t, b: (t, 0)) if table_per_tile else const
    mod_map = (lambda t, b: (b, 0, 0)) if mod_row is None else (lambda t, b: (mod_row, 0, 0))
    return pl.pallas_call(
        _inproj_kernel,
        grid=(nt, nb),
        in_specs=[
            pl.BlockSpec((1, tm, d), lambda t, b: (b, t, 0)),
            pl.BlockSpec((1, 6, d), mod_map),
            pl.BlockSpec((1, d), const),
            pl.BlockSpec((tm, T_COLS), tab_map),
            pl.BlockSpec((tm, T_COLS), tab_map),
            pl.BlockSpec(w.shape, const, pipeline_mode=pl.Buffered(1)),
            pl.BlockSpec(gq.shape, const),
            pl.BlockSpec(gkv.shape, const),
            pl.BlockSpec(wuq.shape, const),
            pl.BlockSpec(wukvk.shape, const),
            pl.BlockSpec(wukvv.shape, const),
            pl.BlockSpec(e.shape, const),
        ],
        out_specs=pl.BlockSpec((1, tm, Z_COLS), lambda t, b: (b, t, 0)),
        out_shape=jax.ShapeDtypeStruct((nb, n, Z_COLS), BF16),
        compiler_params=_params("arbitrary", "arbitrary"),
        name="inproj",
    )(x3, mod, g, cos, sin, w, gq, gkv, wuq, wukvk, wukvv, e)


def _lane_mask(q, lo, width):
    lane = lax.broadcasted_iota(jnp.int32, q.shape, 1)
    keep = jnp.where(lane >= lo, lane, LANES) < lo + width
    return jnp.where(keep, q.astype(F32), 0.0).astype(BF16)


def _localmax_softmax(n_units, chunks, score_of, extra_of, pv_of, e_scr):
    outs = [None] * n_units
    mloc, mfin = {}, {}
    for t in range(n_units + 1):
        acc = None
        if t < n_units:
            mloc[t] = []
        for c, (off, rows) in enumerate(chunks):
            if t < n_units:
                s = score_of(t, c)
                mc = jnp.max(s, axis=0, keepdims=True)
                e_scr[t % 2, off:off + rows, :] = jnp.exp2(s - mc).astype(BF16)
                mloc[t].append(mc)
            if t >= 1:
                u = t - 1
                pv = pv_of(u, c, e_scr[u % 2, off:off + rows, :]) * jnp.exp2(mloc[u][c] - mfin[u])
                acc = pv if acc is None else acc + pv
        if t < n_units:
            m = mloc[t][0]
            for mc in mloc[t][1:]:
                m = jnp.maximum(m, mc)
            x = extra_of(t)
            mfin[t] = m if x is None else jnp.maximum(m, x)
        if t >= 1:
            l = acc[HEAD_DIM:HEAD_DIM + 1]
            x = extra_of(t - 1)
            if x is not None:
                l = l + jnp.exp2(x - mfin[t - 1])
            outs[t - 1] = acc[0:HEAD_DIM] * (1.0 / l)
    return outs


def _localmax_batched(n_units, chunks, score_of, extra_of, pv_of, e_scr):
    mloc, lloc, mfin = [], [], []
    for u in range(n_units):
        ms, ls = [], []
        for c, (off, rows) in enumerate(chunks):
            s = score_of(u, c)
            mc = jnp.max(s, axis=0, keepdims=True)
            e = jnp.exp2(s - mc)
            ls.append(jnp.sum(e, axis=0, keepdims=True))
            e_scr[u, off:off + rows, :] = e.astype(BF16)
            ms.append(mc)
        m = ms[0]
        for mc in ms[1:]:
            m = jnp.maximum(m, mc)
        x = extra_of(u)
        mloc.append(ms); lloc.append(ls); mfin.append(m if x is None else jnp.maximum(m, x))
    outs = []
    for u in range(n_units):
        acc = l = None
        for c, (off, rows) in enumerate(chunks):
            alpha = jnp.exp2(mloc[u][c] - mfin[u])
            pv = pv_of(u, c, e_scr[u, off:off + rows, :]) * alpha
            acc = pv if acc is None else acc + pv
            l = lloc[u][c] * alpha if l is None else l + lloc[u][c] * alpha
        x = extra_of(u)
        if x is not None:
            l = l + jnp.exp2(x - mfin[u])
        outs.append(acc * (1.0 / l))
    return outs


def _head_rows(o, h):
    return o[(h % 2) * HEAD_DIM:(h % 2 + 1) * HEAD_DIM]


def _diff_lambda(lp, lam_init):
    a = jnp.sum(lp[0:1] * lp[1:2], axis=-1, keepdims=True)
    b = jnp.sum(lp[2:3] * lp[3:4], axis=-1, keepdims=True)
    return jnp.exp(a) - jnp.exp(b) + lam_init


def _diff_combine(outs, lam):
    res = []
    for h in range(HEADS):
        o = outs[2 * h] - lam * outs[2 * h + 1]
        ms = jnp.sum(o * o, axis=0, keepdims=True) * (1.0 / HEAD_DIM)
        res.append(o * lax.rsqrt(ms + EPS))
    return jnp.concatenate(res, axis=0)


def _key_chunks(total, size):
    return [(o, min(size, total - o)) for o in range(0, total, size)]


def _exp_scratch(nk, tq, slots=2):
    return [pltpu.VMEM((slots, nk, tq), BF16)]


VT_ROWS = HEAD_DIM + 16


def _transpose_values(vt_scr, srcs):
    for ref, off, n in srcs:
        for r in range(0, n, KEY_CHUNK):
            rows = min(KEY_CHUNK, n - r)
            vt = ref[0, r:r + rows, :].astype(F32).T
            for h in range(HEADS):
                vt_scr[h * VT_ROWS:h * VT_ROWS + HEAD_DIM, off + r:off + r + rows] = (
                    vt[h * HEAD_DIM:(h + 1) * HEAD_DIM].astype(BF16))
                vt_scr[h * VT_ROWS + HEAD_DIM:(h + 1) * VT_ROWS, off + r:off + r + rows] = (
                    jnp.ones((VT_ROWS - HEAD_DIM, rows), BF16))


def _mla_kernel(q_ref, k_ref, v_ref, kc_ref, vc_ref, o_ref, e_scr, vt_scr):
    s_len, c_len = k_ref.shape[1], kc_ref.shape[1]

    @pl.when(pl.program_id(1) == 0)
    def _():
        _transpose_values(vt_scr, [(v_ref, 0, s_len), (vc_ref, s_len, c_len)])

    chunks = _key_chunks(s_len, KEY_CHUNK) + [(s_len + o, r) for o, r in _key_chunks(c_len, KEY_CHUNK)]
    units = [(qt, h) for qt in range(q_ref.shape[1] // UNIT_Q) for h in range(HEADS)]
    qs = [q_ref[0, qt * UNIT_Q:(qt + 1) * UNIT_Q, h * LANES:(h + 1) * LANES] for qt, h in units]

    def score_of(u, c):
        off, rows = chunks[c]
        h = units[u][1]
        sl = slice(h * LANES, (h + 1) * LANES)
        if off < s_len:
            return _dot_nt(k_ref[0, off:off + rows, sl], qs[u])
        return _dot_nt(kc_ref[0, off - s_len:off - s_len + rows, sl], qs[u])

    def pv_of(u, c, e):
        h = units[u][1]
        off, rows = chunks[c]
        return _dot(vt_scr[h * VT_ROWS:(h + 1) * VT_ROWS, off:off + rows], e)

    outs = _localmax_softmax(len(units), chunks, score_of, lambda u: None, pv_of, e_scr)
    for qt in range(len(units) // HEADS):
        o = jnp.concatenate(outs[qt * HEADS:(qt + 1) * HEADS], axis=0)
        o_ref[0, qt * UNIT_Q:(qt + 1) * UNIT_Q, :] = o.T.astype(BF16)


def _mla_call(zl, zc, tq):
    b, s, _ = zl.shape
    c = zc.shape[1]
    return pl.pallas_call(
        _mla_kernel,
        grid=(b, s // tq),
        in_specs=[
            pl.BlockSpec((1, tq, 512), lambda i, j: (i, j, ZC_MQ // 512)),
            pl.BlockSpec((1, s, 512), lambda i, j: (i, 0, ZC_MK // 512)),
            pl.BlockSpec((1, s, 256), lambda i, j: (i, 0, ZC_MV // 256)),
            pl.BlockSpec((1, c, 512), lambda i, j: (i, 0, ZC_MK // 512)),
            pl.BlockSpec((1, c, 256), lambda i, j: (i, 0, ZC_MV // 256)),
        ],
        out_specs=pl.BlockSpec((1, tq, 256), lambda i, j: (i, j, 0)),
        out_shape=jax.ShapeDtypeStruct((b, s, 256), BF16),
        scratch_shapes=_exp_scratch(s + c, UNIT_Q) + [pltpu.VMEM((HEADS * VT_ROWS, s + c), BF16)],
        compiler_params=_params("arbitrary", "arbitrary"),
        name="mla",
    )(zl, zl, zl, zc, zc)


def _diff_kernel(lam_init, q_ref, k_ref, v_ref, kc_ref, vc_ref, lp_ref, gain_ref, o_ref,
                 e_scr, vt_scr):
    s_len, c_len = k_ref.shape[1], kc_ref.shape[1]

    @pl.when(pl.program_id(1) == 0)
    def _():
        _transpose_values(vt_scr, [(v_ref, 0, s_len), (vc_ref, s_len, c_len)])

    chunks = _key_chunks(s_len, KEY_CHUNK) + [(s_len + o, r) for o, r in _key_chunks(c_len, KEY_CHUNK)]
    n_sub = 2 * HEADS
    units = [(qt, w) for qt in range(q_ref.shape[1] // UNIT_Q) for w in range(n_sub)]
    qs = [_lane_mask(q_ref[0, qt * UNIT_Q:(qt + 1) * UNIT_Q, (w // 4) * LANES:(w // 4 + 1) * LANES],
                     (w % 4) * DIFF_DH, DIFF_DH) for qt, w in units]

    def score_of(u, c):
        off, rows = chunks[c]
        w = units[u][1]
        sl = slice((w // 4) * LANES, (w // 4 + 1) * LANES)
        if off < s_len:
            return _dot_nt(k_ref[0, off:off + rows, sl], qs[u])
        return _dot_nt(kc_ref[0, off - s_len:off - s_len + rows, sl], qs[u])

    def pv_of(u, c, e):
        h = units[u][1] // 2
        off, rows = chunks[c]
        return _dot(vt_scr[h * VT_ROWS:(h + 1) * VT_ROWS, off:off + rows], e)

    outs = _localmax_softmax(len(units), chunks, score_of, lambda u: None, pv_of, e_scr)
    lam = _diff_lambda(lp_ref[...], lam_init)
    for qt in range(len(units) // n_sub):
        o = _diff_combine(outs[qt * n_sub:(qt + 1) * n_sub], lam)
        o_ref[0, qt * UNIT_Q:(qt + 1) * UNIT_Q, :] = (o.T * gain_ref[...]).astype(BF16)


def _diff_call(zl, zc, lp, gain, lam_init, tq):
    b, s, _ = zl.shape
    c = zc.shape[1]
    return pl.pallas_call(
        functools.partial(_diff_kernel, lam_init),
        grid=(b, s // tq),
        in_specs=[
            pl.BlockSpec((1, tq, 256), lambda i, j: (i, j, ZC_DQ // 256)),
            pl.BlockSpec((1, s, 256), lambda i, j: (i, 0, ZC_DK // 256)),
            pl.BlockSpec((1, s, 256), lambda i, j: (i, 0, ZC_DV // 256)),
            pl.BlockSpec((1, c, 256), lambda i, j: (i, 0, ZC_DK // 256)),
            pl.BlockSpec((1, c, 256), lambda i, j: (i, 0, ZC_DV // 256)),
            pl.BlockSpec(lp.shape, lambda i, j: (0, 0)),
            pl.BlockSpec(gain.shape, lambda i, j: (0, 0)),
        ],
        out_specs=pl.BlockSpec((1, tq, 256), lambda i, j: (i, j, 0)),
        out_shape=jax.ShapeDtypeStruct((b, s, 256), BF16),
        scratch_shapes=_exp_scratch(s + c, UNIT_Q) + [pltpu.VMEM((HEADS * VT_ROWS, s + c), BF16)],
        compiler_params=_params("arbitrary", "arbitrary"),
        name="diff",
    )(zl, zl, zl, zc, zc, lp, gain)


def _swa_kernel(q_ref, k_ref, v_ref, kc_ref, vc_ref, sink_ref, o_ref, e_scr, mask_scr):
    s_len, c_len = k_ref.shape[1], kc_ref.shape[1]
    n_qt = q_ref.shape[1] // UNIT_Q
    win = UNIT_Q + 2 * SWA_WINDOW
    chunks = _key_chunks(win, KEY_CHUNK) + [(win + o, r) for o, r in _key_chunks(c_len, KEY_CHUNK)]
    starts = []
    for qt in range(n_qt):
        q0 = (pl.program_id(1) * n_qt + qt) * UNIT_Q
        start = pl.multiple_of(jnp.clip(q0 - SWA_WINDOW, 0, s_len - win), SWA_WINDOW)
        kpos = start + lax.broadcasted_iota(jnp.int32, (win, UNIT_Q), 0)
        qpos = q0 + lax.broadcasted_iota(jnp.int32, (win, UNIT_Q), 1)
        mask_scr[qt] = jnp.where(jnp.abs(kpos - qpos) <= SWA_WINDOW, 0.0, NEG_INF)
        starts.append(start)
    units = [(qt, w) for qt in range(n_qt) for w in range(HEADS)]
    qs = [_lane_mask(q_ref[0, qt * UNIT_Q:(qt + 1) * UNIT_Q, (w // SWA_HKV) * LANES:(w // SWA_HKV + 1) * LANES],
                     (w % SWA_HKV) * HEAD_DIM, HEAD_DIM) for qt, w in units]
    head_of = lambda w: (w % SWA_HKV) * SWA_GROUP + w // SWA_HKV

    def score_of(u, c):
        off, rows = chunks[c]
        qt = units[u][0]
        if off < win:
            k = k_ref[0, pl.ds(starts[qt] + off, rows), :]
            return _dot_nt(k, qs[u]) + mask_scr[qt, off:off + rows, :]
        return _dot_nt(kc_ref[0, off - win:off - win + rows, :], qs[u])

    def extra_of(u):
        return jnp.full((1, 1), sink_ref[head_of(units[u][1])] * LOG2E, F32)

    def pv_of(u, c, e):
        qt, w = units[u]
        off, rows = chunks[c]
        if off < win:
            o = _dot_tn(v_ref[0, pl.ds(starts[qt] + off, rows), :], e)
        else:
            o = _dot_tn(vc_ref[0, off - win:off - win + rows, :], e)
        return o[(w % SWA_HKV) * HEAD_DIM:(w % SWA_HKV + 1) * HEAD_DIM]

    outs = _localmax_batched(len(units), chunks, score_of, extra_of, pv_of, e_scr)
    for qt in range(n_qt):
        by_head = {head_of(w): outs[qt * HEADS + w] for w in range(HEADS)}
        o = jnp.concatenate([by_head[h] for h in range(HEADS)], axis=0)
        o_ref[0, qt * UNIT_Q:(qt + 1) * UNIT_Q, :] = o.T.astype(BF16)


def _swa_call(zl, zc, sink, tq):
    b, s, _ = zl.shape
    c = zc.shape[1]
    win = UNIT_Q + 2 * SWA_WINDOW
    return pl.pallas_call(
        _swa_kernel,
        grid=(b, s // tq),
        in_specs=[
            pl.BlockSpec((1, tq, 256), lambda i, j: (i, j, ZC_SQ // 256)),
            pl.BlockSpec((1, s, LANES), lambda i, j: (i, 0, ZC_SK // LANES)),
            pl.BlockSpec((1, s, LANES), lambda i, j: (i, 0, ZC_SV // LANES)),
            pl.BlockSpec((1, c, LANES), lambda i, j: (i, 0, ZC_SK // LANES)),
            pl.BlockSpec((1, c, LANES), lambda i, j: (i, 0, ZC_SV // LANES)),
            pl.BlockSpec(memory_space=pltpu.SMEM),
        ],
        out_specs=pl.BlockSpec((1, tq, 256), lambda i, j: (i, j, 0)),
        out_shape=jax.ShapeDtypeStruct((b, s, 256), BF16),
        scratch_shapes=(_exp_scratch(win + c, UNIT_Q, HEADS * tq // UNIT_Q)
                        + [pltpu.VMEM((tq // UNIT_Q, win, UNIT_Q), F32)]),
        compiler_params=_params("arbitrary", "arbitrary"),
        name="swa",
    )(zl, zl, zl, zc, zc, sink)


def _na_kernel(rows_n, q_ref, k_ref, v_ref, kc_ref, vc_ref, bias_ref, o_ref, e_scr):
    c_len = kc_ref.shape[1]
    n_rb = q_ref.shape[1] // NA_TQ
    nrb = rows_n // 2
    chunks = _key_chunks(NA_NLOC, KEY_CHUNK) + [(NA_NLOC + o, r) for o, r in _key_chunks(c_len, KEY_CHUNK)]
    starts, tids = [], []
    for i in range(n_rb):
        rb = pl.program_id(1) * n_rb + i
        starts.append(pl.multiple_of(
            jnp.clip(2 * rb - NA_WIN_R // 2, 0, rows_n - NA_KROWS) * GRID_W, 2 * GRID_W))
        tids.append(jnp.minimum(rb, 2) + jnp.maximum(rb - (nrb - 3), 0))
    blk = lambda h: slice((h // 2) * LANES, (h // 2 + 1) * LANES)
    units = [(i, h) for i in range(n_rb) for h in range(HEADS)]
    qs = [_lane_mask(q_ref[0, i * NA_TQ:(i + 1) * NA_TQ, blk(h)], (h % 2) * HEAD_DIM, HEAD_DIM)
          for i, h in units]

    def score_of(u, c):
        off, rows = chunks[c]
        i, h = units[u]
        if off < NA_NLOC:
            k = k_ref[0, pl.ds(starts[i] + off, rows), blk(h)]
            return _dot_nt(k, qs[u]) + bias_ref[h, tids[i], off:off + rows, :]
        return _dot_nt(kc_ref[0, off - NA_NLOC:off - NA_NLOC + rows, blk(h)], qs[u])

    def pv_of(u, c, e):
        i, h = units[u]
        off, rows = chunks[c]
        if off < NA_NLOC:
            return _head_rows(_dot_tn(v_ref[0, pl.ds(starts[i] + off, rows), blk(h)], e), h)
        return _head_rows(_dot_tn(vc_ref[0, off - NA_NLOC:off - NA_NLOC + rows, blk(h)], e), h)

    outs = _localmax_batched(len(units), chunks, score_of, lambda u: None, pv_of, e_scr)
    for i in range(n_rb):
        o = jnp.concatenate(outs[i * HEADS:(i + 1) * HEADS], axis=0)
        o_ref[0, i * NA_TQ:(i + 1) * NA_TQ, :] = o.T.astype(BF16)


def _na_call(zl, zc, bias, tq):
    b, s, _ = zl.shape
    c = zc.shape[1]
    rows_n = s // GRID_W
    return pl.pallas_call(
        functools.partial(_na_kernel, rows_n),
        grid=(b, s // tq),
        in_specs=[
            pl.BlockSpec((1, tq, 256), lambda i, j: (i, j, ZC_NQ // 256)),
            pl.BlockSpec((1, s, 256), lambda i, j: (i, 0, ZC_NK // 256)),
            pl.BlockSpec((1, s, 256), lambda i, j: (i, 0, ZC_NV // 256)),
            pl.BlockSpec((1, c, 256), lambda i, j: (i, 0, ZC_NK // 256)),
            pl.BlockSpec((1, c, 256), lambda i, j: (i, 0, ZC_NV // 256)),
            pl.BlockSpec(bias.shape, lambda i, j: (0, 0, 0, 0), pipeline_mode=pl.Buffered(1)),
        ],
        out_specs=pl.BlockSpec((1, tq, 256), lambda i, j: (i, j, 0)),
        out_shape=jax.ShapeDtypeStruct((b, s, 256), BF16),
        scratch_shapes=_exp_scratch(NA_NLOC + c, NA_TQ, HEADS * tq // NA_TQ),
        compiler_params=_params("arbitrary", "arbitrary"),
        name="na",
    )(zl, zl, zl, zc, zc, bias)


def _ctx_kernel(lam_init, z_ref, lp_ref, gain_ref, sink_ref, oa_ref, ob_ref, oc_ref, od_ref,
                e_scr):
    c_len = z_ref.shape[1]

    def blk(col, j=0):
        return z_ref[0, :, col + j * LANES:col + (j + 1) * LANES]

    units = []
    for h in range(HEADS):
        units.append((_lane_mask(blk(ZC_NQ, h // 2), (h % 2) * HEAD_DIM, HEAD_DIM),
                      ZC_NK + (h // 2) * LANES, ZC_NV + (h // 2) * LANES, (h % 2) * HEAD_DIM, None))
    for h in range(HEADS):
        units.append((blk(ZC_MQ, h), ZC_MK + h * LANES, ZC_MV + (h // 2) * LANES,
                      (h % 2) * HEAD_DIM, None))
    for h in range(HEADS):
        hk, g = h // SWA_GROUP, h % SWA_GROUP
        units.append((_lane_mask(blk(ZC_SQ, g), hk * HEAD_DIM, HEAD_DIM), ZC_SK, ZC_SV,
                      hk * HEAD_DIM, h))
    for w in range(2 * HEADS):
        units.append((_lane_mask(blk(ZC_DQ, w // 4), (w % 4) * DIFF_DH, DIFF_DH),
                      ZC_DK + (w // 4) * LANES, ZC_DV + (w // 4) * LANES, ((w // 2) % 2) * HEAD_DIM, None))

    def score_of(u, c):
        q_m, kcol = units[u][0], units[u][1]
        return _dot_nt(z_ref[0, :, kcol:kcol + LANES], q_m)

    def extra_of(u):
        h = units[u][4]
        return None if h is None else jnp.full((1, 1), sink_ref[h] * LOG2E, F32)

    def pv_of(u, c, e):
        vcol, r0 = units[u][2], units[u][3]
        return _dot_tn(z_ref[0, :, vcol:vcol + LANES], e)[r0:r0 + HEAD_DIM]

    outs = _localmax_batched(len(units), [(0, c_len)], score_of, extra_of, pv_of, e_scr)
    for i, ref in enumerate((oa_ref, ob_ref, oc_ref)):
        ref[0] = jnp.concatenate(outs[i * HEADS:(i + 1) * HEADS], axis=0).T.astype(BF16)
    o = _diff_combine(outs[3 * HEADS:], _diff_lambda(lp_ref[...], lam_init))
    od_ref[0] = (o.T * gain_ref[...]).astype(BF16)


def _ctx_call(zc, lp, gain, sink, lam_init):
    b, c, _ = zc.shape
    out = jax.ShapeDtypeStruct((b, c, 256), BF16)
    ospec = pl.BlockSpec((1, c, 256), lambda i: (i, 0, 0))
    return pl.pallas_call(
        functools.partial(_ctx_kernel, lam_init),
        grid=(b,),
        in_specs=[
            pl.BlockSpec((1, c, Z_COLS), lambda i: (i, 0, 0)),
            pl.BlockSpec(lp.shape, lambda i: (0, 0)),
            pl.BlockSpec(gain.shape, lambda i: (0, 0)),
            pl.BlockSpec(memory_space=pltpu.SMEM),
        ],
        out_specs=[ospec] * 4,
        out_shape=[out] * 4,
        scratch_shapes=_exp_scratch(c, c, 5 * HEADS),
        compiler_params=_params("arbitrary"),
        name="ctx_attn",
    )(zc, lp, gain, sink)


def _mlp_kernel(final, x_ref, ma_ref, mb_ref, mc_ref, md_ref, mod_ref, g_ref, gf_ref,
                wout_ref, wup_ref, wdn_ref, o_ref):
    x = x_ref[0]
    mod = mod_ref[0]
    attn = None
    for i, m_ref in enumerate((ma_ref, mb_ref, mc_ref, md_ref)):
        part = _dot(m_ref[0], wout_ref[i * 256:(i + 1) * 256, :])
        attn = part if attn is None else attn + part
    x = x + mod[2:3] * attn
    h = x * _rms_scale(x, D_MODEL) * g_ref[...]
    h = (h * (1.0 + mod[4:5]) + mod[3:4]).astype(BF16)
    acc = None
    ck = 1024
    for c in range(D_FF // ck):
        u = jnp.maximum(_dot(h, wup_ref[:, c * ck:(c + 1) * ck]), 0.0)
        part = _dot((u * u).astype(BF16), wdn_ref[c * ck:(c + 1) * ck, :])
        acc = part if acc is None else acc + part
    x = x + mod[5:6] * acc
    if final:
        x = x * _rms_scale(x, D_MODEL) * gf_ref[...]
    o_ref[0] = x


def _mlp_call(x3, mixes, mod, mod_row, g, gf, wout, wup, wdn, tm, final):
    nb, n, d = x3.shape
    const = lambda t, b: (0, 0)
    tok = lambda t, b: (b, t, 0)
    mod_map = (lambda t, b: (b, 0, 0)) if mod_row is None else (lambda t, b: (mod_row, 0, 0))
    single = pl.Buffered(1)
    return pl.pallas_call(
        functools.partial(_mlp_kernel, final),
        grid=(n // tm, nb),
        in_specs=[pl.BlockSpec((1, tm, d), tok)]
        + [pl.BlockSpec((1, tm, 256), tok)] * 4
        + [
            pl.BlockSpec((1, 6, d), mod_map),
            pl.BlockSpec((1, d), const),
            pl.BlockSpec((1, d), const),
            pl.BlockSpec(wout.shape, const, pipeline_mode=single),
            pl.BlockSpec(wup.shape, const, pipeline_mode=single),
            pl.BlockSpec(wdn.shape, const, pipeline_mode=single),
        ],
        out_specs=pl.BlockSpec((1, tm, d), tok),
        out_shape=jax.ShapeDtypeStruct((nb, n, d), F32),
        compiler_params=_params("arbitrary", "arbitrary"),
        name="mlp",
    )(x3, *mixes, mod, g, gf, wout, wup, wdn)


def _rope_tables(s_len):
    t = np.arange(s_len)
    row, col = t // GRID_W, t % GRID_W

    def axial(dim):
        n_freq = dim // 4
        freqs = jnp.asarray(ROPE_BASE, F32) ** (-jnp.arange(n_freq, dtype=F32) / n_freq)
        ang = jnp.concatenate([jnp.asarray(row, F32)[:, None] * freqs,
                               jnp.asarray(col, F32)[:, None] * freqs], axis=-1)
        return jnp.cos(ang), jnp.sin(ang)

    def group(cs, n_groups, scale):
        cos, sin = cs
        c = jnp.concatenate([cos, cos], axis=-1) * scale
        s = jnp.concatenate([-sin, sin], axis=-1) * scale
        return jnp.tile(c, (1, n_groups)), jnp.tile(s, (1, n_groups))

    r64, r32 = axial(HEAD_DIM), axial(MLA_ROPE)
    ones = lambda n, v=1.0: jnp.full((s_len, n), v, F32)
    zeros = lambda n: jnp.zeros((s_len, n), F32)
    mla_scale = (MLA_NOPE + MLA_ROPE) ** -0.5 * LOG2E
    mq_c, mq_s = group(r32, 1, mla_scale)
    parts = [
        group(r64, 4, HEAD_DIM ** -0.5 * LOG2E), group(r64, 2, 1.0),
        group(r32, 8, DIFF_DH ** -0.5 * LOG2E), group(r32, 8, 1.0),
        (ones(MLA_Q_LORA), zeros(MLA_Q_LORA)), group(r32, 1, 1.0), (ones(32), zeros(32)),
    ] + [(jnp.concatenate([ones(MLA_NOPE, mla_scale), mq_c, ones(32, mla_scale)], axis=-1),
          jnp.concatenate([zeros(MLA_NOPE), mq_s, zeros(32)], axis=-1))] * HEADS
    cos = jnp.concatenate([p[0] for p in parts], axis=-1)
    sin = jnp.concatenate([p[1] for p in parts], axis=-1)
    return cos, sin, cos[0:1]


def _permute_w_in(w_in):
    offs = np.cumsum([0, 256, 256, 256, MLA_Q_LORA, MLA_KV_LORA, MLA_ROPE, 256, 128, 128, 256, 256, 256])
    sec = lambda i: w_in[..., offs[i]:offs[i + 1]]
    sq = sec(6)
    sq = sq.reshape(sq.shape[:-1] + (SWA_HKV, SWA_GROUP, HEAD_DIM))
    sq = jnp.swapaxes(sq, -3, -2).reshape(w_in.shape[:-1] + (256,))
    pad = jnp.zeros(w_in.shape[:-1] + (32,), w_in.dtype)
    cols = [sq, sec(7), sec(9), sec(10), sec(4), sec(3), sec(5), pad,
            sec(0), sec(1), sec(2), sec(11), sec(8)]
    return jnp.concatenate(cols, axis=-1).astype(BF16)


def _mla_weights(w_uq, w_ukv):
    depth = w_uq.shape[0]
    uq = w_uq.reshape(depth, MLA_Q_LORA, HEADS, MLA_NOPE + MLA_ROPE)
    uq = jnp.pad(uq, ((0, 0), (0, 256 - MLA_Q_LORA), (0, 0), (0, LANES - MLA_NOPE - MLA_ROPE)))
    uq = uq.reshape(depth, 256, HEADS * LANES).astype(BF16)
    ukv = w_ukv.reshape(depth, MLA_KV_LORA, HEADS, 2, HEAD_DIM)
    uk = jnp.pad(ukv[:, :, :, 0], ((0, 0), (0, 0), (0, 0), (0, LANES - MLA_NOPE)))
    uk = uk.reshape(depth, MLA_KV_LORA, HEADS * LANES).astype(BF16)
    uv = ukv[:, :, :, 1].reshape(depth, MLA_KV_LORA, HEADS * HEAD_DIM).astype(BF16)
    e = np.zeros((256, HEADS * LANES), np.float32)
    for h in range(HEADS):
        for j in range(MLA_ROPE):
            e[MLA_Q_LORA + j, h * LANES + MLA_NOPE + j] = 1.0
    return uq, uk, uv, jnp.asarray(e, BF16)


def _na_bias_tables(rpb, rows_n):
    kr_n = min(NA_WIN_R, rows_n)
    col = np.arange(GRID_W)
    c0 = np.clip(col - NA_WIN_C // 2, 0, GRID_W - NA_WIN_C)
    col_ok_t = ((col[None, :] >= c0[:, None]) & (col[None, :] < c0[:, None] + NA_WIN_C)).T
    dc_idx_t = np.clip(col[:, None] - col[None, :], 1 - NA_WIN_C, NA_WIN_C - 1) + (NA_WIN_C - 1)
    toe = jnp.zeros(rpb.shape[:3] + (GRID_W, GRID_W), F32)
    for d in range(2 * NA_WIN_C - 1):
        toe = jnp.where(dc_idx_t == d, rpb[..., d, None, None].astype(F32), toe)
    toe = jnp.where(col_ok_t, toe * LOG2E, NEG_INF)
    neg = jnp.full(toe.shape[:2] + (GRID_W, GRID_W), NEG_INF, F32)
    nrb = rows_n // 2
    pairs = {}

    def pair(drs):
        if drs not in pairs:
            pairs[drs] = jnp.concatenate([neg if d is None else toe[:, :, d] for d in drs], axis=-1)
        return pairs[drs]

    blocks = []
    for rb in (0, 1, 2, nrb - 2, nrb - 1):
        start = int(np.clip(2 * rb - NA_WIN_R // 2, 0, rows_n - NA_KROWS))
        for j in range(NA_KROWS):
            drs = []
            for i in range(2):
                kr, qr = start + j, 2 * rb + i
                r0 = int(np.clip(qr - kr_n // 2, 0, rows_n - kr_n))
                drs.append(kr - qr + NA_WIN_R - 1 if r0 <= kr < r0 + kr_n else None)
            blocks.append(pair(tuple(drs)))
    tab = jnp.stack(blocks, axis=2)
    return tab.reshape(tab.shape[:2] + (5, NA_NLOC, NA_TQ))


def kernel(x, c, ctx, c_ctx, w_ada, b_ada, norm_attn_g, w_in, na_rpb, mla_q_norm_g, mla_w_uq,
           mla_kv_norm_g, mla_w_ukv, swa_sink, diff_lambda, diff_norm_g, w_out, norm_mlp_g,
           w_up, w_down, final_norm_g):
    b, s, d = x.shape
    c_len = ctx.shape[1]
    depth = w_in.shape[0]
    rows_n = s // GRID_W

    w_in_p = _permute_w_in(w_in)
    uq, uk, uv, e_mat = _mla_weights(mla_w_uq, mla_w_ukv)
    gq = jnp.pad(mla_q_norm_g, ((0, 0), (0, 256 - MLA_Q_LORA)))[:, None, :]
    gkv = mla_kv_norm_g[:, None, :]
    w_out_b, w_up_b, w_dn_b = w_out.astype(BF16), w_up.astype(BF16), w_down.astype(BF16)
    na_bias = _na_bias_tables(na_rpb, rows_n)
    cos, sin, cos_ctx = _rope_tables(s)
    tm_l, tm_c = 512, min(512, b * c_len)
    cos_c = jnp.broadcast_to(cos_ctx, (tm_c, T_COLS))
    sin_c = jnp.zeros((tm_c, T_COLS), F32)

    c_rows = -(-(b + 1) // 8) * 8
    c_all = jnp.zeros((c_rows, d), F32).at[:b].set(c).at[b].set(c_ctx)
    mod = _ada_call(c_all, w_ada, b_ada).reshape(depth, c_rows, 6, d)

    xl = x
    xc = ctx.reshape(1, b * c_len, d)
    for l in range(depth):
        need_ctx = l < depth - 1
        lam_init = 0.8 - 0.6 * math.exp(-0.3 * l)
        g_attn = norm_attn_g[l][None, :]
        g_mlp = norm_mlp_g[l][None, :]
        gf = final_norm_g[None, :]
        d_gain = (jnp.tile(diff_norm_g[l], HEADS) * (1.0 - lam_init))[None, :]
        proj = (w_in_p[l], gq[l], gkv[l], uq[l], uk[l], uv[l], e_mat)
        zl = _inproj_call(xl, mod[l], None, g_attn, cos, sin, True, *proj, tm=tm_l)
        zc = _inproj_call(xc, mod[l], b, g_attn, cos_c, sin_c, False, *proj, tm=tm_c)
        zc = zc.reshape(b, c_len, Z_COLS)
        mixes = (
            _na_call(zl, zc, na_bias[l], tq=1024),
            _mla_call(zl, zc, tq=1024),
            _swa_call(zl, zc, swa_sink[l], tq=1024),
            _diff_call(zl, zc, diff_lambda[l], d_gain, lam_init, tq=512),
        )
        wts = (w_out_b[l], w_up_b[l], w_dn_b[l])
        xl = _mlp_call(xl, mixes, mod[l], None, g_mlp, gf, *wts, tm=tm_l, final=not need_ctx)
        if need_ctx:
            mixes_c = _ctx_call(zc, diff_lambda[l], d_gain, swa_sink[l], lam_init)
            mixes_c = [m.reshape(1, b * c_len, 256) for m in mixes_c]
            xc = _mlp_call(xc, mixes_c, mod[l], b, g_mlp, gf, *wts, tm=tm_c, final=False)
    return xl
```

```python
import functools
import math

import jax
import jax.numpy as jnp
import numpy as np
from jax import lax
from jax.experimental import pallas as pl
from jax.experimental.pallas import tpu as pltpu

F32 = jnp.float32
BF16 = jnp.bfloat16

D_MODEL = 1024
GRID_W = 64
HEADS = 4
HEAD_DIM = 64
NA_WIN_R = 8
NA_WIN_C = 16
MLA_Q_LORA = 192
MLA_KV_LORA = 128
MLA_NOPE = 64
MLA_ROPE = 32
SWA_HKV = 2
SWA_GROUP = 2
SWA_WINDOW = 128
DIFF_DH = 32
D_FF = 4 * D_MODEL
ROPE_BASE = 10000.0
EPS = 1e-6
NEG_INF = -1e30
LOG2E = math.log2(math.e)

LANES = 128
VMEM_LIMIT = 56 * 1024 * 1024

ZC_DQ, ZC_DK, ZC_SQ, ZC_NQ, ZC_NK, ZC_NV, ZC_DV, ZC_MV = 0, 256, 512, 768, 1024, 1280, 1536, 1792
ZC_MQ, ZC_MK, ZC_SK, ZC_SV = 2048, 2560, 3072, 3200
Z_COLS = 3328
W_R64, W_R32, W_M, W_N, W_COLS = 0, 384, 896, 1280, 2432
T_R64, T_R32, T_B12, T_MQ, T_COLS = 0, 384, 896, 1152, 1664

NA_KROWS = 10
NA_TQ = 2 * GRID_W
NA_NLOC = NA_KROWS * GRID_W
KEY_CHUNK = 256
UNIT_Q = 256


def _params(*sem):
    return pltpu.CompilerParams(dimension_semantics=sem, vmem_limit_bytes=VMEM_LIMIT)


def _dot(a, b):
    return jnp.dot(a, b, preferred_element_type=F32)


def _dot_nt(a, b):
    return lax.dot_general(a, b, (((1,), (1,)), ((), ())), preferred_element_type=F32)


def _dot_tn(a, b):
    return lax.dot_general(a, b, (((0,), (0,)), ((), ())), preferred_element_type=F32)


def _rms_scale(x, n):
    return lax.rsqrt(jnp.sum(x * x, axis=-1, keepdims=True) * (1.0 / n) + EPS)


def _ada_kernel(c_ref, w_ref, b_ref, o_ref):
    c = c_ref[...]
    act = (c * (1.0 / (1.0 + jnp.exp(-c)))).astype(BF16)
    o_ref[0] = _dot(act, w_ref[0].astype(BF16)) + b_ref[0]


def _ada_call(c_all, w_ada, b_ada):
    depth, d, n = w_ada.shape
    rows = c_all.shape[0]
    tn = 1536
    return pl.pallas_call(
        _ada_kernel,
        grid=(depth, n // tn),
        in_specs=[
            pl.BlockSpec((rows, d), lambda l, j: (0, 0)),
            pl.BlockSpec((1, d, tn), lambda l, j: (l, 0, j)),
            pl.BlockSpec((1, 1, tn), lambda l, j: (l, 0, j)),
        ],
        out_specs=pl.BlockSpec((1, rows, tn), lambda l, j: (l, 0, j)),
        out_shape=jax.ShapeDtypeStruct((depth, rows, n), F32),
        compiler_params=_params("arbitrary", "arbitrary"),
        name="ada",
    )(c_all, w_ada, b_ada.reshape(depth, 1, n))


def _rope_blocks(z, cos_ref, sin_ref, tcol, half):
    tm = z.shape[0]
    lane = lax.broadcasted_iota(jnp.int32, (tm, LANES), 1)
    first = (lane % (2 * half)) < half
    out = []
    for j in range(z.shape[1] // LANES):
        xb = z[:, j * LANES:(j + 1) * LANES]
        rot = jnp.where(first, pltpu.roll(xb, LANES - half, 1), pltpu.roll(xb, half, 1))
        c0 = tcol + j * LANES
        out.append(xb * cos_ref[:, c0:c0 + LANES] + rot * sin_ref[:, c0:c0 + LANES])
    return out


def _inproj_kernel(x_ref, mod_ref, g_ref, cos_ref, sin_ref, w_ref, gq_ref, gkv_ref,
                   wuq_ref, wukvk_ref, wukvv_ref, o_ref):
    x = x_ref[0]
    mod = mod_ref[0]
    h = x * _rms_scale(x, D_MODEL) * g_ref[...]
    h = (h * (1.0 + mod[1:2]) + mod[0:1]).astype(BF16)

    def put(col, val):
        o_ref[0, :, col:col + val.shape[1]] = val.astype(BF16)

    z64 = _dot(h, w_ref[:, W_R64:W_R32])
    z32 = _dot(h, w_ref[:, W_R32:W_M])
    r = _rope_blocks(z64, cos_ref, sin_ref, T_R64, HEAD_DIM // 2)
    put(ZC_SQ, r[0]); put(ZC_SQ + LANES, r[1]); put(ZC_SK, r[2])
    zm = _dot(h, w_ref[:, W_M:W_N])
    r = _rope_blocks(z32, cos_ref, sin_ref, T_R32, DIFF_DH // 2)
    for j in range(4):
        put(ZC_DQ + j * LANES, r[j])
    zn = _dot(h, w_ref[:, W_N:W_COLS])

    ckv = zm[:, 0:MLA_KV_LORA]
    b12 = zm[:, MLA_KV_LORA:]
    ckv_n = (ckv * _rms_scale(ckv, MLA_KV_LORA) * gkv_ref[...]).astype(BF16)
    lane = lax.broadcasted_iota(jnp.int32, b12.shape, 1)
    cq = jnp.where(lane < MLA_Q_LORA, b12, 0.0)
    cq_n = (cq * _rms_scale(cq, MLA_Q_LORA) * gq_ref[...]).astype(BF16)
    zq = _dot(cq_n, wuq_ref[...])
    put(ZC_NQ, zn[:, 0:256] * (HEAD_DIM ** -0.5 * LOG2E))
    put(ZC_NK, zn[:, 256:1024])
    put(ZC_SV, zn[:, 1024:1152])
    q = _rope_blocks(zq, cos_ref, sin_ref, T_MQ, MLA_ROPE // 2)
    for j in range(HEADS):
        put(ZC_MQ + j * LANES, q[j])
    kr = _rope_blocks(b12[:, LANES:], cos_ref, sin_ref, T_B12 + LANES, MLA_ROPE // 2)[0]
    lane = lax.broadcasted_iota(jnp.int32, kr.shape, 1)
    kr = jnp.where(jnp.where(lane >= MLA_NOPE, lane, LANES) < MLA_NOPE + MLA_ROPE, kr, 0.0)
    kn = _dot(ckv_n, wukvk_ref[...])
    for j in range(HEADS):
        put(ZC_MK + j * LANES, kn[:, j * LANES:(j + 1) * LANES] + kr)
    put(ZC_MV, _dot(ckv_n, wukvv_ref[...]))


def _inproj_call(x3, mod, mod_row, g, cos, sin, table_per_tile, w, gq, gkv, wuq, wukvk, wukvv, tm):
    nb, n, d = x3.shape
    nt = n // tm
    const = lambda t, b: (0, 0)
    tab_map = (lambda t, b: (t, 0)) if table_per_tile else const
    mod_map = (lambda t, b: (b, 0, 0)) if mod_row is None else (lambda t, b: (mod_row, 0, 0))
    return pl.pallas_call(
        _inproj_kernel,
        grid=(nt, nb),
        in_specs=[
            pl.BlockSpec((1, tm, d), lambda t, b: (b, t, 0)),
            pl.BlockSpec((1, 6, d), mod_map),
            pl.BlockSpec((1, d), const),
            pl.BlockSpec((tm, T_COLS), tab_map),
            pl.BlockSpec((tm, T_COLS), tab_map),
            pl.BlockSpec(w.shape, const, pipeline_mode=pl.Buffered(1)),
            pl.BlockSpec(gq.shape, const),
            pl.BlockSpec(gkv.shape, const),
            pl.BlockSpec(wuq.shape, const),
            pl.BlockSpec(wukvk.shape, const),
            pl.BlockSpec(wukvv.shape, const),
        ],
        out_specs=pl.BlockSpec((1, tm, Z_COLS), lambda t, b: (b, t, 0)),
        out_shape=jax.ShapeDtypeStruct((nb, n, Z_COLS), BF16),
        compiler_params=_params("arbitrary", "arbitrary"),
        name="inproj",
    )(x3, mod, g, cos, sin, w, gq, gkv, wuq, wukvk, wukvv)


def _lane_mask(q, lo, width):
    lane = lax.broadcasted_iota(jnp.int32, q.shape, 1)
    keep = jnp.where(lane >= lo, lane, LANES) < lo + width
    return jnp.where(keep, q.astype(F32), 0.0).astype(BF16)


def _localmax_softmax(n_units, chunks, score_of, extra_of, pv_of, e_scr):
    outs = [None] * n_units
    mloc, mfin = {}, {}
    n_c = len(chunks)
    for t in range(n_units + 1):
        acc = None
        if t < n_units:
            mloc[t] = []
        for cc in range(n_c + 1):
            if t < n_units and cc < n_c:
                off, rows = chunks[cc]
                s = score_of(t, cc)
                mc = jnp.max(s, axis=0, keepdims=True)
                e_scr[t % 2, off:off + rows, :] = jnp.exp2(s - mc).astype(BF16)
                mloc[t].append(mc)
            if t >= 1 and cc >= 1:
                u, c = t - 1, cc - 1
                off, rows = chunks[c]
                pv = pv_of(u, c, e_scr[u % 2, off:off + rows, :]) * jnp.exp2(mloc[u][c] - mfin[u])
                acc = pv if acc is None else acc + pv
        if t < n_units:
            m = mloc[t][0]
            for mc in mloc[t][1:]:
                m = jnp.maximum(m, mc)
            x = extra_of(t)
            mfin[t] = m if x is None else jnp.maximum(m, x)
        if t >= 1:
            l = acc[HEAD_DIM:HEAD_DIM + 1]
            x = extra_of(t - 1)
            if x is not None:
                l = l + jnp.exp2(x - mfin[t - 1])
            outs[t - 1] = acc[0:HEAD_DIM] * (1.0 / l)
    return outs


def _localmax_batched(n_units, chunks, score_of, extra_of, pv_of, e_scr):
    mloc, lloc, mfin = [], [], []
    for u in range(n_units):
        ms, ls = [], []
        for c, (off, rows) in enumerate(chunks):
            s = score_of(u, c)
            mc = jnp.max(s, axis=0, keepdims=True)
            e = jnp.exp2(s - mc)
            ls.append(jnp.sum(e, axis=0, keepdims=True))
            e_scr[u, off:off + rows, :] = e.astype(BF16)
            ms.append(mc)
        m = ms[0]
        for mc in ms[1:]:
            m = jnp.maximum(m, mc)
        x = extra_of(u)
        mloc.append(ms); lloc.append(ls); mfin.append(m if x is None else jnp.maximum(m, x))
    outs = []
    for u in range(n_units):
        acc = l = None
        for c, (off, rows) in enumerate(chunks):
            alpha = jnp.exp2(mloc[u][c] - mfin[u])
            pv = pv_of(u, c, e_scr[u, off:off + rows, :]) * alpha
            acc = pv if acc is None else acc + pv
            l = lloc[u][c] * alpha if l is None else l + lloc[u][c] * alpha
        x = extra_of(u)
        if x is not None:
            l = l + jnp.exp2(x - mfin[u])
        outs.append(acc * (1.0 / l))
    return outs


def _head_rows(o, h):
    return o[(h % 2) * HEAD_DIM:(h % 2 + 1) * HEAD_DIM]


def _diff_lambda(lp, lam_init):
    a = jnp.sum(lp[0:1] * lp[1:2], axis=-1, keepdims=True)
    b = jnp.sum(lp[2:3] * lp[3:4], axis=-1, keepdims=True)
    return jnp.exp(a) - jnp.exp(b) + lam_init


def _diff_combine(outs, lam):
    res = []
    for h in range(HEADS):
        o = outs[2 * h] - lam * outs[2 * h + 1]
        ms = jnp.sum(o * o, axis=0, keepdims=True) * (1.0 / HEAD_DIM)
        res.append(o * lax.rsqrt(ms + EPS))
    return jnp.concatenate(res, axis=0)


def _key_chunks(total, size):
    return [(o, min(size, total - o)) for o in range(0, total, size)]


def _exp_scratch(nk, tq, slots=2):
    return [pltpu.VMEM((slots, nk, tq), BF16)]


VT_ROWS = HEAD_DIM + 16


def _transpose_values(vt_scr, srcs):
    for ref, off, n in srcs:
        for r in range(0, n, KEY_CHUNK):
            rows = min(KEY_CHUNK, n - r)
            vt = ref[0, r:r + rows, :].astype(F32).T
            for h in range(HEADS):
                vt_scr[h * VT_ROWS:h * VT_ROWS + HEAD_DIM, off + r:off + r + rows] = (
                    vt[h * HEAD_DIM:(h + 1) * HEAD_DIM].astype(BF16))
                vt_scr[h * VT_ROWS + HEAD_DIM:(h + 1) * VT_ROWS, off + r:off + r + rows] = (
                    jnp.ones((VT_ROWS - HEAD_DIM, rows), BF16))


def _mla_kernel(q_ref, k_ref, v_ref, kc_ref, vc_ref, o_ref, e_scr, vt_scr):
    s_len, c_len = k_ref.shape[1], kc_ref.shape[1]

    @pl.when(pl.program_id(1) == 0)
    def _():
        _transpose_values(vt_scr, [(v_ref, 0, s_len), (vc_ref, s_len, c_len)])

    chunks = _key_chunks(s_len, KEY_CHUNK) + [(s_len + o, r) for o, r in _key_chunks(c_len, KEY_CHUNK)]
    units = [(qt, h) for qt in range(q_ref.shape[1] // UNIT_Q) for h in range(HEADS)]
    qs = [q_ref[0, qt * UNIT_Q:(qt + 1) * UNIT_Q, h * LANES:(h + 1) * LANES] for qt, h in units]

    def score_of(u, c):
        off, rows = chunks[c]
        h = units[u][1]
        sl = slice(h * LANES, (h + 1) * LANES)
        if off < s_len:
            return _dot_nt(k_ref[0, off:off + rows, sl], qs[u])
        return _dot_nt(kc_ref[0, off - s_len:off - s_len + rows, sl], qs[u])

    def pv_of(u, c, e):
        h = units[u][1]
        off, rows = chunks[c]
        return _dot(vt_scr[h * VT_ROWS:(h + 1) * VT_ROWS, off:off + rows], e)

    outs = _localmax_softmax(len(units), chunks, score_of, lambda u: None, pv_of, e_scr)
    for qt in range(len(units) // HEADS):
        o = jnp.concatenate(outs[qt * HEADS:(qt + 1) * HEADS], axis=0)
        o_ref[0, qt * UNIT_Q:(qt + 1) * UNIT_Q, :] = o.T.astype(BF16)


def _mla_call(zl, zc, tq):
    b, s, _ = zl.shape
    c = zc.shape[1]
    return pl.pallas_call(
        _mla_kernel,
        grid=(b, s // tq),
        in_specs=[
            pl.BlockSpec((1, tq, 512), lambda i, j: (i, j, ZC_MQ // 512)),
            pl.BlockSpec((1, s, 512), lambda i, j: (i, 0, ZC_MK // 512)),
            pl.BlockSpec((1, s, 256), lambda i, j: (i, 0, ZC_MV // 256)),
            pl.BlockSpec((1, c, 512), lambda i, j: (i, 0, ZC_MK // 512)),
            pl.BlockSpec((1, c, 256), lambda i, j: (i, 0, ZC_MV // 256)),
        ],
        out_specs=pl.BlockSpec((1, tq, 256), lambda i, j: (i, j, 0)),
        out_shape=jax.ShapeDtypeStruct((b, s, 256), BF16),
        scratch_shapes=_exp_scratch(s + c, UNIT_Q) + [pltpu.VMEM((HEADS * VT_ROWS, s + c), BF16)],
        compiler_params=_params("arbitrary", "arbitrary"),
        name="mla",
    )(zl, zl, zl, zc, zc)


def _diff_kernel(lam_init, q_ref, k_ref, v_ref, kc_ref, vc_ref, lp_ref, gain_ref, o_ref,
                 e_scr, vt_scr):
    s_len, c_len = k_ref.shape[1], kc_ref.shape[1]

    @pl.when(pl.program_id(1) == 0)
    def _():
        _transpose_values(vt_scr, [(v_ref, 0, s_len), (vc_ref, s_len, c_len)])

    chunks = _key_chunks(s_len, KEY_CHUNK) + [(s_len + o, r) for o, r in _key_chunks(c_len, KEY_CHUNK)]
    n_sub = 2 * HEADS
    units = [(qt, w) for qt in range(q_ref.shape[1] // UNIT_Q) for w in range(n_sub)]
    qs = [_lane_mask(q_ref[0, qt * UNIT_Q:(qt + 1) * UNIT_Q, (w // 4) * LANES:(w // 4 + 1) * LANES],
                     (w % 4) * DIFF_DH, DIFF_DH) for qt, w in units]

    def score_of(u, c):
        off, rows = chunks[c]
        w = units[u][1]
        sl = slice((w // 4) * LANES, (w // 4 + 1) * LANES)
        if off < s_len:
            return _dot_nt(k_ref[0, off:off + rows, sl], qs[u])
        return _dot_nt(kc_ref[0, off - s_len:off - s_len + rows, sl], qs[u])

    def pv_of(u, c, e):
        h = units[u][1] // 2
        off, rows = chunks[c]
        return _dot(vt_scr[h * VT_ROWS:(h + 1) * VT_ROWS, off:off + rows], e)

    outs = _localmax_softmax(len(units), chunks, score_of, lambda u: None, pv_of, e_scr)
    lam = _diff_lambda(lp_ref[...], lam_init)
    for qt in range(len(units) // n_sub):
        o = _diff_combine(outs[qt * n_sub:(qt + 1) * n_sub], lam)
        o_ref[0, qt * UNIT_Q:(qt + 1) * UNIT_Q, :] = (o.T * gain_ref[...]).astype(BF16)


def _diff_call(zl, zc, lp, gain, lam_init, tq):
    b, s, _ = zl.shape
    c = zc.shape[1]
    return pl.pallas_call(
        functools.partial(_diff_kernel, lam_init),
        grid=(b, s // tq),
        in_specs=[
            pl.BlockSpec((1, tq, 256), lambda i, j: (i, j, ZC_DQ // 256)),
            pl.BlockSpec((1, s, 256), lambda i, j: (i, 0, ZC_DK // 256)),
            pl.BlockSpec((1, s, 256), lambda i, j: (i, 0, ZC_DV // 256)),
            pl.BlockSpec((1, c, 256), lambda i, j: (i, 0, ZC_DK // 256)),
            pl.BlockSpec((1, c, 256), lambda i, j: (i, 0, ZC_DV // 256)),
            pl.BlockSpec(lp.shape, lambda i, j: (0, 0)),
            pl.BlockSpec(gain.shape, lambda i, j: (0, 0)),
        ],
        out_specs=pl.BlockSpec((1, tq, 256), lambda i, j: (i, j, 0)),
        out_shape=jax.ShapeDtypeStruct((b, s, 256), BF16),
        scratch_shapes=_exp_scratch(s + c, UNIT_Q) + [pltpu.VMEM((HEADS * VT_ROWS, s + c), BF16)],
        compiler_params=_params("arbitrary", "arbitrary"),
        name="diff",
    )(zl, zl, zl, zc, zc, lp, gain)


def _swa_kernel(q_ref, k_ref, v_ref, kc_ref, vc_ref, sink_ref, o_ref, e_scr, mask_scr):
    s_len, c_len = k_ref.shape[1], kc_ref.shape[1]
    n_qt = q_ref.shape[1] // UNIT_Q
    win = UNIT_Q + 2 * SWA_WINDOW
    chunks = _key_chunks(win, KEY_CHUNK) + [(win + o, r) for o, r in _key_chunks(c_len, KEY_CHUNK)]
    starts = []
    for qt in range(n_qt):
        q0 = (pl.program_id(1) * n_qt + qt) * UNIT_Q
        start = pl.multiple_of(jnp.clip(q0 - SWA_WINDOW, 0, s_len - win), SWA_WINDOW)
        kpos = start + lax.broadcasted_iota(jnp.int32, (win, UNIT_Q), 0)
        qpos = q0 + lax.broadcasted_iota(jnp.int32, (win, UNIT_Q), 1)
        mask_scr[qt] = jnp.where(jnp.abs(kpos - qpos) <= SWA_WINDOW, 0.0, NEG_INF)
        starts.append(start)
    units = [(qt, w) for qt in range(n_qt) for w in range(HEADS)]
    qs = [_lane_mask(q_ref[0, qt * UNIT_Q:(qt + 1) * UNIT_Q, (w // SWA_HKV) * LANES:(w // SWA_HKV + 1) * LANES],
                     (w % SWA_HKV) * HEAD_DIM, HEAD_DIM) for qt, w in units]
    head_of = lambda w: (w % SWA_HKV) * SWA_GROUP + w // SWA_HKV

    def score_of(u, c):
        off, rows = chunks[c]
        qt = units[u][0]
        if off < win:
            k = k_ref[0, pl.ds(starts[qt] + off, rows), :]
            return _dot_nt(k, qs[u]) + mask_scr[qt, off:off + rows, :]
        return _dot_nt(kc_ref[0, off - win:off - win + rows, :], qs[u])

    def extra_of(u):
        return jnp.full((1, 1), sink_ref[head_of(units[u][1])] * LOG2E, F32)

    def pv_of(u, c, e):
        qt, w = units[u]
        off, rows = chunks[c]
        if off < win:
            o = _dot_tn(v_ref[0, pl.ds(starts[qt] + off, rows), :], e)
        else:
            o = _dot_tn(vc_ref[0, off - win:off - win + rows, :], e)
        return o[(w % SWA_HKV) * HEAD_DIM:(w % SWA_HKV + 1) * HEAD_DIM]

    outs = _localmax_batched(len(units), chunks, score_of, extra_of, pv_of, e_scr)
    for qt in range(n_qt):
        by_head = {head_of(w): outs[qt * HEADS + w] for w in range(HEADS)}
        o = jnp.concatenate([by_head[h] for h in range(HEADS)], axis=0)
        o_ref[0, qt * UNIT_Q:(qt + 1) * UNIT_Q, :] = o.T.astype(BF16)


def _swa_call(zl, zc, sink, tq):
    b, s, _ = zl.shape
    c = zc.shape[1]
    win = UNIT_Q + 2 * SWA_WINDOW
    return pl.pallas_call(
        _swa_kernel,
        grid=(b, s // tq),
        in_specs=[
            pl.BlockSpec((1, tq, 256), lambda i, j: (i, j, ZC_SQ // 256)),
            pl.BlockSpec((1, s, LANES), lambda i, j: (i, 0, ZC_SK // LANES)),
            pl.BlockSpec((1, s, LANES), lambda i, j: (i, 0, ZC_SV // LANES)),
            pl.BlockSpec((1, c, LANES), lambda i, j: (i, 0, ZC_SK // LANES)),
            pl.BlockSpec((1, c, LANES), lambda i, j: (i, 0, ZC_SV // LANES)),
            pl.BlockSpec(memory_space=pltpu.SMEM),
        ],
        out_specs=pl.BlockSpec((1, tq, 256), lambda i, j: (i, j, 0)),
        out_shape=jax.ShapeDtypeStruct((b, s, 256), BF16),
        scratch_shapes=(_exp_scratch(win + c, UNIT_Q, HEADS * tq // UNIT_Q)
                        + [pltpu.VMEM((tq // UNIT_Q, win, UNIT_Q), F32)]),
        compiler_params=_params("arbitrary", "arbitrary"),
        name="swa",
    )(zl, zl, zl, zc, zc, sink)


def _na_kernel(rows_n, q_ref, k_ref, v_ref, kc_ref, vc_ref, bias_ref, o_ref, e_scr):
    c_len = kc_ref.shape[1]
    n_rb = q_ref.shape[1] // NA_TQ
    nrb = rows_n // 2
    chunks = _key_chunks(NA_NLOC, KEY_CHUNK) + [(NA_NLOC + o, r) for o, r in _key_chunks(c_len, KEY_CHUNK)]
    starts, tids = [], []
    for i in range(n_rb):
        rb = pl.program_id(1) * n_rb + i
        starts.append(pl.multiple_of(
            jnp.clip(2 * rb - NA_WIN_R // 2, 0, rows_n - NA_KROWS) * GRID_W, 2 * GRID_W))
        tids.append(jnp.minimum(rb, 2) + jnp.maximum(rb - (nrb - 3), 0))
    blk = lambda h: slice((h // 2) * LANES, (h // 2 + 1) * LANES)
    units = [(i, h) for i in range(n_rb) for h in range(HEADS)]
    qs = [_lane_mask(q_ref[0, i * NA_TQ:(i + 1) * NA_TQ, blk(h)], (h % 2) * HEAD_DIM, HEAD_DIM)
          for i, h in units]

    def score_of(u, c):
        off, rows = chunks[c]
        i, h = units[u]
        if off < NA_NLOC:
            k = k_ref[0, pl.ds(starts[i] + off, rows), blk(h)]
            return _dot_nt(k, qs[u]) + bias_ref[h, tids[i], off:off + rows, :]
        return _dot_nt(kc_ref[0, off - NA_NLOC:off - NA_NLOC + rows, blk(h)], qs[u])

    def pv_of(u, c, e):
        i, h = units[u]
        off, rows = chunks[c]
        if off < NA_NLOC:
            return _head_rows(_dot_tn(v_ref[0, pl.ds(starts[i] + off, rows), blk(h)], e), h)
        return _head_rows(_dot_tn(vc_ref[0, off - NA_NLOC:off - NA_NLOC + rows, blk(h)], e), h)

    outs = _localmax_batched(len(units), chunks, score_of, lambda u: None, pv_of, e_scr)
    for i in range(n_rb):
        o = jnp.concatenate(outs[i * HEADS:(i + 1) * HEADS], axis=0)
        o_ref[0, i * NA_TQ:(i + 1) * NA_TQ, :] = o.T.astype(BF16)


def _na_call(zl, zc, bias, tq):
    b, s, _ = zl.shape
    c = zc.shape[1]
    rows_n = s // GRID_W
    return pl.pallas_call(
        functools.partial(_na_kernel, rows_n),
        grid=(b, s // tq),
        in_specs=[
            pl.BlockSpec((1, tq, 256), lambda i, j: (i, j, ZC_NQ // 256)),
            pl.BlockSpec((1, s, 256), lambda i, j: (i, 0, ZC_NK // 256)),
            pl.BlockSpec((1, s, 256), lambda i, j: (i, 0, ZC_NV // 256)),
            pl.BlockSpec((1, c, 256), lambda i, j: (i, 0, ZC_NK // 256)),
            pl.BlockSpec((1, c, 256), lambda i, j: (i, 0, ZC_NV // 256)),
            pl.BlockSpec(bias.shape, lambda i, j: (0, 0, 0, 0), pipeline_mode=pl.Buffered(1)),
        ],
        out_specs=pl.BlockSpec((1, tq, 256), lambda i, j: (i, j, 0)),
        out_shape=jax.ShapeDtypeStruct((b, s, 256), BF16),
        scratch_shapes=_exp_scratch(NA_NLOC + c, NA_TQ, HEADS * tq // NA_TQ),
        compiler_params=_params("arbitrary", "arbitrary"),
        name="na",
    )(zl, zl, zl, zc, zc, bias)


def _ctx_kernel(lam_init, z_ref, lp_ref, gain_ref, sink_ref, oa_ref, ob_ref, oc_ref, od_ref,
                e_scr):
    c_len = z_ref.shape[1]

    def blk(col, j=0):
        return z_ref[0, :, col + j * LANES:col + (j + 1) * LANES]

    units = []
    for h in range(HEADS):
        units.append((_lane_mask(blk(ZC_NQ, h // 2), (h % 2) * HEAD_DIM, HEAD_DIM),
                      ZC_NK + (h // 2) * LANES, ZC_NV + (h // 2) * LANES, (h % 2) * HEAD_DIM, None))
    for h in range(HEADS):
        units.append((blk(ZC_MQ, h), ZC_MK + h * LANES, ZC_MV + (h // 2) * LANES,
                      (h % 2) * HEAD_DIM, None))
    for h in range(HEADS):
        hk, g = h // SWA_GROUP, h % SWA_GROUP
        units.append((_lane_mask(blk(ZC_SQ, g), hk * HEAD_DIM, HEAD_DIM), ZC_SK, ZC_SV,
                      hk * HEAD_DIM, h))
    for w in range(2 * HEADS):
        units.append((_lane_mask(blk(ZC_DQ, w // 4), (w % 4) * DIFF_DH, DIFF_DH),
                      ZC_DK + (w // 4) * LANES, ZC_DV + (w // 4) * LANES, ((w // 2) % 2) * HEAD_DIM, None))

    def score_of(u, c):
        q_m, kcol = units[u][0], units[u][1]
        return _dot_nt(z_ref[0, :, kcol:kcol + LANES], q_m)

    def extra_of(u):
        h = units[u][4]
        return None if h is None else jnp.full((1, 1), sink_ref[h] * LOG2E, F32)

    def pv_of(u, c, e):
        vcol, r0 = units[u][2], units[u][3]
        return _dot_tn(z_ref[0, :, vcol:vcol + LANES], e)[r0:r0 + HEAD_DIM]

    outs = _localmax_batched(len(units), [(0, c_len)], score_of, extra_of, pv_of, e_scr)
    for i, ref in enumerate((oa_ref, ob_ref, oc_ref)):
        ref[0] = jnp.concatenate(outs[i * HEADS:(i + 1) * HEADS], axis=0).T.astype(BF16)
    o = _diff_combine(outs[3 * HEADS:], _diff_lambda(lp_ref[...], lam_init))
    od_ref[0] = (o.T * gain_ref[...]).astype(BF16)


def _ctx_call(zc, lp, gain, sink, lam_init):
    b, c, _ = zc.shape
    out = jax.ShapeDtypeStruct((b, c, 256), BF16)
    ospec = pl.BlockSpec((1, c, 256), lambda i: (i, 0, 0))
    return pl.pallas_call(
        functools.partial(_ctx_kernel, lam_init),
        grid=(b,),
        in_specs=[
            pl.BlockSpec((1, c, Z_COLS), lambda i: (i, 0, 0)),
            pl.BlockSpec(lp.shape, lambda i: (0, 0)),
            pl.BlockSpec(gain.shape, lambda i: (0, 0)),
            pl.BlockSpec(memory_space=pltpu.SMEM),
        ],
        out_specs=[ospec] * 4,
        out_shape=[out] * 4,
        scratch_shapes=_exp_scratch(c, c, 5 * HEADS),
        compiler_params=_params("arbitrary"),
        name="ctx_attn",
    )(zc, lp, gain, sink)


def _mlp_kernel(final, x_ref, ma_ref, mb_ref, mc_ref, md_ref, mod_ref, g_ref, gf_ref,
                wout_ref, wup_ref, wdn_ref, o_ref):
    x = x_ref[0]
    mod = mod_ref[0]
    attn = None
    for i, m_ref in enumerate((ma_ref, mb_ref, mc_ref, md_ref)):
        part = _dot(m_ref[0], wout_ref[i * 256:(i + 1) * 256, :])
        attn = part if attn is None else attn + part
    x = x + mod[2:3] * attn
    h = x * _rms_scale(x, D_MODEL) * g_ref[...]
    h = (h * (1.0 + mod[4:5]) + mod[3:4]).astype(BF16)
    acc = None
    ck = 1024
    for c in range(D_FF // ck):
        u = jnp.maximum(_dot(h, wup_ref[:, c * ck:(c + 1) * ck]), 0.0)
        part = _dot((u * u).astype(BF16), wdn_ref[c * ck:(c + 1) * ck, :])
        acc = part if acc is None else acc + part
    x = x + mod[5:6] * acc
    if final:
        x = x * _rms_scale(x, D_MODEL) * gf_ref[...]
    o_ref[0] = x


def _mlp_call(x3, mixes, mod, mod_row, g, gf, wout, wup, wdn, tm, final):
    nb, n, d = x3.shape
    const = lambda t, b: (0, 0)
    tok = lambda t, b: (b, t, 0)
    mod_map = (lambda t, b: (b, 0, 0)) if mod_row is None else (lambda t, b: (mod_row, 0, 0))
    single = pl.Buffered(1)
    return pl.pallas_call(
        functools.partial(_mlp_kernel, final),
        grid=(n // tm, nb),
        in_specs=[pl.BlockSpec((1, tm, d), tok)]
        + [pl.BlockSpec((1, tm, 256), tok)] * 4
        + [
            pl.BlockSpec((1, 6, d), mod_map),
            pl.BlockSpec((1, d), const),
            pl.BlockSpec((1, d), const),
            pl.BlockSpec(wout.shape, const, pipeline_mode=single),
            pl.BlockSpec(wup.shape, const, pipeline_mode=single),
            pl.BlockSpec(wdn.shape, const, pipeline_mode=single),
        ],
        out_specs=pl.BlockSpec((1, tm, d), tok),
        out_shape=jax.ShapeDtypeStruct((nb, n, d), F32),
        compiler_params=_params("arbitrary", "arbitrary"),
        name="mlp",
    )(x3, *mixes, mod, g, gf, wout, wup, wdn)


def _rope_tables(s_len):
    t = np.arange(s_len)
    row, col = t // GRID_W, t % GRID_W

    def axial(dim):
        n_freq = dim // 4
        freqs = jnp.asarray(ROPE_BASE, F32) ** (-jnp.arange(n_freq, dtype=F32) / n_freq)
        ang = jnp.concatenate([jnp.asarray(row, F32)[:, None] * freqs,
                               jnp.asarray(col, F32)[:, None] * freqs], axis=-1)
        return jnp.cos(ang), jnp.sin(ang)

    def group(cs, n_groups, scale):
        cos, sin = cs
        c = jnp.concatenate([cos, cos], axis=-1) * scale
        s = jnp.concatenate([-sin, sin], axis=-1) * scale
        return jnp.tile(c, (1, n_groups)), jnp.tile(s, (1, n_groups))

    r64, r32 = axial(HEAD_DIM), axial(MLA_ROPE)
    ones = lambda n, v=1.0: jnp.full((s_len, n), v, F32)
    zeros = lambda n: jnp.zeros((s_len, n), F32)
    mla_scale = (MLA_NOPE + MLA_ROPE) ** -0.5 * LOG2E
    mq_c, mq_s = group(r32, 1, mla_scale)
    parts = [
        group(r64, 4, HEAD_DIM ** -0.5 * LOG2E), group(r64, 2, 1.0),
        group(r32, 8, DIFF_DH ** -0.5 * LOG2E), group(r32, 8, 1.0),
        (ones(MLA_Q_LORA), zeros(MLA_Q_LORA)), group(r32, 1, 1.0), (ones(32), zeros(32)),
    ] + [(jnp.concatenate([ones(MLA_NOPE, mla_scale), mq_c, ones(32, mla_scale)], axis=-1),
          jnp.concatenate([zeros(MLA_NOPE), mq_s, zeros(32)], axis=-1))] * HEADS
    cos = jnp.concatenate([p[0] for p in parts], axis=-1)
    sin = jnp.concatenate([p[1] for p in parts], axis=-1)
    return cos, sin, cos[0:1]


def _permute_w_in(w_in):
    offs = np.cumsum([0, 256, 256, 256, MLA_Q_LORA, MLA_KV_LORA, MLA_ROPE, 256, 128, 128, 256, 256, 256])
    sec = lambda i: w_in[..., offs[i]:offs[i + 1]]
    sq = sec(6)
    sq = sq.reshape(sq.shape[:-1] + (SWA_HKV, SWA_GROUP, HEAD_DIM))
    sq = jnp.swapaxes(sq, -3, -2).reshape(w_in.shape[:-1] + (256,))
    pad = jnp.zeros(w_in.shape[:-1] + (32,), w_in.dtype)
    cols = [sq, sec(7), sec(9), sec(10), sec(4), sec(3), sec(5), pad,
            sec(0), sec(1), sec(2), sec(11), sec(8)]
    return jnp.concatenate(cols, axis=-1).astype(BF16)


def _mla_weights(w_uq, w_ukv):
    depth = w_uq.shape[0]
    uq = w_uq.reshape(depth, MLA_Q_LORA, HEADS, MLA_NOPE + MLA_ROPE)
    uq = jnp.pad(uq, ((0, 0), (0, 256 - MLA_Q_LORA), (0, 0), (0, LANES - MLA_NOPE - MLA_ROPE)))
    uq = uq.reshape(depth, 256, HEADS * LANES).astype(BF16)
    ukv = w_ukv.reshape(depth, MLA_KV_LORA, HEADS, 2, HEAD_DIM)
    uk = jnp.pad(ukv[:, :, :, 0], ((0, 0), (0, 0), (0, 0), (0, LANES - MLA_NOPE)))
    uk = uk.reshape(depth, MLA_KV_LORA, HEADS * LANES).astype(BF16)
    uv = ukv[:, :, :, 1].reshape(depth, MLA_KV_LORA, HEADS * HEAD_DIM).astype(BF16)
    return uq, uk, uv


def _na_bias_tables(rpb, rows_n):
    kr_n = min(NA_WIN_R, rows_n)
    col = np.arange(GRID_W)
    c0 = np.clip(col - NA_WIN_C // 2, 0, GRID_W - NA_WIN_C)
    col_ok_t = ((col[None, :] >= c0[:, None]) & (col[None, :] < c0[:, None] + NA_WIN_C)).T
    dc_idx_t = np.clip(col[:, None] - col[None, :], 1 - NA_WIN_C, NA_WIN_C - 1) + (NA_WIN_C - 1)
    toe = jnp.zeros(rpb.shape[:3] + (GRID_W, GRID_W), F32)
    for d in range(2 * NA_WIN_C - 1):
        toe = jnp.where(dc_idx_t == d, rpb[..., d, None, None].astype(F32), toe)
    toe = jnp.where(col_ok_t, toe * LOG2E, NEG_INF)
    neg = jnp.full(toe.shape[:2] + (GRID_W, GRID_W), NEG_INF, F32)
    nrb = rows_n // 2
    pairs = {}

    def pair(drs):
        if drs not in pairs:
            pairs[drs] = jnp.concatenate([neg if d is None else toe[:, :, d] for d in drs], axis=-1)
        return pairs[drs]

    blocks = []
    for rb in (0, 1, 2, nrb - 2, nrb - 1):
        start = int(np.clip(2 * rb - NA_WIN_R // 2, 0, rows_n - NA_KROWS))
        for j in range(NA_KROWS):
            drs = []
            for i in range(2):
                kr, qr = start + j, 2 * rb + i
                r0 = int(np.clip(qr - kr_n // 2, 0, rows_n - kr_n))
                drs.append(kr - qr + NA_WIN_R - 1 if r0 <= kr < r0 + kr_n else None)
            blocks.append(pair(tuple(drs)))
    tab = jnp.stack(blocks, axis=2)
    return tab.reshape(tab.shape[:2] + (5, NA_NLOC, NA_TQ))


def kernel(x, c, ctx, c_ctx, w_ada, b_ada, norm_attn_g, w_in, na_rpb, mla_q_norm_g, mla_w_uq,
           mla_kv_norm_g, mla_w_ukv, swa_sink, diff_lambda, diff_norm_g, w_out, norm_mlp_g,
           w_up, w_down, final_norm_g):
    b, s, d = x.shape
    c_len = ctx.shape[1]
    depth = w_in.shape[0]
    rows_n = s // GRID_W

    w_in_p = _permute_w_in(w_in)
    uq, uk, uv = _mla_weights(mla_w_uq, mla_w_ukv)
    gq = jnp.pad(mla_q_norm_g, ((0, 0), (0, 256 - MLA_Q_LORA)))[:, None, :]
    gkv = mla_kv_norm_g[:, None, :]
    w_out_b, w_up_b, w_dn_b = w_out.astype(BF16), w_up.astype(BF16), w_down.astype(BF16)
    na_bias = _na_bias_tables(na_rpb, rows_n)
    cos, sin, cos_ctx = _rope_tables(s)
    tm_l, tm_c = 512, min(512, b * c_len)
    cos_c = jnp.broadcast_to(cos_ctx, (tm_c, T_COLS))
    sin_c = jnp.zeros((tm_c, T_COLS), F32)

    c_rows = -(-(b + 1) // 8) * 8
    c_all = jnp.zeros((c_rows, d), F32).at[:b].set(c).at[b].set(c_ctx)
    mod = _ada_call(c_all, w_ada, b_ada).reshape(depth, c_rows, 6, d)

    xl = x
    xc = ctx.reshape(1, b * c_len, d)
    for l in range(depth):
        need_ctx = l < depth - 1
        lam_init = 0.8 - 0.6 * math.exp(-0.3 * l)
        g_attn = norm_attn_g[l][None, :]
        g_mlp = norm_mlp_g[l][None, :]
        gf = final_norm_g[None, :]
        d_gain = (jnp.tile(diff_norm_g[l], HEADS) * (1.0 - lam_init))[None, :]
        proj = (w_in_p[l], gq[l], gkv[l], uq[l], uk[l], uv[l])
        zl = _inproj_call(xl, mod[l], None, g_attn, cos, sin, True, *proj, tm=tm_l)
        zc = _inproj_call(xc, mod[l], b, g_attn, cos_c, sin_c, False, *proj, tm=tm_c)
        zc = zc.reshape(b, c_len, Z_COLS)
        mixes = (
            _na_call(zl, zc, na_bias[l], tq=1024),
            _mla_call(zl, zc, tq=1024),
            _swa_call(zl, zc, swa_sink[l], tq=1024),
            _diff_call(zl, zc, diff_lambda[l], d_gain, lam_init, tq=512),
        )
        wts = (w_out_b[l], w_up_b[l], w_dn_b[l])
        xl = _mlp_call(xl, mixes, mod[l], None, g_mlp, gf, *wts, tm=tm_l, final=not need_ctx)
        if need_ctx:
            mixes_c = _ctx_call(zc, diff_lambda[l], d_gain, swa_sink[l], lam_init)
            mixes_c = [m.reshape(1, b * c_len, 256) for m in mixes_c]
            xc = _mlp_call(xc, mixes_c, mod[l], b, g_mlp, gf, *wts, tm=tm_c, final=False)
    return xl
```

```python
import functools
import math

import jax
import jax.numpy as jnp
import numpy as np
from jax import lax
from jax.experimental import pallas as pl
from jax.experimental.pallas import tpu as pltpu

F32 = jnp.float32
BF16 = jnp.bfloat16

D_MODEL = 1024
GRID_W = 64
HEADS = 4
HEAD_DIM = 64
NA_WIN_R = 8
NA_WIN_C = 16
MLA_Q_LORA = 192
MLA_KV_LORA = 128
MLA_NOPE = 64
MLA_ROPE = 32
SWA_HKV = 2
SWA_GROUP = 2
SWA_WINDOW = 128
DIFF_DH = 32
D_FF = 4 * D_MODEL
ROPE_BASE = 10000.0
EPS = 1e-6
NEG_INF = -1e30
LOG2E = math.log2(math.e)

LANES = 128
VMEM_LIMIT = 56 * 1024 * 1024

ZC_DQ, ZC_DK, ZC_SQ, ZC_NQ, ZC_NK, ZC_NV, ZC_DV, ZC_MV = 0, 256, 512, 768, 1024, 1280, 1536, 1792
ZC_MQ, ZC_MK, ZC_SK, ZC_SV = 2048, 2560, 3072, 3200
Z_COLS = 3328
W_R64, W_R32, W_M, W_N, W_COLS = 0, 384, 896, 1280, 2432
T_R64, T_R32, T_B12, T_MQ, T_COLS = 0, 384, 896, 1152, 1664

NA_KROWS = 10
NA_TQ = 2 * GRID_W
NA_NLOC = NA_KROWS * GRID_W
KEY_CHUNK = 256
UNIT_Q = 256
MIX_W = HEADS * HEAD_DIM
MLA_W = HEADS * LANES

TOKEN_TILE = 512
ATTN_TQ = 1024
DIFF_TQ = 1024
ADA_TN = 1536
FF_CHUNK = 1024


def _params(*sem):
    return pltpu.CompilerParams(dimension_semantics=sem, vmem_limit_bytes=VMEM_LIMIT)


def _dot(a, b):
    return jnp.dot(a, b, preferred_element_type=F32)


def _dot_nt(a, b):
    return lax.dot_general(a, b, (((1,), (1,)), ((), ())), preferred_element_type=F32)


def _dot_tn(a, b):
    return lax.dot_general(a, b, (((0,), (0,)), ((), ())), preferred_element_type=F32)


def _rms_scale(x, n):
    return lax.rsqrt(jnp.sum(x * x, axis=-1, keepdims=True) * (1.0 / n) + EPS)


def _ada_kernel(c_ref, w_ref, b_ref, o_ref):
    c = c_ref[...]
    act = (c * (1.0 / (1.0 + jnp.exp(-c)))).astype(BF16)
    o_ref[0] = _dot(act, w_ref[0].astype(BF16)) + b_ref[0]


def _ada_call(c_all, w_ada, b_ada):
    depth, d, n = w_ada.shape
    rows = c_all.shape[0]
    tn = ADA_TN
    return pl.pallas_call(
        _ada_kernel,
        grid=(depth, n // tn),
        in_specs=[
            pl.BlockSpec((rows, d), lambda l, j: (0, 0)),
            pl.BlockSpec((1, d, tn), lambda l, j: (l, 0, j)),
            pl.BlockSpec((1, 1, tn), lambda l, j: (l, 0, j)),
        ],
        out_specs=pl.BlockSpec((1, rows, tn), lambda l, j: (l, 0, j)),
        out_shape=jax.ShapeDtypeStruct((depth, rows, n), F32),
        compiler_params=_params("arbitrary", "arbitrary"),
        name="ada",
    )(c_all, w_ada, b_ada.reshape(depth, 1, n))


def _rope_blocks(z, cos_ref, sin_ref, tcol, half):
    tm = z.shape[0]
    lane = lax.broadcasted_iota(jnp.int32, (tm, LANES), 1)
    first = (lane % (2 * half)) < half
    out = []
    for j in range(z.shape[1] // LANES):
        xb = z[:, j * LANES:(j + 1) * LANES]
        rot = jnp.where(first, pltpu.roll(xb, LANES - half, 1), pltpu.roll(xb, half, 1))
        c0 = tcol + j * LANES
        out.append(xb * cos_ref[:, c0:c0 + LANES] + rot * sin_ref[:, c0:c0 + LANES])
    return out


def _inproj_kernel(x_ref, mod_ref, g_ref, cos_ref, sin_ref, w_ref, gq_ref, gkv_ref,
                   wuq_ref, wukvk_ref, wukvv_ref, o_ref):
    x = x_ref[0]
    mod = mod_ref[0]
    h = x * _rms_scale(x, D_MODEL) * g_ref[...]
    h = (h * (1.0 + mod[1:2]) + mod[0:1]).astype(BF16)

    def put(col, val):
        o_ref[0, :, col:col + val.shape[1]] = val.astype(BF16)

    z64 = _dot(h, w_ref[:, W_R64:W_R32])
    z32 = _dot(h, w_ref[:, W_R32:W_M])
    r = _rope_blocks(z64, cos_ref, sin_ref, T_R64, HEAD_DIM // 2)
    put(ZC_SQ, r[0]); put(ZC_SQ + LANES, r[1]); put(ZC_SK, r[2])
    zm = _dot(h, w_ref[:, W_M:W_N])
    r = _rope_blocks(z32, cos_ref, sin_ref, T_R32, DIFF_DH // 2)
    for j in range(4):
        put(ZC_DQ + j * LANES, r[j])
    zn = _dot(h, w_ref[:, W_N:W_COLS])

    ckv = zm[:, 0:MLA_KV_LORA]
    b12 = zm[:, MLA_KV_LORA:]
    ckv_n = (ckv * _rms_scale(ckv, MLA_KV_LORA) * gkv_ref[...]).astype(BF16)
    lane = lax.broadcasted_iota(jnp.int32, b12.shape, 1)
    cq = jnp.where(lane < MLA_Q_LORA, b12, 0.0)
    cq_n = (cq * _rms_scale(cq, MLA_Q_LORA) * gq_ref[...]).astype(BF16)
    zq = _dot(cq_n, wuq_ref[...])
    put(ZC_NQ, zn[:, 0:256] * (HEAD_DIM ** -0.5 * LOG2E))
    put(ZC_NK, zn[:, 256:1024])
    put(ZC_SV, zn[:, 1024:1152])
    q = _rope_blocks(zq, cos_ref, sin_ref, T_MQ, MLA_ROPE // 2)
    for j in range(HEADS):
        put(ZC_MQ + j * LANES, q[j])
    kr = _rope_blocks(b12[:, LANES:], cos_ref, sin_ref, T_B12 + LANES, MLA_ROPE // 2)[0]
    lane = lax.broadcasted_iota(jnp.int32, kr.shape, 1)
    kr = jnp.where(jnp.where(lane >= MLA_NOPE, lane, LANES) < MLA_NOPE + MLA_ROPE, kr, 0.0)
    kn = _dot(ckv_n, wukvk_ref[...])
    for j in range(HEADS):
        put(ZC_MK + j * LANES, kn[:, j * LANES:(j + 1) * LANES] + kr)
    put(ZC_MV, _dot(ckv_n, wukvv_ref[...]))


def _inproj_call(x3, mod, mod_row, g, cos, sin, table_per_tile, w, gq, gkv, wuq, wukvk, wukvv, tm):
    nb, n, d = x3.shape
    nt = n // tm
    const = lambda t, b: (0, 0)
    tab_map = (lambda t, b: (t, 0)) if table_per_tile else const
    mod_map = (lambda t, b: (b, 0, 0)) if mod_row is None else (lambda t, b: (mod_row, 0, 0))
    return pl.pallas_call(
        _inproj_kernel,
        grid=(nt, nb),
        in_specs=[
            pl.BlockSpec((1, tm, d), lambda t, b: (b, t, 0)),
            pl.BlockSpec((1, 6, d), mod_map),
            pl.BlockSpec((1, d), const),
            pl.BlockSpec((tm, T_COLS), tab_map),
            pl.BlockSpec((tm, T_COLS), tab_map),
            pl.BlockSpec(w.shape, const, pipeline_mode=pl.Buffered(1)),
            pl.BlockSpec(gq.shape, const),
            pl.BlockSpec(gkv.shape, const),
            pl.BlockSpec(wuq.shape, const),
            pl.BlockSpec(wukvk.shape, const),
            pl.BlockSpec(wukvv.shape, const),
        ],
        out_specs=pl.BlockSpec((1, tm, Z_COLS), lambda t, b: (b, t, 0)),
        out_shape=jax.ShapeDtypeStruct((nb, n, Z_COLS), BF16),
        compiler_params=_params("arbitrary", "arbitrary"),
        name="inproj",
    )(x3, mod, g, cos, sin, w, gq, gkv, wuq, wukvk, wukvv)


def _lane_mask(q, lo, width):
    lane = lax.broadcasted_iota(jnp.int32, q.shape, 1)
    keep = jnp.where(lane >= lo, lane, LANES) < lo + width
    return jnp.where(keep, q.astype(F32), 0.0).astype(BF16)


def _localmax_softmax(n_units, chunks, score_of, extra_of, pv_of, e_scr):
    outs = [None] * n_units
    mloc, mfin = {}, {}
    n_c = len(chunks)
    for t in range(n_units + 1):
        acc = None
        if t < n_units:
            mloc[t] = []
        for cc in range(n_c + 1):
            if t < n_units and cc < n_c:
                off, rows = chunks[cc]
                s = score_of(t, cc)
                mc = jnp.max(s, axis=0, keepdims=True)
                e_scr[t % 2, off:off + rows, :] = jnp.exp2(s - mc).astype(BF16)
                mloc[t].append(mc)
            if t >= 1 and cc >= 1:
                u, c = t - 1, cc - 1
                off, rows = chunks[c]
                pv = pv_of(u, c, e_scr[u % 2, off:off + rows, :]) * jnp.exp2(mloc[u][c] - mfin[u])
                acc = pv if acc is None else acc + pv
        if t < n_units:
            m = mloc[t][0]
            for mc in mloc[t][1:]:
                m = jnp.maximum(m, mc)
            x = extra_of(t)
            mfin[t] = m if x is None else jnp.maximum(m, x)
        if t >= 1:
            l = acc[HEAD_DIM:HEAD_DIM + 1]
            x = extra_of(t - 1)
            if x is not None:
                l = l + jnp.exp2(x - mfin[t - 1])
            outs[t - 1] = acc[0:HEAD_DIM] * (1.0 / l)
    return outs


def _localmax_batched(n_units, chunks, score_of, extra_of, pv_of, e_scr):
    mloc, lloc, mfin = [], [], []
    for u in range(n_units):
        ms, ls = [], []
        for c, (off, rows) in enumerate(chunks):
            s = score_of(u, c)
            mc = jnp.max(s, axis=0, keepdims=True)
            e = jnp.exp2(s - mc)
            ls.append(jnp.sum(e, axis=0, keepdims=True))
            e_scr[u, off:off + rows, :] = e.astype(BF16)
            ms.append(mc)
        m = ms[0]
        for mc in ms[1:]:
            m = jnp.maximum(m, mc)
        x = extra_of(u)
        mloc.append(ms); lloc.append(ls); mfin.append(m if x is None else jnp.maximum(m, x))
    outs = []
    for u in range(n_units):
        acc = l = None
        for c, (off, rows) in enumerate(chunks):
            alpha = jnp.exp2(mloc[u][c] - mfin[u])
            pv = pv_of(u, c, e_scr[u, off:off + rows, :]) * alpha
            acc = pv if acc is None else acc + pv
            l = lloc[u][c] * alpha if l is None else l + lloc[u][c] * alpha
        x = extra_of(u)
        if x is not None:
            l = l + jnp.exp2(x - mfin[u])
        outs.append(acc * (1.0 / l))
    return outs


def _head_rows(o, h):
    return o[(h % 2) * HEAD_DIM:(h % 2 + 1) * HEAD_DIM]


def _diff_lambda(lp, lam_init):
    a = jnp.sum(lp[0:1] * lp[1:2], axis=-1, keepdims=True)
    b = jnp.sum(lp[2:3] * lp[3:4], axis=-1, keepdims=True)
    return jnp.exp(a) - jnp.exp(b) + lam_init


def _diff_combine(outs, lam):
    res = []
    for h in range(HEADS):
        o = outs[2 * h] - lam * outs[2 * h + 1]
        ms = jnp.sum(o * o, axis=0, keepdims=True) * (1.0 / HEAD_DIM)
        res.append(o * lax.rsqrt(ms + EPS))
    return jnp.concatenate(res, axis=0)


def _key_chunks(total, size):
    return [(o, min(size, total - o)) for o in range(0, total, size)]


def _exp_scratch(nk, tq, slots=2):
    return [pltpu.VMEM((slots, nk, tq), BF16)]


VT_ROWS = HEAD_DIM + 16


def _transpose_values(vt_scr, srcs):
    for ref, off, n in srcs:
        for r in range(0, n, KEY_CHUNK):
            rows = min(KEY_CHUNK, n - r)
            vt = ref[0, r:r + rows, :].astype(F32).T
            for h in range(HEADS):
                vt_scr[h * VT_ROWS:h * VT_ROWS + HEAD_DIM, off + r:off + r + rows] = (
                    vt[h * HEAD_DIM:(h + 1) * HEAD_DIM].astype(BF16))
                vt_scr[h * VT_ROWS + HEAD_DIM:(h + 1) * VT_ROWS, off + r:off + r + rows] = (
                    jnp.ones((VT_ROWS - HEAD_DIM, rows), BF16))


def _mla_kernel(q_ref, k_ref, v_ref, kc_ref, vc_ref, o_ref, e_scr, vt_scr):
    s_len, c_len = k_ref.shape[1], kc_ref.shape[1]

    @pl.when(pl.program_id(1) == 0)
    def _():
        _transpose_values(vt_scr, [(v_ref, 0, s_len), (vc_ref, s_len, c_len)])

    chunks = _key_chunks(s_len, KEY_CHUNK) + [(s_len + o, r) for o, r in _key_chunks(c_len, KEY_CHUNK)]
    units = [(qt, h) for qt in range(q_ref.shape[1] // UNIT_Q) for h in range(HEADS)]
    qs = [q_ref[0, qt * UNIT_Q:(qt + 1) * UNIT_Q, h * LANES:(h + 1) * LANES] for qt, h in units]

    def score_of(u, c):
        off, rows = chunks[c]
        h = units[u][1]
        sl = slice(h * LANES, (h + 1) * LANES)
        if off < s_len:
            return _dot_nt(k_ref[0, off:off + rows, sl], qs[u])
        return _dot_nt(kc_ref[0, off - s_len:off - s_len + rows, sl], qs[u])

    def pv_of(u, c, e):
        h = units[u][1]
        off, rows = chunks[c]
        return _dot(vt_scr[h * VT_ROWS:(h + 1) * VT_ROWS, off:off + rows], e)

    outs = _localmax_softmax(len(units), chunks, score_of, lambda u: None, pv_of, e_scr)
    for qt in range(len(units) // HEADS):
        o = jnp.concatenate(outs[qt * HEADS:(qt + 1) * HEADS], axis=0)
        o_ref[0, qt * UNIT_Q:(qt + 1) * UNIT_Q, :] = o.T.astype(BF16)


def _mla_call(zl, zc, tq):
    b, s, _ = zl.shape
    c = zc.shape[1]
    return pl.pallas_call(
        _mla_kernel,
        grid=(b, s // tq),
        in_specs=[
            pl.BlockSpec((1, tq, MLA_W), lambda i, j: (i, j, ZC_MQ // MLA_W)),
            pl.BlockSpec((1, s, MLA_W), lambda i, j: (i, 0, ZC_MK // MLA_W)),
            pl.BlockSpec((1, s, MIX_W), lambda i, j: (i, 0, ZC_MV // MIX_W)),
            pl.BlockSpec((1, c, MLA_W), lambda i, j: (i, 0, ZC_MK // MLA_W)),
            pl.BlockSpec((1, c, MIX_W), lambda i, j: (i, 0, ZC_MV // MIX_W)),
        ],
        out_specs=pl.BlockSpec((1, tq, MIX_W), lambda i, j: (i, j, 0)),
        out_shape=jax.ShapeDtypeStruct((b, s, MIX_W), BF16),
        scratch_shapes=_exp_scratch(s + c, UNIT_Q) + [pltpu.VMEM((HEADS * VT_ROWS, s + c), BF16)],
        compiler_params=_params("arbitrary", "arbitrary"),
        name="mla",
    )(zl, zl, zl, zc, zc)


def _diff_kernel(lam_init, q_ref, k_ref, v_ref, kc_ref, vc_ref, lp_ref, gain_ref, o_ref,
                 e_scr, vt_scr):
    s_len, c_len = k_ref.shape[1], kc_ref.shape[1]

    @pl.when(pl.program_id(1) == 0)
    def _():
        _transpose_values(vt_scr, [(v_ref, 0, s_len), (vc_ref, s_len, c_len)])

    chunks = _key_chunks(s_len, KEY_CHUNK) + [(s_len + o, r) for o, r in _key_chunks(c_len, KEY_CHUNK)]
    n_sub = 2 * HEADS
    units = [(qt, w) for qt in range(q_ref.shape[1] // UNIT_Q) for w in range(n_sub)]
    qs = [_lane_mask(q_ref[0, qt * UNIT_Q:(qt + 1) * UNIT_Q, (w // 4) * LANES:(w // 4 + 1) * LANES],
                     (w % 4) * DIFF_DH, DIFF_DH) for qt, w in units]

    def score_of(u, c):
        off, rows = chunks[c]
        w = units[u][1]
        sl = slice((w // 4) * LANES, (w // 4 + 1) * LANES)
        if off < s_len:
            return _dot_nt(k_ref[0, off:off + rows, sl], qs[u])
        return _dot_nt(kc_ref[0, off - s_len:off - s_len + rows, sl], qs[u])

    def pv_of(u, c, e):
        h = units[u][1] // 2
        off, rows = chunks[c]
        return _dot(vt_scr[h * VT_ROWS:(h + 1) * VT_ROWS, off:off + rows], e)

    outs = _localmax_softmax(len(units), chunks, score_of, lambda u: None, pv_of, e_scr)
    lam = _diff_lambda(lp_ref[...], lam_init)
    for qt in range(len(units) // n_sub):
        o = _diff_combine(outs[qt * n_sub:(qt + 1) * n_sub], lam)
        o_ref[0, qt * UNIT_Q:(qt + 1) * UNIT_Q, :] = (o.T * gain_ref[...]).astype(BF16)


def _diff_call(zl, zc, lp, gain, lam_init, tq):
    b, s, _ = zl.shape
    c = zc.shape[1]
    return pl.pallas_call(
        functools.partial(_diff_kernel, lam_init),
        grid=(b, s // tq),
        in_specs=[
            pl.BlockSpec((1, tq, MIX_W), lambda i, j: (i, j, ZC_DQ // MIX_W)),
            pl.BlockSpec((1, s, MIX_W), lambda i, j: (i, 0, ZC_DK // MIX_W)),
            pl.BlockSpec((1, s, MIX_W), lambda i, j: (i, 0, ZC_DV // MIX_W)),
            pl.BlockSpec((1, c, MIX_W), lambda i, j: (i, 0, ZC_DK // MIX_W)),
            pl.BlockSpec((1, c, MIX_W), lambda i, j: (i, 0, ZC_DV // MIX_W)),
            pl.BlockSpec(lp.shape, lambda i, j: (0, 0)),
            pl.BlockSpec(gain.shape, lambda i, j: (0, 0)),
        ],
        out_specs=pl.BlockSpec((1, tq, MIX_W), lambda i, j: (i, j, 0)),
        out_shape=jax.ShapeDtypeStruct((b, s, MIX_W), BF16),
        scratch_shapes=_exp_scratch(s + c, UNIT_Q) + [pltpu.VMEM((HEADS * VT_ROWS, s + c), BF16)],
        compiler_params=_params("arbitrary", "arbitrary"),
        name="diff",
    )(zl, zl, zl, zc, zc, lp, gain)


def _swa_kernel(q_ref, k_ref, v_ref, kc_ref, vc_ref, sink_ref, o_ref, e_scr, mask_scr):
    s_len, c_len = k_ref.shape[1], kc_ref.shape[1]
    n_qt = q_ref.shape[1] // UNIT_Q
    win = UNIT_Q + 2 * SWA_WINDOW
    chunks = _key_chunks(win, KEY_CHUNK) + [(win + o, r) for o, r in _key_chunks(c_len, KEY_CHUNK)]
    starts = []
    for qt in range(n_qt):
        q0 = (pl.program_id(1) * n_qt + qt) * UNIT_Q
        start = pl.multiple_of(jnp.clip(q0 - SWA_WINDOW, 0, s_len - win), SWA_WINDOW)
        kpos = start + lax.broadcasted_iota(jnp.int32, (win, UNIT_Q), 0)
        qpos = q0 + lax.broadcasted_iota(jnp.int32, (win, UNIT_Q), 1)
        mask_scr[qt] = jnp.where(jnp.abs(kpos - qpos) <= SWA_WINDOW, 0.0, NEG_INF)
        starts.append(start)
    units = [(qt, w) for qt in range(n_qt) for w in range(HEADS)]
    qs = [_lane_mask(q_ref[0, qt * UNIT_Q:(qt + 1) * UNIT_Q, (w // SWA_HKV) * LANES:(w // SWA_HKV + 1) * LANES],
                     (w % SWA_HKV) * HEAD_DIM, HEAD_DIM) for qt, w in units]
    head_of = lambda w: (w % SWA_HKV) * SWA_GROUP + w // SWA_HKV

    def score_of(u, c):
        off, rows = chunks[c]
        qt = units[u][0]
        if off < win:
            k = k_ref[0, pl.ds(starts[qt] + off, rows), :]
            return _dot_nt(k, qs[u]) + mask_scr[qt, off:off + rows, :]
        return _dot_nt(kc_ref[0, off - win:off - win + rows, :], qs[u])

    def extra_of(u):
        return jnp.full((1, 1), sink_ref[head_of(units[u][1])] * LOG2E, F32)

    def pv_of(u, c, e):
        qt, w = units[u]
        off, rows = chunks[c]
        if off < win:
            o = _dot_tn(v_ref[0, pl.ds(starts[qt] + off, rows), :], e)
        else:
            o = _dot_tn(vc_ref[0, off - win:off - win + rows, :], e)
        return o[(w % SWA_HKV) * HEAD_DIM:(w % SWA_HKV + 1) * HEAD_DIM]

    outs = _localmax_batched(len(units), chunks, score_of, extra_of, pv_of, e_scr)
    for qt in range(n_qt):
        by_head = {head_of(w): outs[qt * HEADS + w] for w in range(HEADS)}
        o = jnp.concatenate([by_head[h] for h in range(HEADS)], axis=0)
        o_ref[0, qt * UNIT_Q:(qt + 1) * UNIT_Q, :] = o.T.astype(BF16)


def _swa_call(zl, zc, sink, tq):
    b, s, _ = zl.shape
    c = zc.shape[1]
    win = UNIT_Q + 2 * SWA_WINDOW
    return pl.pallas_call(
        _swa_kernel,
        grid=(b, s // tq),
        in_specs=[
            pl.BlockSpec((1, tq, MIX_W), lambda i, j: (i, j, ZC_SQ // MIX_W)),
            pl.BlockSpec((1, s, LANES), lambda i, j: (i, 0, ZC_SK // LANES)),
            pl.BlockSpec((1, s, LANES), lambda i, j: (i, 0, ZC_SV // LANES)),
            pl.BlockSpec((1, c, LANES), lambda i, j: (i, 0, ZC_SK // LANES)),
            pl.BlockSpec((1, c, LANES), lambda i, j: (i, 0, ZC_SV // LANES)),
            pl.BlockSpec(memory_space=pltpu.SMEM),
        ],
        out_specs=pl.BlockSpec((1, tq, MIX_W), lambda i, j: (i, j, 0)),
        out_shape=jax.ShapeDtypeStruct((b, s, MIX_W), BF16),
        scratch_shapes=(_exp_scratch(win + c, UNIT_Q, HEADS * tq // UNIT_Q)
                        + [pltpu.VMEM((tq // UNIT_Q, win, UNIT_Q), F32)]),
        compiler_params=_params("arbitrary", "arbitrary"),
        name="swa",
    )(zl, zl, zl, zc, zc, sink)


def _na_kernel(rows_n, q_ref, k_ref, v_ref, kc_ref, vc_ref, bias_ref, o_ref, e_scr):
    c_len = kc_ref.shape[1]
    n_rb = q_ref.shape[1] // NA_TQ
    nrb = rows_n // 2
    chunks = _key_chunks(NA_NLOC, KEY_CHUNK) + [(NA_NLOC + o, r) for o, r in _key_chunks(c_len, KEY_CHUNK)]
    starts, tids = [], []
    for i in range(n_rb):
        rb = pl.program_id(1) * n_rb + i
        starts.append(pl.multiple_of(
            jnp.clip(2 * rb - NA_WIN_R // 2, 0, rows_n - NA_KROWS) * GRID_W, 2 * GRID_W))
        tids.append(jnp.minimum(rb, 2) + jnp.maximum(rb - (nrb - 3), 0))
    blk = lambda h: slice((h // 2) * LANES, (h // 2 + 1) * LANES)
    units = [(i, h) for i in range(n_rb) for h in range(HEADS)]
    qs = [_lane_mask(q_ref[0, i * NA_TQ:(i + 1) * NA_TQ, blk(h)], (h % 2) * HEAD_DIM, HEAD_DIM)
          for i, h in units]

    def score_of(u, c):
        off, rows = chunks[c]
        i, h = units[u]
        if off < NA_NLOC:
            k = k_ref[0, pl.ds(starts[i] + off, rows), blk(h)]
            return _dot_nt(k, qs[u]) + bias_ref[h, tids[i], off:off + rows, :]
        return _dot_nt(kc_ref[0, off - NA_NLOC:off - NA_NLOC + rows, blk(h)], qs[u])

    def pv_of(u, c, e):
        i, h = units[u]
        off, rows = chunks[c]
        if off < NA_NLOC:
            return _head_rows(_dot_tn(v_ref[0, pl.ds(starts[i] + off, rows), blk(h)], e), h)
        return _head_rows(_dot_tn(vc_ref[0, off - NA_NLOC:off - NA_NLOC + rows, blk(h)], e), h)

    outs = _localmax_batched(len(units), chunks, score_of, lambda u: None, pv_of, e_scr)
    for i in range(n_rb):
        o = jnp.concatenate(outs[i * HEADS:(i + 1) * HEADS], axis=0)
        o_ref[0, i * NA_TQ:(i + 1) * NA_TQ, :] = o.T.astype(BF16)


def _na_call(zl, zc, bias, tq):
    b, s, _ = zl.shape
    c = zc.shape[1]
    rows_n = s // GRID_W
    return pl.pallas_call(
        functools.partial(_na_kernel, rows_n),
        grid=(b, s // tq),
        in_specs=[
            pl.BlockSpec((1, tq, MIX_W), lambda i, j: (i, j, ZC_NQ // MIX_W)),
            pl.BlockSpec((1, s, MIX_W), lambda i, j: (i, 0, ZC_NK // MIX_W)),
            pl.BlockSpec((1, s, MIX_W), lambda i, j: (i, 0, ZC_NV // MIX_W)),
            pl.BlockSpec((1, c, MIX_W), lambda i, j: (i, 0, ZC_NK // MIX_W)),
            pl.BlockSpec((1, c, MIX_W), lambda i, j: (i, 0, ZC_NV // MIX_W)),
            pl.BlockSpec(bias.shape, lambda i, j: (0, 0, 0, 0), pipeline_mode=pl.Buffered(1)),
        ],
        out_specs=pl.BlockSpec((1, tq, MIX_W), lambda i, j: (i, j, 0)),
        out_shape=jax.ShapeDtypeStruct((b, s, MIX_W), BF16),
        scratch_shapes=_exp_scratch(NA_NLOC + c, NA_TQ, HEADS * tq // NA_TQ),
        compiler_params=_params("arbitrary", "arbitrary"),
        name="na",
    )(zl, zl, zl, zc, zc, bias)


def _ctx_kernel(lam_init, z_ref, lp_ref, gain_ref, sink_ref, oa_ref, ob_ref, oc_ref, od_ref,
                e_scr):
    c_len = z_ref.shape[1]

    def blk(col, j=0):
        return z_ref[0, :, col + j * LANES:col + (j + 1) * LANES]

    units = []
    for h in range(HEADS):
        units.append((_lane_mask(blk(ZC_NQ, h // 2), (h % 2) * HEAD_DIM, HEAD_DIM),
                      ZC_NK + (h // 2) * LANES, ZC_NV + (h // 2) * LANES, (h % 2) * HEAD_DIM, None))
    for h in range(HEADS):
        units.append((blk(ZC_MQ, h), ZC_MK + h * LANES, ZC_MV + (h // 2) * LANES,
                      (h % 2) * HEAD_DIM, None))
    for h in range(HEADS):
        hk, g = h // SWA_GROUP, h % SWA_GROUP
        units.append((_lane_mask(blk(ZC_SQ, g), hk * HEAD_DIM, HEAD_DIM), ZC_SK, ZC_SV,
                      hk * HEAD_DIM, h))
    for w in range(2 * HEADS):
        units.append((_lane_mask(blk(ZC_DQ, w // 4), (w % 4) * DIFF_DH, DIFF_DH),
                      ZC_DK + (w // 4) * LANES, ZC_DV + (w // 4) * LANES, ((w // 2) % 2) * HEAD_DIM, None))

    def score_of(u, c):
        q_m, kcol = units[u][0], units[u][1]
        return _dot_nt(z_ref[0, :, kcol:kcol + LANES], q_m)

    def extra_of(u):
        h = units[u][4]
        return None if h is None else jnp.full((1, 1), sink_ref[h] * LOG2E, F32)

    def pv_of(u, c, e):
        vcol, r0 = units[u][2], units[u][3]
        return _dot_tn(z_ref[0, :, vcol:vcol + LANES], e)[r0:r0 + HEAD_DIM]

    outs = _localmax_batched(len(units), [(0, c_len)], score_of, extra_of, pv_of, e_scr)
    for i, ref in enumerate((oa_ref, ob_ref, oc_ref)):
        ref[0] = jnp.concatenate(outs[i * HEADS:(i + 1) * HEADS], axis=0).T.astype(BF16)
    o = _diff_combine(outs[3 * HEADS:], _diff_lambda(lp_ref[...], lam_init))
    od_ref[0] = (o.T * gain_ref[...]).astype(BF16)


def _ctx_call(zc, lp, gain, sink, lam_init):
    b, c, _ = zc.shape
    out = jax.ShapeDtypeStruct((b, c, MIX_W), BF16)
    ospec = pl.BlockSpec((1, c, MIX_W), lambda i: (i, 0, 0))
    return pl.pallas_call(
        functools.partial(_ctx_kernel, lam_init),
        grid=(b,),
        in_specs=[
            pl.BlockSpec((1, c, Z_COLS), lambda i: (i, 0, 0)),
            pl.BlockSpec(lp.shape, lambda i: (0, 0)),
            pl.BlockSpec(gain.shape, lambda i: (0, 0)),
            pl.BlockSpec(memory_space=pltpu.SMEM),
        ],
        out_specs=[ospec] * 4,
        out_shape=[out] * 4,
        scratch_shapes=_exp_scratch(c, c, 5 * HEADS),
        compiler_params=_params("arbitrary"),
        name="ctx_attn",
    )(zc, lp, gain, sink)


def _mlp_kernel(final, x_ref, ma_ref, mb_ref, mc_ref, md_ref, mod_ref, g_ref, gf_ref,
                wout_ref, wup_ref, wdn_ref, o_ref):
    x = x_ref[0]
    mod = mod_ref[0]
    attn = None
    for i, m_ref in enumerate((ma_ref, mb_ref, mc_ref, md_ref)):
        part = _dot(m_ref[0], wout_ref[i * MIX_W:(i + 1) * MIX_W, :])
        attn = part if attn is None else attn + part
    x = x + mod[2:3] * attn
    h = x * _rms_scale(x, D_MODEL) * g_ref[...]
    h = (h * (1.0 + mod[4:5]) + mod[3:4]).astype(BF16)
    acc = None
    ck = FF_CHUNK
    for c in range(D_FF // ck):
        u = jnp.maximum(_dot(h, wup_ref[:, c * ck:(c + 1) * ck]), 0.0)
        part = _dot((u * u).astype(BF16), wdn_ref[c * ck:(c + 1) * ck, :])
        acc = part if acc is None else acc + part
    x = x + mod[5:6] * acc
    if final:
        x = x * _rms_scale(x, D_MODEL) * gf_ref[...]
    o_ref[0] = x


def _mlp_call(x3, mixes, mod, mod_row, g, gf, wout, wup, wdn, tm, final):
    nb, n, d = x3.shape
    const = lambda t, b: (0, 0)
    tok = lambda t, b: (b, t, 0)
    mod_map = (lambda t, b: (b, 0, 0)) if mod_row is None else (lambda t, b: (mod_row, 0, 0))
    single = pl.Buffered(1)
    return pl.pallas_call(
        functools.partial(_mlp_kernel, final),
        grid=(n // tm, nb),
        in_specs=[pl.BlockSpec((1, tm, d), tok)]
        + [pl.BlockSpec((1, tm, MIX_W), tok)] * 4
        + [
            pl.BlockSpec((1, 6, d), mod_map),
            pl.BlockSpec((1, d), const),
            pl.BlockSpec((1, d), const),
            pl.BlockSpec(wout.shape, const, pipeline_mode=single),
            pl.BlockSpec(wup.shape, const, pipeline_mode=single),
            pl.BlockSpec(wdn.shape, const, pipeline_mode=single),
        ],
        out_specs=pl.BlockSpec((1, tm, d), tok),
        out_shape=jax.ShapeDtypeStruct((nb, n, d), F32),
        compiler_params=_params("arbitrary", "arbitrary"),
        name="mlp",
    )(x3, *mixes, mod, g, gf, wout, wup, wdn)


def _rope_tables(s_len):
    t = np.arange(s_len)
    row, col = t // GRID_W, t % GRID_W

    def axial(dim):
        n_freq = dim // 4
        freqs = jnp.asarray(ROPE_BASE, F32) ** (-jnp.arange(n_freq, dtype=F32) / n_freq)
        ang = jnp.concatenate([jnp.asarray(row, F32)[:, None] * freqs,
                               jnp.asarray(col, F32)[:, None] * freqs], axis=-1)
        return jnp.cos(ang), jnp.sin(ang)

    def group(cs, n_groups, scale):
        cos, sin = cs
        c = jnp.concatenate([cos, cos], axis=-1) * scale
        s = jnp.concatenate([-sin, sin], axis=-1) * scale
        return jnp.tile(c, (1, n_groups)), jnp.tile(s, (1, n_groups))

    r64, r32 = axial(HEAD_DIM), axial(MLA_ROPE)
    ones = lambda n, v=1.0: jnp.full((s_len, n), v, F32)
    zeros = lambda n: jnp.zeros((s_len, n), F32)
    mla_scale = (MLA_NOPE + MLA_ROPE) ** -0.5 * LOG2E
    mq_c, mq_s = group(r32, 1, mla_scale)
    parts = [
        group(r64, 4, HEAD_DIM ** -0.5 * LOG2E), group(r64, 2, 1.0),
        group(r32, 8, DIFF_DH ** -0.5 * LOG2E), group(r32, 8, 1.0),
        (ones(MLA_Q_LORA), zeros(MLA_Q_LORA)), group(r32, 1, 1.0), (ones(32), zeros(32)),
    ] + [(jnp.concatenate([ones(MLA_NOPE, mla_scale), mq_c, ones(32, mla_scale)], axis=-1),
          jnp.concatenate([zeros(MLA_NOPE), mq_s, zeros(32)], axis=-1))] * HEADS
    cos = jnp.concatenate([p[0] for p in parts], axis=-1)
    sin = jnp.concatenate([p[1] for p in parts], axis=-1)
    return cos, sin, cos[0:1]


def _permute_w_in(w_in):
    offs = np.cumsum([0, 256, 256, 256, MLA_Q_LORA, MLA_KV_LORA, MLA_ROPE, 256, 128, 128, 256, 256, 256])
    sec = lambda i: w_in[..., offs[i]:offs[i + 1]]
    sq = sec(6)
    sq = sq.reshape(sq.shape[:-1] + (SWA_HKV, SWA_GROUP, HEAD_DIM))
    sq = jnp.swapaxes(sq, -3, -2).reshape(w_in.shape[:-1] + (256,))
    pad = jnp.zeros(w_in.shape[:-1] + (32,), w_in.dtype)
    cols = [sq, sec(7), sec(9), sec(10), sec(4), sec(3), sec(5), pad,
            sec(0), sec(1), sec(2), sec(11), sec(8)]
    return jnp.concatenate(cols, axis=-1).astype(BF16)


def _mla_weights(w_uq, w_ukv):
    depth = w_uq.shape[0]
    uq = w_uq.reshape(depth, MLA_Q_LORA, HEADS, MLA_NOPE + MLA_ROPE)
    uq = jnp.pad(uq, ((0, 0), (0, 256 - MLA_Q_LORA), (0, 0), (0, LANES - MLA_NOPE - MLA_ROPE)))
    uq = uq.reshape(depth, 256, HEADS * LANES).astype(BF16)
    ukv = w_ukv.reshape(depth, MLA_KV_LORA, HEADS, 2, HEAD_DIM)
    uk = jnp.pad(ukv[:, :, :, 0], ((0, 0), (0, 0), (0, 0), (0, LANES - MLA_NOPE)))
    uk = uk.reshape(depth, MLA_KV_LORA, HEADS * LANES).astype(BF16)
    uv = ukv[:, :, :, 1].reshape(depth, MLA_KV_LORA, HEADS * HEAD_DIM).astype(BF16)
    return uq, uk, uv


def _na_bias_tables(rpb, rows_n):
    kr_n = min(NA_WIN_R, rows_n)
    col = np.arange(GRID_W)
    c0 = np.clip(col - NA_WIN_C // 2, 0, GRID_W - NA_WIN_C)
    col_ok_t = ((col[None, :] >= c0[:, None]) & (col[None, :] < c0[:, None] + NA_WIN_C)).T
    dc_idx_t = np.clip(col[:, None] - col[None, :], 1 - NA_WIN_C, NA_WIN_C - 1) + (NA_WIN_C - 1)
    toe = jnp.zeros(rpb.shape[:3] + (GRID_W, GRID_W), F32)
    for d in range(2 * NA_WIN_C - 1):
        toe = jnp.where(dc_idx_t == d, rpb[..., d, None, None].astype(F32), toe)
    toe = jnp.where(col_ok_t, toe * LOG2E, NEG_INF)
    neg = jnp.full(toe.shape[:2] + (GRID_W, GRID_W), NEG_INF, F32)
    nrb = rows_n // 2
    pairs = {}

    def pair(drs):
        if drs not in pairs:
            pairs[drs] = jnp.concatenate([neg if d is None else toe[:, :, d] for d in drs], axis=-1)
        return pairs[drs]

    blocks = []
    for rb in (0, 1, 2, nrb - 2, nrb - 1):
        start = int(np.clip(2 * rb - NA_WIN_R // 2, 0, rows_n - NA_KROWS))
        for j in range(NA_KROWS):
            drs = []
            for i in range(2):
                kr, qr = start + j, 2 * rb + i
                r0 = int(np.clip(qr - kr_n // 2, 0, rows_n - kr_n))
                drs.append(kr - qr + NA_WIN_R - 1 if r0 <= kr < r0 + kr_n else None)
            blocks.append(pair(tuple(drs)))
    tab = jnp.stack(blocks, axis=2)
    return tab.reshape(tab.shape[:2] + (5, NA_NLOC, NA_TQ))


def kernel(x, c, ctx, c_ctx, w_ada, b_ada, norm_attn_g, w_in, na_rpb, mla_q_norm_g, mla_w_uq,
           mla_kv_norm_g, mla_w_ukv, swa_sink, diff_lambda, diff_norm_g, w_out, norm_mlp_g,
           w_up, w_down, final_norm_g):
    b, s, d = x.shape
    c_len = ctx.shape[1]
    depth = w_in.shape[0]
    rows_n = s // GRID_W
    assert d == D_MODEL and s % max(ATTN_TQ, DIFF_TQ, TOKEN_TILE) == 0 and rows_n >= NA_KROWS
    assert c_len % KEY_CHUNK == 0 and (b * c_len) % min(TOKEN_TILE, b * c_len) == 0

    w_in_p = _permute_w_in(w_in)
    uq, uk, uv = _mla_weights(mla_w_uq, mla_w_ukv)
    gq = jnp.pad(mla_q_norm_g, ((0, 0), (0, 256 - MLA_Q_LORA)))[:, None, :]
    gkv = mla_kv_norm_g[:, None, :]
    w_out_b, w_up_b, w_dn_b = w_out.astype(BF16), w_up.astype(BF16), w_down.astype(BF16)
    na_bias = _na_bias_tables(na_rpb, rows_n)
    cos, sin, cos_ctx = _rope_tables(s)
    tm_l, tm_c = TOKEN_TILE, min(TOKEN_TILE, b * c_len)
    cos_c = jnp.broadcast_to(cos_ctx, (tm_c, T_COLS))
    sin_c = jnp.zeros((tm_c, T_COLS), F32)

    c_rows = -(-(b + 1) // 8) * 8
    c_all = jnp.zeros((c_rows, d), F32).at[:b].set(c).at[b].set(c_ctx)
    mod = _ada_call(c_all, w_ada, b_ada).reshape(depth, c_rows, 6, d)

    xl = x
    xc = ctx.reshape(1, b * c_len, d)
    for l in range(depth):
        need_ctx = l < depth - 1
        lam_init = 0.8 - 0.6 * math.exp(-0.3 * l)
        g_attn = norm_attn_g[l][None, :]
        g_mlp = norm_mlp_g[l][None, :]
        gf = final_norm_g[None, :]
        d_gain = (jnp.tile(diff_norm_g[l], HEADS) * (1.0 - lam_init))[None, :]
        proj = (w_in_p[l], gq[l], gkv[l], uq[l], uk[l], uv[l])
        zl = _inproj_call(xl, mod[l], None, g_attn, cos, sin, True, *proj, tm=tm_l)
        zc = _inproj_call(xc, mod[l], b, g_attn, cos_c, sin_c, False, *proj, tm=tm_c)
        zc = zc.reshape(b, c_len, Z_COLS)
        mixes = (
            _na_call(zl, zc, na_bias[l], tq=ATTN_TQ),
            _mla_call(zl, zc, tq=ATTN_TQ),
            _swa_call(zl, zc, swa_sink[l], tq=ATTN_TQ),
            _diff_call(zl, zc, diff_lambda[l], d_gain, lam_init, tq=DIFF_TQ),
        )
        wts = (w_out_b[l], w_up_b[l], w_dn_b[l])
        xl = _mlp_call(xl, mixes, mod[l], None, g_mlp, gf, *wts, tm=tm_l, final=not need_ctx)
        if need_ctx:
            mixes_c = _ctx_call(zc, diff_lambda[l], d_gain, swa_sink[l], lam_init)
            mixes_c = [m.reshape(1, b * c_len, MIX_W) for m in mixes_c]
            xc = _mlp_call(xc, mixes_c, mod[l], b, g_mlp, gf, *wts, tm=tm_c, final=False)
    return xl
```

```python
import functools
import math

import jax
import jax.numpy as jnp
import numpy as np
from jax import lax
from jax.experimental import pallas as pl
from jax.experimental.pallas import tpu as pltpu

F32 = jnp.float32
BF16 = jnp.bfloat16

D_MODEL = 1024
GRID_W = 64
HEADS = 4
HEAD_DIM = 64
NA_WIN_R = 8
NA_WIN_C = 16
MLA_Q_LORA = 192
MLA_KV_LORA = 128
MLA_NOPE = 64
MLA_ROPE = 32
SWA_HKV = 2
SWA_GROUP = 2
SWA_WINDOW = 128
DIFF_DH = 32
D_FF = 4 * D_MODEL
ROPE_BASE = 10000.0
EPS = 1e-6
NEG_INF = -1e30
LOG2E = math.log2(math.e)

LANES = 128
VMEM_LIMIT = 56 * 1024 * 1024

ZC_DQ, ZC_DK, ZC_SQ, ZC_NQ, ZC_NK, ZC_NV, ZC_DV, ZC_MV = 0, 256, 512, 768, 1024, 1280, 1536, 1792
ZC_MQ, ZC_MK, ZC_SK, ZC_SV = 2048, 2560, 3072, 3200
Z_COLS = 3328
W_R64, W_R32, W_M, W_N, W_COLS = 0, 384, 896, 1280, 2432
T_R64, T_R32, T_B12, T_MQ, T_COLS = 0, 384, 896, 1152, 1664

NA_KROWS = 10
NA_TQ = 2 * GRID_W
NA_NLOC = NA_KROWS * GRID_W
KEY_CHUNK = 256
WINDOW_CHUNK = 640
UNIT_Q = 256
MIX_W = HEADS * HEAD_DIM
MLA_W = HEADS * LANES

TOKEN_TILE = 512
ATTN_TQ = 1024
DIFF_TQ = 1024
ADA_TN = 1536
FF_CHUNK = 1024


def _params(*sem):
    return pltpu.CompilerParams(dimension_semantics=sem, vmem_limit_bytes=VMEM_LIMIT)


def _dot(a, b):
    return jnp.dot(a, b, preferred_element_type=F32)


def _dot_nt(a, b):
    return lax.dot_general(a, b, (((1,), (1,)), ((), ())), preferred_element_type=F32)


def _dot_tn(a, b):
    return lax.dot_general(a, b, (((0,), (0,)), ((), ())), preferred_element_type=F32)


def _rms_scale(x, n):
    return lax.rsqrt(jnp.sum(x * x, axis=-1, keepdims=True) * (1.0 / n) + EPS)


def _ada_kernel(c_ref, w_ref, b_ref, o_ref):
    c = c_ref[...]
    act = (c * (1.0 / (1.0 + jnp.exp(-c)))).astype(BF16)
    o_ref[0] = _dot(act, w_ref[0].astype(BF16)) + b_ref[0]


def _ada_call(c_all, w_ada, b_ada):
    depth, d, n = w_ada.shape
    rows = c_all.shape[0]
    tn = ADA_TN
    return pl.pallas_call(
        _ada_kernel,
        grid=(depth, n // tn),
        in_specs=[
            pl.BlockSpec((rows, d), lambda l, j: (0, 0)),
            pl.BlockSpec((1, d, tn), lambda l, j: (l, 0, j)),
            pl.BlockSpec((1, 1, tn), lambda l, j: (l, 0, j)),
        ],
        out_specs=pl.BlockSpec((1, rows, tn), lambda l, j: (l, 0, j)),
        out_shape=jax.ShapeDtypeStruct((depth, rows, n), F32),
        compiler_params=_params("arbitrary", "arbitrary"),
        name="ada",
    )(c_all, w_ada, b_ada.reshape(depth, 1, n))


def _rope_blocks(z, cos_ref, sin_ref, tcol, half):
    tm = z.shape[0]
    lane = lax.broadcasted_iota(jnp.int32, (tm, LANES), 1)
    first = (lane % (2 * half)) < half
    out = []
    for j in range(z.shape[1] // LANES):
        xb = z[:, j * LANES:(j + 1) * LANES]
        rot = jnp.where(first, pltpu.roll(xb, LANES - half, 1), pltpu.roll(xb, half, 1))
        c0 = tcol + j * LANES
        out.append(xb * cos_ref[:, c0:c0 + LANES] + rot * sin_ref[:, c0:c0 + LANES])
    return out


def _inproj_kernel(x_ref, mod_ref, g_ref, cos_ref, sin_ref, w_ref, gq_ref, gkv_ref,
                   wuq_ref, wukvk_ref, wukvv_ref, o_ref):
    x = x_ref[0]
    mod = mod_ref[0]
    h = x * _rms_scale(x, D_MODEL) * g_ref[...]
    h = (h * (1.0 + mod[1:2]) + mod[0:1]).astype(BF16)

    def put(col, val):
        o_ref[0, :, col:col + val.shape[1]] = val.astype(BF16)

    z64 = _dot(h, w_ref[:, W_R64:W_R32])
    z32 = _dot(h, w_ref[:, W_R32:W_M])
    r = _rope_blocks(z64, cos_ref, sin_ref, T_R64, HEAD_DIM // 2)
    put(ZC_SQ, r[0]); put(ZC_SQ + LANES, r[1]); put(ZC_SK, r[2])
    zm = _dot(h, w_ref[:, W_M:W_N])
    r = _rope_blocks(z32, cos_ref, sin_ref, T_R32, DIFF_DH // 2)
    for j in range(4):
        put(ZC_DQ + j * LANES, r[j])
    zn = _dot(h, w_ref[:, W_N:W_COLS])

    ckv = zm[:, 0:MLA_KV_LORA]
    b12 = zm[:, MLA_KV_LORA:]
    ckv_n = (ckv * _rms_scale(ckv, MLA_KV_LORA) * gkv_ref[...]).astype(BF16)
    lane = lax.broadcasted_iota(jnp.int32, b12.shape, 1)
    cq = jnp.where(lane < MLA_Q_LORA, b12, 0.0)
    cq_n = (cq * _rms_scale(cq, MLA_Q_LORA) * gq_ref[...]).astype(BF16)
    zq = _dot(cq_n, wuq_ref[...])
    put(ZC_NQ, zn[:, 0:256] * (HEAD_DIM ** -0.5 * LOG2E))
    put(ZC_NK, zn[:, 256:1024])
    put(ZC_SV, zn[:, 1024:1152])
    q = _rope_blocks(zq, cos_ref, sin_ref, T_MQ, MLA_ROPE // 2)
    for j in range(HEADS):
        put(ZC_MQ + j * LANES, q[j])
    kr = _rope_blocks(b12[:, LANES:], cos_ref, sin_ref, T_B12 + LANES, MLA_ROPE // 2)[0]
    lane = lax.broadcasted_iota(jnp.int32, kr.shape, 1)
    kr = jnp.where(jnp.where(lane >= MLA_NOPE, lane, LANES) < MLA_NOPE + MLA_ROPE, kr, 0.0)
    kn = _dot(ckv_n, wukvk_ref[...])
    for j in range(HEADS):
        put(ZC_MK + j * LANES, kn[:, j * LANES:(j + 1) * LANES] + kr)
    put(ZC_MV, _dot(ckv_n, wukvv_ref[...]))


def _inproj_call(x3, mod, mod_row, g, cos, sin, table_per_tile, w, gq, gkv, wuq, wukvk, wukvv, tm):
    nb, n, d = x3.shape
    nt = n // tm
    const = lambda t, b: (0, 0)
    tab_map = (lambda t, b: (t, 0)) if table_per_tile else const
    mod_map = (lambda t, b: (b, 0, 0)) if mod_row is None else (lambda t, b: (mod_row, 0, 0))
    return pl.pallas_call(
        _inproj_kernel,
        grid=(nt, nb),
        in_specs=[
            pl.BlockSpec((1, tm, d), lambda t, b: (b, t, 0)),
            pl.BlockSpec((1, 6, d), mod_map),
            pl.BlockSpec((1, d), const),
            pl.BlockSpec((tm, T_COLS), tab_map),
            pl.BlockSpec((tm, T_COLS), tab_map),
            pl.BlockSpec(w.shape, const, pipeline_mode=pl.Buffered(1)),
            pl.BlockSpec(gq.shape, const),
            pl.BlockSpec(gkv.shape, const),
            pl.BlockSpec(wuq.shape, const),
            pl.BlockSpec(wukvk.shape, const),
            pl.BlockSpec(wukvv.shape, const),
        ],
        out_specs=pl.BlockSpec((1, tm, Z_COLS), lambda t, b: (b, t, 0)),
        out_shape=jax.ShapeDtypeStruct((nb, n, Z_COLS), BF16),
        compiler_params=_params("arbitrary", "arbitrary"),
        name="inproj",
    )(x3, mod, g, cos, sin, w, gq, gkv, wuq, wukvk, wukvv)


def _lane_mask(q, lo, width):
    lane = lax.broadcasted_iota(jnp.int32, q.shape, 1)
    keep = jnp.where(lane >= lo, lane, LANES) < lo + width
    return jnp.where(keep, q.astype(F32), 0.0).astype(BF16)


def _localmax_softmax(n_units, chunks, score_of, extra_of, pv_of, e_scr):
    outs = [None] * n_units
    mloc, mfin = {}, {}
    n_c = len(chunks)
    for t in range(n_units + 1):
        acc = None
        if t < n_units:
            mloc[t] = []
        for cc in range(n_c + 1):
            if t < n_units and cc < n_c:
                off, rows = chunks[cc]
                s = score_of(t, cc)
                mc = jnp.max(s, axis=0, keepdims=True)
                e_scr[t % 2, off:off + rows, :] = jnp.exp2(s - mc).astype(BF16)
                mloc[t].append(mc)
            if t >= 1 and cc >= 1:
                u, c = t - 1, cc - 1
                off, rows = chunks[c]
                pv = pv_of(u, c, e_scr[u % 2, off:off + rows, :]) * jnp.exp2(mloc[u][c] - mfin[u])
                acc = pv if acc is None else acc + pv
        if t < n_units:
            m = mloc[t][0]
            for mc in mloc[t][1:]:
                m = jnp.maximum(m, mc)
            x = extra_of(t)
            mfin[t] = m if x is None else jnp.maximum(m, x)
        if t >= 1:
            l = acc[HEAD_DIM:HEAD_DIM + 1]
            x = extra_of(t - 1)
            if x is not None:
                l = l + jnp.exp2(x - mfin[t - 1])
            outs[t - 1] = acc[0:HEAD_DIM] * (1.0 / l)
    return outs


def _localmax_batched(n_units, chunks, score_of, extra_of, pv_of, e_scr):
    mloc, lloc, mfin = [], [], []
    for u in range(n_units):
        ms, ls = [], []
        for c, (off, rows) in enumerate(chunks):
            s = score_of(u, c)
            mc = jnp.max(s, axis=0, keepdims=True)
            e = jnp.exp2(s - mc)
            ls.append(jnp.sum(e, axis=0, keepdims=True))
            e_scr[u, off:off + rows, :] = e.astype(BF16)
            ms.append(mc)
        m = ms[0]
        for mc in ms[1:]:
            m = jnp.maximum(m, mc)
        x = extra_of(u)
        mloc.append(ms); lloc.append(ls); mfin.append(m if x is None else jnp.maximum(m, x))
    outs = []
    for u in range(n_units):
        acc = l = None
        for c, (off, rows) in enumerate(chunks):
            alpha = jnp.exp2(mloc[u][c] - mfin[u])
            pv = pv_of(u, c, e_scr[u, off:off + rows, :]) * alpha
            acc = pv if acc is None else acc + pv
            l = lloc[u][c] * alpha if l is None else l + lloc[u][c] * alpha
        x = extra_of(u)
        if x is not None:
            l = l + jnp.exp2(x - mfin[u])
        outs.append(acc * (1.0 / l))
    return outs


def _head_rows(o, h):
    return o[(h % 2) * HEAD_DIM:(h % 2 + 1) * HEAD_DIM]


def _diff_lambda(lp, lam_init):
    a = jnp.sum(lp[0:1] * lp[1:2], axis=-1, keepdims=True)
    b = jnp.sum(lp[2:3] * lp[3:4], axis=-1, keepdims=True)
    return jnp.exp(a) - jnp.exp(b) + lam_init


def _diff_combine(outs, lam):
    res = []
    for h in range(HEADS):
        o = outs[2 * h] - lam * outs[2 * h + 1]
        ms = jnp.sum(o * o, axis=0, keepdims=True) * (1.0 / HEAD_DIM)
        res.append(o * lax.rsqrt(ms + EPS))
    return jnp.concatenate(res, axis=0)


def _key_chunks(total, size):
    return [(o, min(size, total - o)) for o in range(0, total, size)]


def _exp_scratch(nk, tq, slots=2):
    return [pltpu.VMEM((slots, nk, tq), BF16)]


VT_ROWS = HEAD_DIM + 16


def _transpose_values(vt_scr, srcs):
    for ref, off, n in srcs:
        for r in range(0, n, KEY_CHUNK):
            rows = min(KEY_CHUNK, n - r)
            vt = ref[0, r:r + rows, :].astype(F32).T
            for h in range(HEADS):
                vt_scr[h * VT_ROWS:h * VT_ROWS + HEAD_DIM, off + r:off + r + rows] = (
                    vt[h * HEAD_DIM:(h + 1) * HEAD_DIM].astype(BF16))
                vt_scr[h * VT_ROWS + HEAD_DIM:(h + 1) * VT_ROWS, off + r:off + r + rows] = (
                    jnp.ones((VT_ROWS - HEAD_DIM, rows), BF16))


def _mla_kernel(q_ref, k_ref, v_ref, kc_ref, vc_ref, o_ref, e_scr, vt_scr):
    s_len, c_len = k_ref.shape[1], kc_ref.shape[1]

    @pl.when(pl.program_id(1) == 0)
    def _():
        _transpose_values(vt_scr, [(v_ref, 0, s_len), (vc_ref, s_len, c_len)])

    chunks = _key_chunks(s_len, KEY_CHUNK) + [(s_len + o, r) for o, r in _key_chunks(c_len, KEY_CHUNK)]
    units = [(qt, h) for qt in range(q_ref.shape[1] // UNIT_Q) for h in range(HEADS)]
    qs = [q_ref[0, qt * UNIT_Q:(qt + 1) * UNIT_Q, h * LANES:(h + 1) * LANES] for qt, h in units]

    def score_of(u, c):
        off, rows = chunks[c]
        h = units[u][1]
        sl = slice(h * LANES, (h + 1) * LANES)
        if off < s_len:
            return _dot_nt(k_ref[0, off:off + rows, sl], qs[u])
        return _dot_nt(kc_ref[0, off - s_len:off - s_len + rows, sl], qs[u])

    def pv_of(u, c, e):
        h = units[u][1]
        off, rows = chunks[c]
        return _dot(vt_scr[h * VT_ROWS:(h + 1) * VT_ROWS, off:off + rows], e)

    outs = _localmax_softmax(len(units), chunks, score_of, lambda u: None, pv_of, e_scr)
    for qt in range(len(units) // HEADS):
        o = jnp.concatenate(outs[qt * HEADS:(qt + 1) * HEADS], axis=0)
        o_ref[0, qt * UNIT_Q:(qt + 1) * UNIT_Q, :] = o.T.astype(BF16)


def _mla_call(zl, zc, tq):
    b, s, _ = zl.shape
    c = zc.shape[1]
    return pl.pallas_call(
        _mla_kernel,
        grid=(b, s // tq),
        in_specs=[
            pl.BlockSpec((1, tq, MLA_W), lambda i, j: (i, j, ZC_MQ // MLA_W)),
            pl.BlockSpec((1, s, MLA_W), lambda i, j: (i, 0, ZC_MK // MLA_W)),
            pl.BlockSpec((1, s, MIX_W), lambda i, j: (i, 0, ZC_MV // MIX_W)),
            pl.BlockSpec((1, c, MLA_W), lambda i, j: (i, 0, ZC_MK // MLA_W)),
            pl.BlockSpec((1, c, MIX_W), lambda i, j: (i, 0, ZC_MV // MIX_W)),
        ],
        out_specs=pl.BlockSpec((1, tq, MIX_W), lambda i, j: (i, j, 0)),
        out_shape=jax.ShapeDtypeStruct((b, s, MIX_W), BF16),
        scratch_shapes=_exp_scratch(s + c, UNIT_Q) + [pltpu.VMEM((HEADS * VT_ROWS, s + c), BF16)],
        compiler_params=_params("arbitrary", "arbitrary"),
        name="mla",
    )(zl, zl, zl, zc, zc)


def _diff_kernel(lam_init, q_ref, k_ref, v_ref, kc_ref, vc_ref, lp_ref, gain_ref, o_ref,
                 e_scr, vt_scr):
    s_len, c_len = k_ref.shape[1], kc_ref.shape[1]

    @pl.when(pl.program_id(1) == 0)
    def _():
        _transpose_values(vt_scr, [(v_ref, 0, s_len), (vc_ref, s_len, c_len)])

    chunks = _key_chunks(s_len, KEY_CHUNK) + [(s_len + o, r) for o, r in _key_chunks(c_len, KEY_CHUNK)]
    n_sub = 2 * HEADS
    units = [(qt, w) for qt in range(q_ref.shape[1] // UNIT_Q) for w in range(n_sub)]
    qs = [_lane_mask(q_ref[0, qt * UNIT_Q:(qt + 1) * UNIT_Q, (w // 4) * LANES:(w // 4 + 1) * LANES],
                     (w % 4) * DIFF_DH, DIFF_DH) for qt, w in units]

    def score_of(u, c):
        off, rows = chunks[c]
        w = units[u][1]
        sl = slice((w // 4) * LANES, (w // 4 + 1) * LANES)
        if off < s_len:
            return _dot_nt(k_ref[0, off:off + rows, sl], qs[u])
        return _dot_nt(kc_ref[0, off - s_len:off - s_len + rows, sl], qs[u])

    def pv_of(u, c, e):
        h = units[u][1] // 2
        off, rows = chunks[c]
        return _dot(vt_scr[h * VT_ROWS:(h + 1) * VT_ROWS, off:off + rows], e)

    outs = _localmax_softmax(len(units), chunks, score_of, lambda u: None, pv_of, e_scr)
    lam = _diff_lambda(lp_ref[...], lam_init)
    for qt in range(len(units) // n_sub):
        o = _diff_combine(outs[qt * n_sub:(qt + 1) * n_sub], lam)
        o_ref[0, qt * UNIT_Q:(qt + 1) * UNIT_Q, :] = (o.T * gain_ref[...]).astype(BF16)


def _diff_call(zl, zc, lp, gain, lam_init, tq):
    b, s, _ = zl.shape
    c = zc.shape[1]
    return pl.pallas_call(
        functools.partial(_diff_kernel, lam_init),
        grid=(b, s // tq),
        in_specs=[
            pl.BlockSpec((1, tq, MIX_W), lambda i, j: (i, j, ZC_DQ // MIX_W)),
            pl.BlockSpec((1, s, MIX_W), lambda i, j: (i, 0, ZC_DK // MIX_W)),
            pl.BlockSpec((1, s, MIX_W), lambda i, j: (i, 0, ZC_DV // MIX_W)),
            pl.BlockSpec((1, c, MIX_W), lambda i, j: (i, 0, ZC_DK // MIX_W)),
            pl.BlockSpec((1, c, MIX_W), lambda i, j: (i, 0, ZC_DV // MIX_W)),
            pl.BlockSpec(lp.shape, lambda i, j: (0, 0)),
            pl.BlockSpec(gain.shape, lambda i, j: (0, 0)),
        ],
        out_specs=pl.BlockSpec((1, tq, MIX_W), lambda i, j: (i, j, 0)),
        out_shape=jax.ShapeDtypeStruct((b, s, MIX_W), BF16),
        scratch_shapes=_exp_scratch(s + c, UNIT_Q) + [pltpu.VMEM((HEADS * VT_ROWS, s + c), BF16)],
        compiler_params=_params("arbitrary", "arbitrary"),
        name="diff",
    )(zl, zl, zl, zc, zc, lp, gain)


def _swa_kernel(q_ref, k_ref, v_ref, kc_ref, vc_ref, sink_ref, o_ref, e_scr, mask_scr):
    s_len, c_len = k_ref.shape[1], kc_ref.shape[1]
    n_qt = q_ref.shape[1] // UNIT_Q
    win = UNIT_Q + 2 * SWA_WINDOW
    chunks = _key_chunks(win, WINDOW_CHUNK) + [(win + o, r) for o, r in _key_chunks(c_len, WINDOW_CHUNK)]
    starts = []
    for qt in range(n_qt):
        q0 = (pl.program_id(1) * n_qt + qt) * UNIT_Q
        start = pl.multiple_of(jnp.clip(q0 - SWA_WINDOW, 0, s_len - win), SWA_WINDOW)
        kpos = start + lax.broadcasted_iota(jnp.int32, (win, UNIT_Q), 0)
        qpos = q0 + lax.broadcasted_iota(jnp.int32, (win, UNIT_Q), 1)
        mask_scr[qt] = jnp.where(jnp.abs(kpos - qpos) <= SWA_WINDOW, 0.0, NEG_INF)
        starts.append(start)
    units = [(qt, w) for qt in range(n_qt) for w in range(HEADS)]
    qs = [_lane_mask(q_ref[0, qt * UNIT_Q:(qt + 1) * UNIT_Q, (w // SWA_HKV) * LANES:(w // SWA_HKV + 1) * LANES],
                     (w % SWA_HKV) * HEAD_DIM, HEAD_DIM) for qt, w in units]
    head_of = lambda w: (w % SWA_HKV) * SWA_GROUP + w // SWA_HKV

    def score_of(u, c):
        off, rows = chunks[c]
        qt = units[u][0]
        if off < win:
            k = k_ref[0, pl.ds(starts[qt] + off, rows), :]
            return _dot_nt(k, qs[u]) + mask_scr[qt, off:off + rows, :]
        return _dot_nt(kc_ref[0, off - win:off - win + rows, :], qs[u])

    def extra_of(u):
        return jnp.full((1, 1), sink_ref[head_of(units[u][1])] * LOG2E, F32)

    def pv_of(u, c, e):
        qt, w = units[u]
        off, rows = chunks[c]
        if off < win:
            o = _dot_tn(v_ref[0, pl.ds(starts[qt] + off, rows), :], e)
        else:
            o = _dot_tn(vc_ref[0, off - win:off - win + rows, :], e)
        return o[(w % SWA_HKV) * HEAD_DIM:(w % SWA_HKV + 1) * HEAD_DIM]

    outs = _localmax_batched(len(units), chunks, score_of, extra_of, pv_of, e_scr)
    for qt in range(n_qt):
        by_head = {head_of(w): outs[qt * HEADS + w] for w in range(HEADS)}
        o = jnp.concatenate([by_head[h] for h in range(HEADS)], axis=0)
        o_ref[0, qt * UNIT_Q:(qt + 1) * UNIT_Q, :] = o.T.astype(BF16)


def _swa_call(zl, zc, sink, tq):
    b, s, _ = zl.shape
    c = zc.shape[1]
    win = UNIT_Q + 2 * SWA_WINDOW
    return pl.pallas_call(
        _swa_kernel,
        grid=(b, s // tq),
        in_specs=[
            pl.BlockSpec((1, tq, MIX_W), lambda i, j: (i, j, ZC_SQ // MIX_W)),
            pl.BlockSpec((1, s, LANES), lambda i, j: (i, 0, ZC_SK // LANES)),
            pl.BlockSpec((1, s, LANES), lambda i, j: (i, 0, ZC_SV // LANES)),
            pl.BlockSpec((1, c, LANES), lambda i, j: (i, 0, ZC_SK // LANES)),
            pl.BlockSpec((1, c, LANES), lambda i, j: (i, 0, ZC_SV // LANES)),
            pl.BlockSpec(memory_space=pltpu.SMEM),
        ],
        out_specs=pl.BlockSpec((1, tq, MIX_W), lambda i, j: (i, j, 0)),
        out_shape=jax.ShapeDtypeStruct((b, s, MIX_W), BF16),
        scratch_shapes=(_exp_scratch(win + c, UNIT_Q, HEADS * tq // UNIT_Q)
                        + [pltpu.VMEM((tq // UNIT_Q, win, UNIT_Q), F32)]),
        compiler_params=_params("arbitrary", "arbitrary"),
        name="swa",
    )(zl, zl, zl, zc, zc, sink)


def _na_kernel(rows_n, q_ref, k_ref, v_ref, kc_ref, vc_ref, bias_ref, o_ref, e_scr):
    c_len = kc_ref.shape[1]
    n_rb = q_ref.shape[1] // NA_TQ
    nrb = rows_n // 2
    chunks = (_key_chunks(NA_NLOC, WINDOW_CHUNK)
              + [(NA_NLOC + o, r) for o, r in _key_chunks(c_len, WINDOW_CHUNK)])
    starts, tids = [], []
    for i in range(n_rb):
        rb = pl.program_id(1) * n_rb + i
        starts.append(pl.multiple_of(
            jnp.clip(2 * rb - NA_WIN_R // 2, 0, rows_n - NA_KROWS) * GRID_W, 2 * GRID_W))
        tids.append(jnp.minimum(rb, 2) + jnp.maximum(rb - (nrb - 3), 0))
    blk = lambda h: slice((h // 2) * LANES, (h // 2 + 1) * LANES)
    units = [(i, h) for i in range(n_rb) for h in range(HEADS)]
    qs = [_lane_mask(q_ref[0, i * NA_TQ:(i + 1) * NA_TQ, blk(h)], (h % 2) * HEAD_DIM, HEAD_DIM)
          for i, h in units]

    def score_of(u, c):
        off, rows = chunks[c]
        i, h = units[u]
        if off < NA_NLOC:
            k = k_ref[0, pl.ds(starts[i] + off, rows), blk(h)]
            return _dot_nt(k, qs[u]) + bias_ref[h, tids[i], off:off + rows, :]
        return _dot_nt(kc_ref[0, off - NA_NLOC:off - NA_NLOC + rows, blk(h)], qs[u])

    def pv_of(u, c, e):
        i, h = units[u]
        off, rows = chunks[c]
        if off < NA_NLOC:
            return _head_rows(_dot_tn(v_ref[0, pl.ds(starts[i] + off, rows), blk(h)], e), h)
        return _head_rows(_dot_tn(vc_ref[0, off - NA_NLOC:off - NA_NLOC + rows, blk(h)], e), h)

    outs = _localmax_batched(len(units), chunks, score_of, lambda u: None, pv_of, e_scr)
    for i in range(n_rb):
        o = jnp.concatenate(outs[i * HEADS:(i + 1) * HEADS], axis=0)
        o_ref[0, i * NA_TQ:(i + 1) * NA_TQ, :] = o.T.astype(BF16)


def _na_call(zl, zc, bias, tq):
    b, s, _ = zl.shape
    c = zc.shape[1]
    rows_n = s // GRID_W
    return pl.pallas_call(
        functools.partial(_na_kernel, rows_n),
        grid=(b, s // tq),
        in_specs=[
            pl.BlockSpec((1, tq, MIX_W), lambda i, j: (i, j, ZC_NQ // MIX_W)),
            pl.BlockSpec((1, s, MIX_W), lambda i, j: (i, 0, ZC_NK // MIX_W)),
            pl.BlockSpec((1, s, MIX_W), lambda i, j: (i, 0, ZC_NV // MIX_W)),
            pl.BlockSpec((1, c, MIX_W), lambda i, j: (i, 0, ZC_NK // MIX_W)),
            pl.BlockSpec((1, c, MIX_W), lambda i, j: (i, 0, ZC_NV // MIX_W)),
            pl.BlockSpec(bias.shape, lambda i, j: (0, 0, 0, 0), pipeline_mode=pl.Buffered(1)),
        ],
        out_specs=pl.BlockSpec((1, tq, MIX_W), lambda i, j: (i, j, 0)),
        out_shape=jax.ShapeDtypeStruct((b, s, MIX_W), BF16),
        scratch_shapes=_exp_scratch(NA_NLOC + c, NA_TQ, HEADS * tq // NA_TQ),
        compiler_params=_params("arbitrary", "arbitrary"),
        name="na",
    )(zl, zl, zl, zc, zc, bias)


def _ctx_kernel(lam_init, z_ref, lp_ref, gain_ref, sink_ref, oa_ref, ob_ref, oc_ref, od_ref,
                e_scr):
    c_len = z_ref.shape[1]

    def blk(col, j=0):
        return z_ref[0, :, col + j * LANES:col + (j + 1) * LANES]

    units = []
    for h in range(HEADS):
        units.append((_lane_mask(blk(ZC_NQ, h // 2), (h % 2) * HEAD_DIM, HEAD_DIM),
                      ZC_NK + (h // 2) * LANES, ZC_NV + (h // 2) * LANES, (h % 2) * HEAD_DIM, None))
    for h in range(HEADS):
        units.append((blk(ZC_MQ, h), ZC_MK + h * LANES, ZC_MV + (h // 2) * LANES,
                      (h % 2) * HEAD_DIM, None))
    for h in range(HEADS):
        hk, g = h // SWA_GROUP, h % SWA_GROUP
        units.append((_lane_mask(blk(ZC_SQ, g), hk * HEAD_DIM, HEAD_DIM), ZC_SK, ZC_SV,
                      hk * HEAD_DIM, h))
    for w in range(2 * HEADS):
        units.append((_lane_mask(blk(ZC_DQ, w // 4), (w % 4) * DIFF_DH, DIFF_DH),
                      ZC_DK + (w // 4) * LANES, ZC_DV + (w // 4) * LANES, ((w // 2) % 2) * HEAD_DIM, None))

    def score_of(u, c):
        q_m, kcol = units[u][0], units[u][1]
        return _dot_nt(z_ref[0, :, kcol:kcol + LANES], q_m)

    def extra_of(u):
        h = units[u][4]
        return None if h is None else jnp.full((1, 1), sink_ref[h] * LOG2E, F32)

    def pv_of(u, c, e):
        vcol, r0 = units[u][2], units[u][3]
        return _dot_tn(z_ref[0, :, vcol:vcol + LANES], e)[r0:r0 + HEAD_DIM]

    outs = _localmax_batched(len(units), [(0, c_len)], score_of, extra_of, pv_of, e_scr)
    for i, ref in enumerate((oa_ref, ob_ref, oc_ref)):
        ref[0] = jnp.concatenate(outs[i * HEADS:(i + 1) * HEADS], axis=0).T.astype(BF16)
    o = _diff_combine(outs[3 * HEADS:], _diff_lambda(lp_ref[...], lam_init))
    od_ref[0] = (o.T * gain_ref[...]).astype(BF16)


def _ctx_call(zc, lp, gain, sink, lam_init):
    b, c, _ = zc.shape
    out = jax.ShapeDtypeStruct((b, c, MIX_W), BF16)
    ospec = pl.BlockSpec((1, c, MIX_W), lambda i: (i, 0, 0))
    return pl.pallas_call(
        functools.partial(_ctx_kernel, lam_init),
        grid=(b,),
        in_specs=[
            pl.BlockSpec((1, c, Z_COLS), lambda i: (i, 0, 0)),
            pl.BlockSpec(lp.shape, lambda i: (0, 0)),
            pl.BlockSpec(gain.shape, lambda i: (0, 0)),
            pl.BlockSpec(memory_space=pltpu.SMEM),
        ],
        out_specs=[ospec] * 4,
        out_shape=[out] * 4,
        scratch_shapes=_exp_scratch(c, c, 5 * HEADS),
        compiler_params=_params("arbitrary"),
        name="ctx_attn",
    )(zc, lp, gain, sink)


def _mlp_kernel(final, x_ref, ma_ref, mb_ref, mc_ref, md_ref, mod_ref, g_ref, gf_ref,
                wout_ref, wup_ref, wdn_ref, o_ref):
    x = x_ref[0]
    mod = mod_ref[0]
    attn = None
    for i, m_ref in enumerate((ma_ref, mb_ref, mc_ref, md_ref)):
        part = _dot(m_ref[0], wout_ref[i * MIX_W:(i + 1) * MIX_W, :])
        attn = part if attn is None else attn + part
    x = x + mod[2:3] * attn
    h = x * _rms_scale(x, D_MODEL) * g_ref[...]
    h = (h * (1.0 + mod[4:5]) + mod[3:4]).astype(BF16)
    acc = None
    ck = FF_CHUNK
    for c in range(D_FF // ck):
        u = jnp.maximum(_dot(h, wup_ref[:, c * ck:(c + 1) * ck]), 0.0)
        part = _dot((u * u).astype(BF16), wdn_ref[c * ck:(c + 1) * ck, :])
        acc = part if acc is None else acc + part
    x = x + mod[5:6] * acc
    if final:
        x = x * _rms_scale(x, D_MODEL) * gf_ref[...]
    o_ref[0] = x


def _mlp_call(x3, mixes, mod, mod_row, g, gf, wout, wup, wdn, tm, final):
    nb, n, d = x3.shape
    const = lambda t, b: (0, 0)
    tok = lambda t, b: (b, t, 0)
    mod_map = (lambda t, b: (b, 0, 0)) if mod_row is None else (lambda t, b: (mod_row, 0, 0))
    single = pl.Buffered(1)
    return pl.pallas_call(
        functools.partial(_mlp_kernel, final),
        grid=(n // tm, nb),
        in_specs=[pl.BlockSpec((1, tm, d), tok)]
        + [pl.BlockSpec((1, tm, MIX_W), tok)] * 4
        + [
            pl.BlockSpec((1, 6, d), mod_map),
            pl.BlockSpec((1, d), const),
            pl.BlockSpec((1, d), const),
            pl.BlockSpec(wout.shape, const, pipeline_mode=single),
            pl.BlockSpec(wup.shape, const, pipeline_mode=single),
            pl.BlockSpec(wdn.shape, const, pipeline_mode=single),
        ],
        out_specs=pl.BlockSpec((1, tm, d), tok),
        out_shape=jax.ShapeDtypeStruct((nb, n, d), F32),
        compiler_params=_params("arbitrary", "arbitrary"),
        name="mlp",
    )(x3, *mixes, mod, g, gf, wout, wup, wdn)


def _rope_tables(s_len):
    t = np.arange(s_len)
    row, col = t // GRID_W, t % GRID_W

    def axial(dim):
        n_freq = dim // 4
        freqs = jnp.asarray(ROPE_BASE, F32) ** (-jnp.arange(n_freq, dtype=F32) / n_freq)
        ang = jnp.concatenate([jnp.asarray(row, F32)[:, None] * freqs,
                               jnp.asarray(col, F32)[:, None] * freqs], axis=-1)
        return jnp.cos(ang), jnp.sin(ang)

    def group(cs, n_groups, scale):
        cos, sin = cs
        c = jnp.concatenate([cos, cos], axis=-1) * scale
        s = jnp.concatenate([-sin, sin], axis=-1) * scale
        return jnp.tile(c, (1, n_groups)), jnp.tile(s, (1, n_groups))

    r64, r32 = axial(HEAD_DIM), axial(MLA_ROPE)
    ones = lambda n, v=1.0: jnp.full((s_len, n), v, F32)
    zeros = lambda n: jnp.zeros((s_len, n), F32)
    mla_scale = (MLA_NOPE + MLA_ROPE) ** -0.5 * LOG2E
    mq_c, mq_s = group(r32, 1, mla_scale)
    parts = [
        group(r64, 4, HEAD_DIM ** -0.5 * LOG2E), group(r64, 2, 1.0),
        group(r32, 8, DIFF_DH ** -0.5 * LOG2E), group(r32, 8, 1.0),
        (ones(MLA_Q_LORA), zeros(MLA_Q_LORA)), group(r32, 1, 1.0), (ones(32), zeros(32)),
    ] + [(jnp.concatenate([ones(MLA_NOPE, mla_scale), mq_c, ones(32, mla_scale)], axis=-1),
          jnp.concatenate([zeros(MLA_NOPE), mq_s, zeros(32)], axis=-1))] * HEADS
    cos = jnp.concatenate([p[0] for p in parts], axis=-1)
    sin = jnp.concatenate([p[1] for p in parts], axis=-1)
    return cos, sin, cos[0:1]


def _permute_w_in(w_in):
    offs = np.cumsum([0, 256, 256, 256, MLA_Q_LORA, MLA_KV_LORA, MLA_ROPE, 256, 128, 128, 256, 256, 256])
    sec = lambda i: w_in[..., offs[i]:offs[i + 1]]
    sq = sec(6)
    sq = sq.reshape(sq.shape[:-1] + (SWA_HKV, SWA_GROUP, HEAD_DIM))
    sq = jnp.swapaxes(sq, -3, -2).reshape(w_in.shape[:-1] + (256,))
    pad = jnp.zeros(w_in.shape[:-1] + (32,), w_in.dtype)
    cols = [sq, sec(7), sec(9), sec(10), sec(4), sec(3), sec(5), pad,
            sec(0), sec(1), sec(2), sec(11), sec(8)]
    return jnp.concatenate(cols, axis=-1).astype(BF16)


def _mla_weights(w_uq, w_ukv):
    depth = w_uq.shape[0]
    uq = w_uq.reshape(depth, MLA_Q_LORA, HEADS, MLA_NOPE + MLA_ROPE)
    uq = jnp.pad(uq, ((0, 0), (0, 256 - MLA_Q_LORA), (0, 0), (0, LANES - MLA_NOPE - MLA_ROPE)))
    uq = uq.reshape(depth, 256, HEADS * LANES).astype(BF16)
    ukv = w_ukv.reshape(depth, MLA_KV_LORA, HEADS, 2, HEAD_DIM)
    uk = jnp.pad(ukv[:, :, :, 0], ((0, 0), (0, 0), (0, 0), (0, LANES - MLA_NOPE)))
    uk = uk.reshape(depth, MLA_KV_LORA, HEADS * LANES).astype(BF16)
    uv = ukv[:, :, :, 1].reshape(depth, MLA_KV_LORA, HEADS * HEAD_DIM).astype(BF16)
    return uq, uk, uv


def _na_bias_tables(rpb, rows_n):
    kr_n = min(NA_WIN_R, rows_n)
    col = np.arange(GRID_W)
    c0 = np.clip(col - NA_WIN_C // 2, 0, GRID_W - NA_WIN_C)
    col_ok_t = ((col[None, :] >= c0[:, None]) & (col[None, :] < c0[:, None] + NA_WIN_C)).T
    dc_idx_t = np.clip(col[:, None] - col[None, :], 1 - NA_WIN_C, NA_WIN_C - 1) + (NA_WIN_C - 1)
    toe = jnp.zeros(rpb.shape[:3] + (GRID_W, GRID_W), F32)
    for d in range(2 * NA_WIN_C - 1):
        toe = jnp.where(dc_idx_t == d, rpb[..., d, None, None].astype(F32), toe)
    toe = jnp.where(col_ok_t, toe * LOG2E, NEG_INF)
    neg = jnp.full(toe.shape[:2] + (GRID_W, GRID_W), NEG_INF, F32)
    nrb = rows_n // 2
    pairs = {}

    def pair(drs):
        if drs not in pairs:
            pairs[drs] = jnp.concatenate([neg if d is None else toe[:, :, d] for d in drs], axis=-1)
        return pairs[drs]

    blocks = []
    for rb in (0, 1, 2, nrb - 2, nrb - 1):
        start = int(np.clip(2 * rb - NA_WIN_R // 2, 0, rows_n - NA_KROWS))
        for j in range(NA_KROWS):
            drs = []
            for i in range(2):
                kr, qr = start + j, 2 * rb + i
                r0 = int(np.clip(qr - kr_n // 2, 0, rows_n - kr_n))
                drs.append(kr - qr + NA_WIN_R - 1 if r0 <= kr < r0 + kr_n else None)
            blocks.append(pair(tuple(drs)))
    tab = jnp.stack(blocks, axis=2)
    return tab.reshape(tab.shape[:2] + (5, NA_NLOC, NA_TQ))


def kernel(x, c, ctx, c_ctx, w_ada, b_ada, norm_attn_g, w_in, na_rpb, mla_q_norm_g, mla_w_uq,
           mla_kv_norm_g, mla_w_ukv, swa_sink, diff_lambda, diff_norm_g, w_out, norm_mlp_g,
           w_up, w_down, final_norm_g):
    b, s, d = x.shape
    c_len = ctx.shape[1]
    depth = w_in.shape[0]
    rows_n = s // GRID_W
    assert d == D_MODEL and s % max(ATTN_TQ, DIFF_TQ, TOKEN_TILE) == 0 and rows_n >= NA_KROWS
    assert c_len % KEY_CHUNK == 0 and (b * c_len) % min(TOKEN_TILE, b * c_len) == 0

    w_in_p = _permute_w_in(w_in)
    uq, uk, uv = _mla_weights(mla_w_uq, mla_w_ukv)
    gq = jnp.pad(mla_q_norm_g, ((0, 0), (0, 256 - MLA_Q_LORA)))[:, None, :]
    gkv = mla_kv_norm_g[:, None, :]
    w_out_b, w_up_b, w_dn_b = w_out.astype(BF16), w_up.astype(BF16), w_down.astype(BF16)
    na_bias = _na_bias_tables(na_rpb, rows_n)
    cos, sin, cos_ctx = _rope_tables(s)
    tm_l, tm_c = TOKEN_TILE, min(TOKEN_TILE, b * c_len)
    cos_c = jnp.broadcast_to(cos_ctx, (tm_c, T_COLS))
    sin_c = jnp.zeros((tm_c, T_COLS), F32)

    c_rows = -(-(b + 1) // 8) * 8
    c_all = jnp.zeros((c_rows, d), F32).at[:b].set(c).at[b].set(c_ctx)
    mod = _ada_call(c_all, w_ada, b_ada).reshape(depth, c_rows, 6, d)

    xl = x
    xc = ctx.reshape(1, b * c_len, d)
    for l in range(depth):
        need_ctx = l < depth - 1
        lam_init = 0.8 - 0.6 * math.exp(-0.3 * l)
        g_attn = norm_attn_g[l][None, :]
        g_mlp = norm_mlp_g[l][None, :]
        gf = final_norm_g[None, :]
        d_gain = (jnp.tile(diff_norm_g[l], HEADS) * (1.0 - lam_init))[None, :]
        proj = (w_in_p[l], gq[l], gkv[l], uq[l], uk[l], uv[l])
        zl = _inproj_call(xl, mod[l], None, g_attn, cos, sin, True, *proj, tm=tm_l)
        zc = _inproj_call(xc, mod[l], b, g_attn, cos_c, sin_c, False, *proj, tm=tm_c)
        zc = zc.reshape(b, c_len, Z_COLS)
        mixes = (
            _na_call(zl, zc, na_bias[l], tq=ATTN_TQ),
            _mla_call(zl, zc, tq=ATTN_TQ),
            _swa_call(zl, zc, swa_sink[l], tq=ATTN_TQ),
            _diff_call(zl, zc, diff_lambda[l], d_gain, lam_init, tq=DIFF_TQ),
        )
        wts = (w_out_b[l], w_up_b[l], w_dn_b[l])
        xl = _mlp_call(xl, mixes, mod[l], None, g_mlp, gf, *wts, tm=tm_l, final=not need_ctx)
        if need_ctx:
            mixes_c = _ctx_call(zc, diff_lambda[l], d_gain, swa_sink[l], lam_init)
            mixes_c = [m.reshape(1, b * c_len, MIX_W) for m in mixes_c]
            xc = _mlp_call(xc, mixes_c, mod[l], b, g_mlp, gf, *wts, tm=tm_c, final=False)
    return xl
```

```python
import functools
import math

import jax
import jax.numpy as jnp
import numpy as np
from jax import lax
from jax.experimental import pallas as pl
from jax.experimental.pallas import tpu as pltpu

F32 = jnp.float32
BF16 = jnp.bfloat16

D_MODEL = 1024
GRID_W = 64
HEADS = 4
HEAD_DIM = 64
NA_WIN_R = 8
NA_WIN_C = 16
MLA_Q_LORA = 192
MLA_KV_LORA = 128
MLA_NOPE = 64
MLA_ROPE = 32
SWA_HKV = 2
SWA_GROUP = 2
SWA_WINDOW = 128
DIFF_DH = 32
D_FF = 4 * D_MODEL
ROPE_BASE = 10000.0
EPS = 1e-6
NEG_INF = -1e30
LOG2E = math.log2(math.e)

LANES = 128
VMEM_LIMIT = 56 * 1024 * 1024

ZC_DQ, ZC_DK, ZC_SQ, ZC_NQ, ZC_NK, ZC_NV, ZC_DV, ZC_MV = 0, 256, 512, 768, 1024, 1280, 1536, 1792
ZC_MQ, ZC_MK, ZC_SK, ZC_SV = 2048, 2560, 3072, 3200
Z_COLS = 3328
W_R64, W_R32, W_M, W_N, W_COLS = 0, 384, 896, 1280, 2432
T_R64, T_R32, T_B12, T_MQ, T_COLS = 0, 384, 896, 1152, 1664

NA_KROWS = 10
NA_TQ = 2 * GRID_W
NA_NLOC = NA_KROWS * GRID_W
KEY_CHUNK = 256
WINDOW_CHUNK = 512
UNIT_Q = 256
MIX_W = HEADS * HEAD_DIM
MLA_W = HEADS * LANES

TOKEN_TILE = 512
ATTN_TQ = 1024
DIFF_TQ = 1024
ADA_TN = 1536
FF_CHUNK = 1024


def _params(*sem):
    return pltpu.CompilerParams(dimension_semantics=sem, vmem_limit_bytes=VMEM_LIMIT)


def _dot(a, b):
    return jnp.dot(a, b, preferred_element_type=F32)


def _dot_nt(a, b):
    return lax.dot_general(a, b, (((1,), (1,)), ((), ())), preferred_element_type=F32)


def _dot_tn(a, b):
    return lax.dot_general(a, b, (((0,), (0,)), ((), ())), preferred_element_type=F32)


def _rms_scale(x, n):
    return lax.rsqrt(jnp.sum(x * x, axis=-1, keepdims=True) * (1.0 / n) + EPS)


def _ada_kernel(c_ref, w_ref, b_ref, o_ref):
    c = c_ref[...]
    act = (c * (1.0 / (1.0 + jnp.exp(-c)))).astype(BF16)
    o_ref[0] = _dot(act, w_ref[0].astype(BF16)) + b_ref[0]


def _ada_call(c_all, w_ada, b_ada):
    depth, d, n = w_ada.shape
    rows = c_all.shape[0]
    tn = ADA_TN
    return pl.pallas_call(
        _ada_kernel,
        grid=(depth, n // tn),
        in_specs=[
            pl.BlockSpec((rows, d), lambda l, j: (0, 0)),
            pl.BlockSpec((1, d, tn), lambda l, j: (l, 0, j)),
            pl.BlockSpec((1, 1, tn), lambda l, j: (l, 0, j)),
        ],
        out_specs=pl.BlockSpec((1, rows, tn), lambda l, j: (l, 0, j)),
        out_shape=jax.ShapeDtypeStruct((depth, rows, n), F32),
        compiler_params=_params("arbitrary", "arbitrary"),
        name="ada",
    )(c_all, w_ada, b_ada.reshape(depth, 1, n))


def _rope_blocks(z, cos_ref, sin_ref, tcol, half):
    tm = z.shape[0]
    lane = lax.broadcasted_iota(jnp.int32, (tm, LANES), 1)
    first = (lane % (2 * half)) < half
    out = []
    for j in range(z.shape[1] // LANES):
        xb = z[:, j * LANES:(j + 1) * LANES]
        rot = jnp.where(first, pltpu.roll(xb, LANES - half, 1), pltpu.roll(xb, half, 1))
        c0 = tcol + j * LANES
        out.append(xb * cos_ref[:, c0:c0 + LANES] + rot * sin_ref[:, c0:c0 + LANES])
    return out


def _inproj_kernel(x_ref, mod_ref, g_ref, cos_ref, sin_ref, w_ref, gq_ref, gkv_ref,
                   wuq_ref, wukvk_ref, wukvv_ref, o_ref):
    x = x_ref[0]
    mod = mod_ref[0]
    h = x * _rms_scale(x, D_MODEL) * g_ref[...]
    h = (h * (1.0 + mod[1:2]) + mod[0:1]).astype(BF16)

    def put(col, val):
        o_ref[0, :, col:col + val.shape[1]] = val.astype(BF16)

    z64 = _dot(h, w_ref[:, W_R64:W_R32])
    z32 = _dot(h, w_ref[:, W_R32:W_M])
    r = _rope_blocks(z64, cos_ref, sin_ref, T_R64, HEAD_DIM // 2)
    put(ZC_SQ, r[0]); put(ZC_SQ + LANES, r[1]); put(ZC_SK, r[2])
    zm = _dot(h, w_ref[:, W_M:W_N])
    r = _rope_blocks(z32, cos_ref, sin_ref, T_R32, DIFF_DH // 2)
    for j in range(4):
        put(ZC_DQ + j * LANES, r[j])
    zn = _dot(h, w_ref[:, W_N:W_COLS])

    ckv = zm[:, 0:MLA_KV_LORA]
    b12 = zm[:, MLA_KV_LORA:]
    ckv_n = (ckv * _rms_scale(ckv, MLA_KV_LORA) * gkv_ref[...]).astype(BF16)
    lane = lax.broadcasted_iota(jnp.int32, b12.shape, 1)
    cq = jnp.where(lane < MLA_Q_LORA, b12, 0.0)
    cq_n = (cq * _rms_scale(cq, MLA_Q_LORA) * gq_ref[...]).astype(BF16)
    zq = _dot(cq_n, wuq_ref[...])
    put(ZC_NQ, zn[:, 0:256] * (HEAD_DIM ** -0.5 * LOG2E))
    put(ZC_NK, zn[:, 256:1024])
    put(ZC_SV, zn[:, 1024:1152])
    q = _rope_blocks(zq, cos_ref, sin_ref, T_MQ, MLA_ROPE // 2)
    for j in range(HEADS):
        put(ZC_MQ + j * LANES, q[j])
    kr = _rope_blocks(b12[:, LANES:], cos_ref, sin_ref, T_B12 + LANES, MLA_ROPE // 2)[0]
    lane = lax.broadcasted_iota(jnp.int32, kr.shape, 1)
    kr = jnp.where(jnp.where(lane >= MLA_NOPE, lane, LANES) < MLA_NOPE + MLA_ROPE, kr, 0.0)
    kn = _dot(ckv_n, wukvk_ref[...])
    for j in range(HEADS):
        put(ZC_MK + j * LANES, kn[:, j * LANES:(j + 1) * LANES] + kr)
    put(ZC_MV, _dot(ckv_n, wukvv_ref[...]))


def _inproj_call(x3, mod, mod_row, g, cos, sin, table_per_tile, w, gq, gkv, wuq, wukvk, wukvv, tm):
    nb, n, d = x3.shape
    nt = n // tm
    const = lambda t, b: (0, 0)
    tab_map = (lambda t, b: (t, 0)) if table_per_tile else const
    mod_map = (lambda t, b: (b, 0, 0)) if mod_row is None else (lambda t, b: (mod_row, 0, 0))
    return pl.pallas_call(
        _inproj_kernel,
        grid=(nt, nb),
        in_specs=[
            pl.BlockSpec((1, tm, d), lambda t, b: (b, t, 0)),
            pl.BlockSpec((1, 6, d), mod_map),
            pl.BlockSpec((1, d), const),
            pl.BlockSpec((tm, T_COLS), tab_map),
            pl.BlockSpec((tm, T_COLS), tab_map),
            pl.BlockSpec(w.shape, const, pipeline_mode=pl.Buffered(1)),
            pl.BlockSpec(gq.shape, const),
            pl.BlockSpec(gkv.shape, const),
            pl.BlockSpec(wuq.shape, const),
            pl.BlockSpec(wukvk.shape, const),
            pl.BlockSpec(wukvv.shape, const),
        ],
        out_specs=pl.BlockSpec((1, tm, Z_COLS), lambda t, b: (b, t, 0)),
        out_shape=jax.ShapeDtypeStruct((nb, n, Z_COLS), BF16),
        compiler_params=_params("arbitrary", "arbitrary"),
        name="inproj",
    )(x3, mod, g, cos, sin, w, gq, gkv, wuq, wukvk, wukvv)


def _lane_mask(q, lo, width):
    lane = lax.broadcasted_iota(jnp.int32, q.shape, 1)
    keep = jnp.where(lane >= lo, lane, LANES) < lo + width
    return jnp.where(keep, q.astype(F32), 0.0).astype(BF16)


def _localmax_softmax(n_units, chunks, score_of, extra_of, pv_of, e_scr):
    outs = [None] * n_units
    mloc, mfin = {}, {}
    n_c = len(chunks)
    for t in range(n_units + 1):
        acc = None
        if t < n_units:
            mloc[t] = []
        for cc in range(n_c + 1):
            if t < n_units and cc < n_c:
                off, rows = chunks[cc]
                s = score_of(t, cc)
                mc = jnp.max(s, axis=0, keepdims=True)
                e_scr[t % 2, off:off + rows, :] = jnp.exp2(s - mc).astype(BF16)
                mloc[t].append(mc)
            if t >= 1 and cc >= 1:
                u, c = t - 1, cc - 1
                off, rows = chunks[c]
                pv = pv_of(u, c, e_scr[u % 2, off:off + rows, :]) * jnp.exp2(mloc[u][c] - mfin[u])
                acc = pv if acc is None else acc + pv
        if t < n_units:
            m = mloc[t][0]
            for mc in mloc[t][1:]:
                m = jnp.maximum(m, mc)
            x = extra_of(t)
            mfin[t] = m if x is None else jnp.maximum(m, x)
        if t >= 1:
            l = acc[HEAD_DIM:HEAD_DIM + 1]
            x = extra_of(t - 1)
            if x is not None:
                l = l + jnp.exp2(x - mfin[t - 1])
            outs[t - 1] = acc[0:HEAD_DIM] * (1.0 / l)
    return outs


def _localmax_batched(n_units, chunks, score_of, extra_of, pv_of, e_scr):
    mloc, lloc, mfin = [], [], []
    for u in range(n_units):
        ms, ls = [], []
        for c, (off, rows) in enumerate(chunks):
            s = score_of(u, c)
            mc = jnp.max(s, axis=0, keepdims=True)
            e = jnp.exp2(s - mc)
            ls.append(jnp.sum(e, axis=0, keepdims=True))
            e_scr[u, off:off + rows, :] = e.astype(BF16)
            ms.append(mc)
        m = ms[0]
        for mc in ms[1:]:
            m = jnp.maximum(m, mc)
        x = extra_of(u)
        mloc.append(ms); lloc.append(ls); mfin.append(m if x is None else jnp.maximum(m, x))
    outs = []
    for u in range(n_units):
        acc = l = None
        for c, (off, rows) in enumerate(chunks):
            alpha = jnp.exp2(mloc[u][c] - mfin[u])
            pv = pv_of(u, c, e_scr[u, off:off + rows, :]) * alpha
            acc = pv if acc is None else acc + pv
            l = lloc[u][c] * alpha if l is None else l + lloc[u][c] * alpha
        x = extra_of(u)
        if x is not None:
            l = l + jnp.exp2(x - mfin[u])
        outs.append(acc * (1.0 / l))
    return outs


def _head_rows(o, h):
    return o[(h % 2) * HEAD_DIM:(h % 2 + 1) * HEAD_DIM]


def _diff_lambda(lp, lam_init):
    a = jnp.sum(lp[0:1] * lp[1:2], axis=-1, keepdims=True)
    b = jnp.sum(lp[2:3] * lp[3:4], axis=-1, keepdims=True)
    return jnp.exp(a) - jnp.exp(b) + lam_init


def _diff_combine(outs, lam):
    res = []
    for h in range(HEADS):
        o = outs[2 * h] - lam * outs[2 * h + 1]
        ms = jnp.sum(o * o, axis=0, keepdims=True) * (1.0 / HEAD_DIM)
        res.append(o * lax.rsqrt(ms + EPS))
    return jnp.concatenate(res, axis=0)


def _key_chunks(total, size):
    return [(o, min(size, total - o)) for o in range(0, total, size)]


def _exp_scratch(nk, tq, slots=2):
    return [pltpu.VMEM((slots, nk, tq), BF16)]


VT_ROWS = HEAD_DIM + 16


def _transpose_values(vt_scr, srcs):
    for ref, off, n in srcs:
        for r in range(0, n, KEY_CHUNK):
            rows = min(KEY_CHUNK, n - r)
            vt = ref[0, r:r + rows, :].astype(F32).T
            for h in range(HEADS):
                vt_scr[h * VT_ROWS:h * VT_ROWS + HEAD_DIM, off + r:off + r + rows] = (
                    vt[h * HEAD_DIM:(h + 1) * HEAD_DIM].astype(BF16))
                vt_scr[h * VT_ROWS + HEAD_DIM:(h + 1) * VT_ROWS, off + r:off + r + rows] = (
                    jnp.ones((VT_ROWS - HEAD_DIM, rows), BF16))


def _mla_kernel(q_ref, k_ref, v_ref, kc_ref, vc_ref, o_ref, e_scr, vt_scr):
    s_len, c_len = k_ref.shape[1], kc_ref.shape[1]

    @pl.when(pl.program_id(1) == 0)
    def _():
        _transpose_values(vt_scr, [(v_ref, 0, s_len), (vc_ref, s_len, c_len)])

    chunks = _key_chunks(s_len, KEY_CHUNK) + [(s_len + o, r) for o, r in _key_chunks(c_len, KEY_CHUNK)]
    units = [(qt, h) for qt in range(q_ref.shape[1] // UNIT_Q) for h in range(HEADS)]
    qs = [q_ref[0, qt * UNIT_Q:(qt + 1) * UNIT_Q, h * LANES:(h + 1) * LANES] for qt, h in units]

    def score_of(u, c):
        off, rows = chunks[c]
        h = units[u][1]
        sl = slice(h * LANES, (h + 1) * LANES)
        if off < s_len:
            return _dot_nt(k_ref[0, off:off + rows, sl], qs[u])
        return _dot_nt(kc_ref[0, off - s_len:off - s_len + rows, sl], qs[u])

    def pv_of(u, c, e):
        h = units[u][1]
        off, rows = chunks[c]
        return _dot(vt_scr[h * VT_ROWS:(h + 1) * VT_ROWS, off:off + rows], e)

    outs = _localmax_softmax(len(units), chunks, score_of, lambda u: None, pv_of, e_scr)
    for qt in range(len(units) // HEADS):
        o = jnp.concatenate(outs[qt * HEADS:(qt + 1) * HEADS], axis=0)
        o_ref[0, qt * UNIT_Q:(qt + 1) * UNIT_Q, :] = o.T.astype(BF16)


def _mla_call(zl, zc, tq):
    b, s, _ = zl.shape
    c = zc.shape[1]
    return pl.pallas_call(
        _mla_kernel,
        grid=(b, s // tq),
        in_specs=[
            pl.BlockSpec((1, tq, MLA_W), lambda i, j: (i, j, ZC_MQ // MLA_W)),
            pl.BlockSpec((1, s, MLA_W), lambda i, j: (i, 0, ZC_MK // MLA_W)),
            pl.BlockSpec((1, s, MIX_W), lambda i, j: (i, 0, ZC_MV // MIX_W)),
            pl.BlockSpec((1, c, MLA_W), lambda i, j: (i, 0, ZC_MK // MLA_W)),
            pl.BlockSpec((1, c, MIX_W), lambda i, j: (i, 0, ZC_MV // MIX_W)),
        ],
        out_specs=pl.BlockSpec((1, tq, MIX_W), lambda i, j: (i, j, 0)),
        out_shape=jax.ShapeDtypeStruct((b, s, MIX_W), BF16),
        scratch_shapes=_exp_scratch(s + c, UNIT_Q) + [pltpu.VMEM((HEADS * VT_ROWS, s + c), BF16)],
        compiler_params=_params("arbitrary", "arbitrary"),
        name="mla",
    )(zl, zl, zl, zc, zc)


def _diff_kernel(lam_init, q_ref, k_ref, v_ref, kc_ref, vc_ref, lp_ref, gain_ref, o_ref,
                 e_scr, vt_scr):
    s_len, c_len = k_ref.shape[1], kc_ref.shape[1]

    @pl.when(pl.program_id(1) == 0)
    def _():
        _transpose_values(vt_scr, [(v_ref, 0, s_len), (vc_ref, s_len, c_len)])

    chunks = _key_chunks(s_len, KEY_CHUNK) + [(s_len + o, r) for o, r in _key_chunks(c_len, KEY_CHUNK)]
    n_sub = 2 * HEADS
    units = [(qt, w) for qt in range(q_ref.shape[1] // UNIT_Q) for w in range(n_sub)]
    qs = [_lane_mask(q_ref[0, qt * UNIT_Q:(qt + 1) * UNIT_Q, (w // 4) * LANES:(w // 4 + 1) * LANES],
                     (w % 4) * DIFF_DH, DIFF_DH) for qt, w in units]

    def score_of(u, c):
        off, rows = chunks[c]
        w = units[u][1]
        sl = slice((w // 4) * LANES, (w // 4 + 1) * LANES)
        if off < s_len:
            return _dot_nt(k_ref[0, off:off + rows, sl], qs[u])
        return _dot_nt(kc_ref[0, off - s_len:off - s_len + rows, sl], qs[u])

    def pv_of(u, c, e):
        h = units[u][1] // 2
        off, rows = chunks[c]
        return _dot(vt_scr[h * VT_ROWS:(h + 1) * VT_ROWS, off:off + rows], e)

    outs = _localmax_softmax(len(units), chunks, score_of, lambda u: None, pv_of, e_scr)
    lam = _diff_lambda(lp_ref[...], lam_init)
    for qt in range(len(units) // n_sub):
        o = _diff_combine(outs[qt * n_sub:(qt + 1) * n_sub], lam)
        o_ref[0, qt * UNIT_Q:(qt + 1) * UNIT_Q, :] = (o.T * gain_ref[...]).astype(BF16)


def _diff_call(zl, zc, lp, gain, lam_init, tq):
    b, s, _ = zl.shape
    c = zc.shape[1]
    return pl.pallas_call(
        functools.partial(_diff_kernel, lam_init),
        grid=(b, s // tq),
        in_specs=[
            pl.BlockSpec((1, tq, MIX_W), lambda i, j: (i, j, ZC_DQ // MIX_W)),
            pl.BlockSpec((1, s, MIX_W), lambda i, j: (i, 0, ZC_DK // MIX_W)),
            pl.BlockSpec((1, s, MIX_W), lambda i, j: (i, 0, ZC_DV // MIX_W)),
            pl.BlockSpec((1, c, MIX_W), lambda i, j: (i, 0, ZC_DK // MIX_W)),
            pl.BlockSpec((1, c, MIX_W), lambda i, j: (i, 0, ZC_DV // MIX_W)),
            pl.BlockSpec(lp.shape, lambda i, j: (0, 0)),
            pl.BlockSpec(gain.shape, lambda i, j: (0, 0)),
        ],
        out_specs=pl.BlockSpec((1, tq, MIX_W), lambda i, j: (i, j, 0)),
        out_shape=jax.ShapeDtypeStruct((b, s, MIX_W), BF16),
        scratch_shapes=_exp_scratch(s + c, UNIT_Q) + [pltpu.VMEM((HEADS * VT_ROWS, s + c), BF16)],
        compiler_params=_params("arbitrary", "arbitrary"),
        name="diff",
    )(zl, zl, zl, zc, zc, lp, gain)


def _swa_kernel(q_ref, k_ref, v_ref, kc_ref, vc_ref, sink_ref, o_ref, e_scr, mask_scr):
    s_len, c_len = k_ref.shape[1], kc_ref.shape[1]
    n_qt = q_ref.shape[1] // UNIT_Q
    win = UNIT_Q + 2 * SWA_WINDOW
    chunks = _key_chunks(win, WINDOW_CHUNK) + [(win + o, r) for o, r in _key_chunks(c_len, WINDOW_CHUNK)]
    starts = []
    for qt in range(n_qt):
        q0 = (pl.program_id(1) * n_qt + qt) * UNIT_Q
        start = pl.multiple_of(jnp.clip(q0 - SWA_WINDOW, 0, s_len - win), SWA_WINDOW)
        kpos = start + lax.broadcasted_iota(jnp.int32, (win, UNIT_Q), 0)
        qpos = q0 + lax.broadcasted_iota(jnp.int32, (win, UNIT_Q), 1)
        mask_scr[qt] = jnp.where(jnp.abs(kpos - qpos) <= SWA_WINDOW, 0.0, NEG_INF)
        starts.append(start)
    units = [(qt, w) for qt in range(n_qt) for w in range(HEADS)]
    qs = [_lane_mask(q_ref[0, qt * UNIT_Q:(qt + 1) * UNIT_Q, (w // SWA_HKV) * LANES:(w // SWA_HKV + 1) * LANES],
                     (w % SWA_HKV) * HEAD_DIM, HEAD_DIM) for qt, w in units]
    head_of = lambda w: (w % SWA_HKV) * SWA_GROUP + w // SWA_HKV

    def score_of(u, c):
        off, rows = chunks[c]
        qt = units[u][0]
        if off < win:
            k = k_ref[0, pl.ds(starts[qt] + off, rows), :]
            return _dot_nt(k, qs[u]) + mask_scr[qt, off:off + rows, :]
        return _dot_nt(kc_ref[0, off - win:off - win + rows, :], qs[u])

    def extra_of(u):
        return jnp.full((1, 1), sink_ref[head_of(units[u][1])] * LOG2E, F32)

    def pv_of(u, c, e):
        qt, w = units[u]
        off, rows = chunks[c]
        if off < win:
            o = _dot_tn(v_ref[0, pl.ds(starts[qt] + off, rows), :], e)
        else:
            o = _dot_tn(vc_ref[0, off - win:off - win + rows, :], e)
        return o[(w % SWA_HKV) * HEAD_DIM:(w % SWA_HKV + 1) * HEAD_DIM]

    outs = _localmax_batched(len(units), chunks, score_of, extra_of, pv_of, e_scr)
    for qt in range(n_qt):
        by_head = {head_of(w): outs[qt * HEADS + w] for w in range(HEADS)}
        o = jnp.concatenate([by_head[h] for h in range(HEADS)], axis=0)
        o_ref[0, qt * UNIT_Q:(qt + 1) * UNIT_Q, :] = o.T.astype(BF16)


def _swa_call(zl, zc, sink, tq):
    b, s, _ = zl.shape
    c = zc.shape[1]
    win = UNIT_Q + 2 * SWA_WINDOW
    return pl.pallas_call(
        _swa_kernel,
        grid=(b, s // tq),
        in_specs=[
            pl.BlockSpec((1, tq, MIX_W), lambda i, j: (i, j, ZC_SQ // MIX_W)),
            pl.BlockSpec((1, s, LANES), lambda i, j: (i, 0, ZC_SK // LANES)),
            pl.BlockSpec((1, s, LANES), lambda i, j: (i, 0, ZC_SV // LANES)),
            pl.BlockSpec((1, c, LANES), lambda i, j: (i, 0, ZC_SK // LANES)),
            pl.BlockSpec((1, c, LANES), lambda i, j: (i, 0, ZC_SV // LANES)),
            pl.BlockSpec(memory_space=pltpu.SMEM),
        ],
        out_specs=pl.BlockSpec((1, tq, MIX_W), lambda i, j: (i, j, 0)),
        out_shape=jax.ShapeDtypeStruct((b, s, MIX_W), BF16),
        scratch_shapes=(_exp_scratch(win + c, UNIT_Q, HEADS * tq // UNIT_Q)
                        + [pltpu.VMEM((tq // UNIT_Q, win, UNIT_Q), F32)]),
        compiler_params=_params("arbitrary", "arbitrary"),
        name="swa",
    )(zl, zl, zl, zc, zc, sink)


def _na_kernel(rows_n, q_ref, k_ref, v_ref, kc_ref, vc_ref, bias_ref, o_ref, e_scr):
    c_len = kc_ref.shape[1]
    n_rb = q_ref.shape[1] // NA_TQ
    nrb = rows_n // 2
    chunks = _key_chunks(NA_NLOC, KEY_CHUNK) + [(NA_NLOC + o, r) for o, r in _key_chunks(c_len, KEY_CHUNK)]
    starts, tids = [], []
    for i in range(n_rb):
        rb = pl.program_id(1) * n_rb + i
        starts.append(pl.multiple_of(
            jnp.clip(2 * rb - NA_WIN_R // 2, 0, rows_n - NA_KROWS) * GRID_W, 2 * GRID_W))
        tids.append(jnp.minimum(rb, 2) + jnp.maximum(rb - (nrb - 3), 0))
    blk = lambda h: slice((h // 2) * LANES, (h // 2 + 1) * LANES)
    units = [(i, h) for i in range(n_rb) for h in range(HEADS)]
    qs = [_lane_mask(q_ref[0, i * NA_TQ:(i + 1) * NA_TQ, blk(h)], (h % 2) * HEAD_DIM, HEAD_DIM)
          for i, h in units]

    def score_of(u, c):
        off, rows = chunks[c]
        i, h = units[u]
        if off < NA_NLOC:
            k = k_ref[0, pl.ds(starts[i] + off, rows), blk(h)]
            return _dot_nt(k, qs[u]) + bias_ref[h, tids[i], off:off + rows, :]
        return _dot_nt(kc_ref[0, off - NA_NLOC:off - NA_NLOC + rows, blk(h)], qs[u])

    def pv_of(u, c, e):
        i, h = units[u]
        off, rows = chunks[c]
        if off < NA_NLOC:
            return _head_rows(_dot_tn(v_ref[0, pl.ds(starts[i] + off, rows), blk(h)], e), h)
        return _head_rows(_dot_tn(vc_ref[0, off - NA_NLOC:off - NA_NLOC + rows, blk(h)], e), h)

    outs = _localmax_batched(len(units), chunks, score_of, lambda u: None, pv_of, e_scr)
    for i in range(n_rb):
        o = jnp.concatenate(outs[i * HEADS:(i + 1) * HEADS], axis=0)
        o_ref[0, i * NA_TQ:(i + 1) * NA_TQ, :] = o.T.astype(BF16)


def _na_call(zl, zc, bias, tq):
    b, s, _ = zl.shape
    c = zc.shape[1]
    rows_n = s // GRID_W
    return pl.pallas_call(
        functools.partial(_na_kernel, rows_n),
        grid=(b, s // tq),
        in_specs=[
            pl.BlockSpec((1, tq, MIX_W), lambda i, j: (i, j, ZC_NQ // MIX_W)),
            pl.BlockSpec((1, s, MIX_W), lambda i, j: (i, 0, ZC_NK // MIX_W)),
            pl.BlockSpec((1, s, MIX_W), lambda i, j: (i, 0, ZC_NV // MIX_W)),
            pl.BlockSpec((1, c, MIX_W), lambda i, j: (i, 0, ZC_NK // MIX_W)),
            pl.BlockSpec((1, c, MIX_W), lambda i, j: (i, 0, ZC_NV // MIX_W)),
            pl.BlockSpec(bias.shape, lambda i, j: (0, 0, 0, 0), pipeline_mode=pl.Buffered(1)),
        ],
        out_specs=pl.BlockSpec((1, tq, MIX_W), lambda i, j: (i, j, 0)),
        out_shape=jax.ShapeDtypeStruct((b, s, MIX_W), BF16),
        scratch_shapes=_exp_scratch(NA_NLOC + c, NA_TQ, HEADS * tq // NA_TQ),
        compiler_params=_params("arbitrary", "arbitrary"),
        name="na",
    )(zl, zl, zl, zc, zc, bias)


def _ctx_kernel(lam_init, z_ref, lp_ref, gain_ref, sink_ref, oa_ref, ob_ref, oc_ref, od_ref,
                e_scr):
    c_len = z_ref.shape[1]

    def blk(col, j=0):
        return z_ref[0, :, col + j * LANES:col + (j + 1) * LANES]

    units = []
    for h in range(HEADS):
        units.append((_lane_mask(blk(ZC_NQ, h // 2), (h % 2) * HEAD_DIM, HEAD_DIM),
                      ZC_NK + (h // 2) * LANES, ZC_NV + (h // 2) * LANES, (h % 2) * HEAD_DIM, None))
    for h in range(HEADS):
        units.append((blk(ZC_MQ, h), ZC_MK + h * LANES, ZC_MV + (h // 2) * LANES,
                      (h % 2) * HEAD_DIM, None))
    for h in range(HEADS):
        hk, g = h // SWA_GROUP, h % SWA_GROUP
        units.append((_lane_mask(blk(ZC_SQ, g), hk * HEAD_DIM, HEAD_DIM), ZC_SK, ZC_SV,
                      hk * HEAD_DIM, h))
    for w in range(2 * HEADS):
        units.append((_lane_mask(blk(ZC_DQ, w // 4), (w % 4) * DIFF_DH, DIFF_DH),
                      ZC_DK + (w // 4) * LANES, ZC_DV + (w // 4) * LANES, ((w // 2) % 2) * HEAD_DIM, None))

    def score_of(u, c):
        q_m, kcol = units[u][0], units[u][1]
        return _dot_nt(z_ref[0, :, kcol:kcol + LANES], q_m)

    def extra_of(u):
        h = units[u][4]
        return None if h is None else jnp.full((1, 1), sink_ref[h] * LOG2E, F32)

    def pv_of(u, c, e):
        vcol, r0 = units[u][2], units[u][3]
        return _dot_tn(z_ref[0, :, vcol:vcol + LANES], e)[r0:r0 + HEAD_DIM]

    outs = _localmax_batched(len(units), [(0, c_len)], score_of, extra_of, pv_of, e_scr)
    for i, ref in enumerate((oa_ref, ob_ref, oc_ref)):
        ref[0] = jnp.concatenate(outs[i * HEADS:(i + 1) * HEADS], axis=0).T.astype(BF16)
    o = _diff_combine(outs[3 * HEADS:], _diff_lambda(lp_ref[...], lam_init))
    od_ref[0] = (o.T * gain_ref[...]).astype(BF16)


def _ctx_call(zc, lp, gain, sink, lam_init):
    b, c, _ = zc.shape
    out = jax.ShapeDtypeStruct((b, c, MIX_W), BF16)
    ospec = pl.BlockSpec((1, c, MIX_W), lambda i: (i, 0, 0))
    return pl.pallas_call(
        functools.partial(_ctx_kernel, lam_init),
        grid=(b,),
        in_specs=[
            pl.BlockSpec((1, c, Z_COLS), lambda i: (i, 0, 0)),
            pl.BlockSpec(lp.shape, lambda i: (0, 0)),
            pl.BlockSpec(gain.shape, lambda i: (0, 0)),
            pl.BlockSpec(memory_space=pltpu.SMEM),
        ],
        out_specs=[ospec] * 4,
        out_shape=[out] * 4,
        scratch_shapes=_exp_scratch(c, c, 5 * HEADS),
        compiler_params=_params("arbitrary"),
        name="ctx_attn",
    )(zc, lp, gain, sink)


def _mlp_kernel(final, x_ref, ma_ref, mb_ref, mc_ref, md_ref, mod_ref, g_ref, gf_ref,
                wout_ref, wup_ref, wdn_ref, o_ref):
    x = x_ref[0]
    mod = mod_ref[0]
    attn = None
    for i, m_ref in enumerate((ma_ref, mb_ref, mc_ref, md_ref)):
        part = _dot(m_ref[0], wout_ref[i * MIX_W:(i + 1) * MIX_W, :])
        attn = part if attn is None else attn + part
    x = x + mod[2:3] * attn
    h = x * _rms_scale(x, D_MODEL) * g_ref[...]
    h = (h * (1.0 + mod[4:5]) + mod[3:4]).astype(BF16)
    acc = None
    ck = FF_CHUNK
    for c in range(D_FF // ck):
        u = jnp.maximum(_dot(h, wup_ref[:, c * ck:(c + 1) * ck]), 0.0)
        part = _dot((u * u).astype(BF16), wdn_ref[c * ck:(c + 1) * ck, :])
        acc = part if acc is None else acc + part
    x = x + mod[5:6] * acc
    if final:
        x = x * _rms_scale(x, D_MODEL) * gf_ref[...]
    o_ref[0] = x


def _mlp_call(x3, mixes, mod, mod_row, g, gf, wout, wup, wdn, tm, final):
    nb, n, d = x3.shape
    const = lambda t, b: (0, 0)
    tok = lambda t, b: (b, t, 0)
    mod_map = (lambda t, b: (b, 0, 0)) if mod_row is None else (lambda t, b: (mod_row, 0, 0))
    single = pl.Buffered(1)
    return pl.pallas_call(
        functools.partial(_mlp_kernel, final),
        grid=(n // tm, nb),
        in_specs=[pl.BlockSpec((1, tm, d), tok)]
        + [pl.BlockSpec((1, tm, MIX_W), tok)] * 4
        + [
            pl.BlockSpec((1, 6, d), mod_map),
            pl.BlockSpec((1, d), const),
            pl.BlockSpec((1, d), const),
            pl.BlockSpec(wout.shape, const, pipeline_mode=single),
            pl.BlockSpec(wup.shape, const, pipeline_mode=single),
            pl.BlockSpec(wdn.shape, const, pipeline_mode=single),
        ],
        out_specs=pl.BlockSpec((1, tm, d), tok),
        out_shape=jax.ShapeDtypeStruct((nb, n, d), F32),
        compiler_params=_params("arbitrary", "arbitrary"),
        name="mlp",
    )(x3, *mixes, mod, g, gf, wout, wup, wdn)


def _rope_tables(s_len):
    t = np.arange(s_len)
    row, col = t // GRID_W, t % GRID_W

    def axial(dim):
        n_freq = dim // 4
        freqs = jnp.asarray(ROPE_BASE, F32) ** (-jnp.arange(n_freq, dtype=F32) / n_freq)
        ang = jnp.concatenate([jnp.asarray(row, F32)[:, None] * freqs,
                               jnp.asarray(col, F32)[:, None] * freqs], axis=-1)
        return jnp.cos(ang), jnp.sin(ang)

    def group(cs, n_groups, scale):
        cos, sin = cs
        c = jnp.concatenate([cos, cos], axis=-1) * scale
        s = jnp.concatenate([-sin, sin], axis=-1) * scale
        return jnp.tile(c, (1, n_groups)), jnp.tile(s, (1, n_groups))

    r64, r32 = axial(HEAD_DIM), axial(MLA_ROPE)
    ones = lambda n, v=1.0: jnp.full((s_len, n), v, F32)
    zeros = lambda n: jnp.zeros((s_len, n), F32)
    mla_scale = (MLA_NOPE + MLA_ROPE) ** -0.5 * LOG2E
    mq_c, mq_s = group(r32, 1, mla_scale)
    parts = [
        group(r64, 4, HEAD_DIM ** -0.5 * LOG2E), group(r64, 2, 1.0),
        group(r32, 8, DIFF_DH ** -0.5 * LOG2E), group(r32, 8, 1.0),
        (ones(MLA_Q_LORA), zeros(MLA_Q_LORA)), group(r32, 1, 1.0), (ones(32), zeros(32)),
    ] + [(jnp.concatenate([ones(MLA_NOPE, mla_scale), mq_c, ones(32, mla_scale)], axis=-1),
          jnp.concatenate([zeros(MLA_NOPE), mq_s, zeros(32)], axis=-1))] * HEADS
    cos = jnp.concatenate([p[0] for p in parts], axis=-1)
    sin = jnp.concatenate([p[1] for p in parts], axis=-1)
    return cos, sin, cos[0:1]


def _permute_w_in(w_in):
    offs = np.cumsum([0, 256, 256, 256, MLA_Q_LORA, MLA_KV_LORA, MLA_ROPE, 256, 128, 128, 256, 256, 256])
    sec = lambda i: w_in[..., offs[i]:offs[i + 1]]
    sq = sec(6)
    sq = sq.reshape(sq.shape[:-1] + (SWA_HKV, SWA_GROUP, HEAD_DIM))
    sq = jnp.swapaxes(sq, -3, -2).reshape(w_in.shape[:-1] + (256,))
    pad = jnp.zeros(w_in.shape[:-1] + (32,), w_in.dtype)
    cols = [sq, sec(7), sec(9), sec(10), sec(4), sec(3), sec(5), pad,
            sec(0), sec(1), sec(2), sec(11), sec(8)]
    return jnp.concatenate(cols, axis=-1).astype(BF16)


def _mla_weights(w_uq, w_ukv):
    depth = w_uq.shape[0]
    uq = w_uq.reshape(depth, MLA_Q_LORA, HEADS, MLA_NOPE + MLA_ROPE)
    uq = jnp.pad(uq, ((0, 0), (0, 256 - MLA_Q_LORA), (0, 0), (0, LANES - MLA_NOPE - MLA_ROPE)))
    uq = uq.reshape(depth, 256, HEADS * LANES).astype(BF16)
    ukv = w_ukv.reshape(depth, MLA_KV_LORA, HEADS, 2, HEAD_DIM)
    uk = jnp.pad(ukv[:, :, :, 0], ((0, 0), (0, 0), (0, 0), (0, LANES - MLA_NOPE)))
    uk = uk.reshape(depth, MLA_KV_LORA, HEADS * LANES).astype(BF16)
    uv = ukv[:, :, :, 1].reshape(depth, MLA_KV_LORA, HEADS * HEAD_DIM).astype(BF16)
    return uq, uk, uv


def _na_bias_tables(rpb, rows_n):
    kr_n = min(NA_WIN_R, rows_n)
    col = np.arange(GRID_W)
    c0 = np.clip(col - NA_WIN_C // 2, 0, GRID_W - NA_WIN_C)
    col_ok_t = ((col[None, :] >= c0[:, None]) & (col[None, :] < c0[:, None] + NA_WIN_C)).T
    dc_idx_t = np.clip(col[:, None] - col[None, :], 1 - NA_WIN_C, NA_WIN_C - 1) + (NA_WIN_C - 1)
    toe = jnp.zeros(rpb.shape[:3] + (GRID_W, GRID_W), F32)
    for d in range(2 * NA_WIN_C - 1):
        toe = jnp.where(dc_idx_t == d, rpb[..., d, None, None].astype(F32), toe)
    toe = jnp.where(col_ok_t, toe * LOG2E, NEG_INF)
    neg = jnp.full(toe.shape[:2] + (GRID_W, GRID_W), NEG_INF, F32)
    nrb = rows_n // 2
    pairs = {}

    def pair(drs):
        if drs not in pairs:
            pairs[drs] = jnp.concatenate([neg if d is None else toe[:, :, d] for d in drs], axis=-1)
        return pairs[drs]

    blocks = []
    for rb in (0, 1, 2, nrb - 2, nrb - 1):
        start = int(np.clip(2 * rb - NA_WIN_R // 2, 0, rows_n - NA_KROWS))
        for j in range(NA_KROWS):
            drs = []
            for i in range(2):
                kr, qr = start + j, 2 * rb + i
                r0 = int(np.clip(qr - kr_n // 2, 0, rows_n - kr_n))
                drs.append(kr - qr + NA_WIN_R - 1 if r0 <= kr < r0 + kr_n else None)
            blocks.append(pair(tuple(drs)))
    tab = jnp.stack(blocks, axis=2)
    return tab.reshape(tab.shape[:2] + (5, NA_NLOC, NA_TQ))


def kernel(x, c, ctx, c_ctx, w_ada, b_ada, norm_attn_g, w_in, na_rpb, mla_q_norm_g, mla_w_uq,
           mla_kv_norm_g, mla_w_ukv, swa_sink, diff_lambda, diff_norm_g, w_out, norm_mlp_g,
           w_up, w_down, final_norm_g):
    b, s, d = x.shape
    c_len = ctx.shape[1]
    depth = w_in.shape[0]
    rows_n = s // GRID_W
    assert d == D_MODEL and s % max(ATTN_TQ, DIFF_TQ, TOKEN_TILE) == 0 and rows_n >= NA_KROWS
    assert c_len % KEY_CHUNK == 0 and (b * c_len) % min(TOKEN_TILE, b * c_len) == 0

    w_in_p = _permute_w_in(w_in)
    uq, uk, uv = _mla_weights(mla_w_uq, mla_w_ukv)
    gq = jnp.pad(mla_q_norm_g, ((0, 0), (0, 256 - MLA_Q_LORA)))[:, None, :]
    gkv = mla_kv_norm_g[:, None, :]
    w_out_b, w_up_b, w_dn_b = w_out.astype(BF16), w_up.astype(BF16), w_down.astype(BF16)
    na_bias = _na_bias_tables(na_rpb, rows_n)
    cos, sin, cos_ctx = _rope_tables(s)
    tm_l, tm_c = TOKEN_TILE, min(TOKEN_TILE, b * c_len)
    cos_c = jnp.broadcast_to(cos_ctx, (tm_c, T_COLS))
    sin_c = jnp.zeros((tm_c, T_COLS), F32)

    c_rows = -(-(b + 1) // 8) * 8
    c_all = jnp.zeros((c_rows, d), F32).at[:b].set(c).at[b].set(c_ctx)
    mod = _ada_call(c_all, w_ada, b_ada).reshape(depth, c_rows, 6, d)

    xl = x
    xc = ctx.reshape(1, b * c_len, d)
    for l in range(depth):
        need_ctx = l < depth - 1
        lam_init = 0.8 - 0.6 * math.exp(-0.3 * l)
        g_attn = norm_attn_g[l][None, :]
        g_mlp = norm_mlp_g[l][None, :]
        gf = final_norm_g[None, :]
        d_gain = (jnp.tile(diff_norm_g[l], HEADS) * (1.0 - lam_init))[None, :]
        proj = (w_in_p[l], gq[l], gkv[l], uq[l], uk[l], uv[l])
        zl = _inproj_call(xl, mod[l], None, g_attn, cos, sin, True, *proj, tm=tm_l)
        zc = _inproj_call(xc, mod[l], b, g_attn, cos_c, sin_c, False, *proj, tm=tm_c)
        zc = zc.reshape(b, c_len, Z_COLS)
        mixes = (
            _na_call(zl, zc, na_bias[l], tq=ATTN_TQ),
            _mla_call(zl, zc, tq=ATTN_TQ),
            _swa_call(zl, zc, swa_sink[l], tq=ATTN_TQ),
            _diff_call(zl, zc, diff_lambda[l], d_gain, lam_init, tq=DIFF_TQ),
        )
        wts = (w_out_b[l], w_up_b[l], w_dn_b[l])
        xl = _mlp_call(xl, mixes, mod[l], None, g_mlp, gf, *wts, tm=tm_l, final=not need_ctx)
        if need_ctx:
            mixes_c = _ctx_call(zc, diff_lambda[l], d_gain, swa_sink[l], lam_init)
            mixes_c = [m.reshape(1, b * c_len, MIX_W) for m in mixes_c]
            xc = _mlp_call(xc, mixes_c, mod[l], b, g_mlp, gf, *wts, tm=tm_c, final=False)
    return xl
```

```python
import functools
import math

import jax
import jax.numpy as jnp
import numpy as np
from jax import lax
from jax.experimental import pallas as pl
from jax.experimental.pallas import tpu as pltpu

F32 = jnp.float32
BF16 = jnp.bfloat16

D_MODEL = 1024
GRID_W = 64
HEADS = 4
HEAD_DIM = 64
NA_WIN_R = 8
NA_WIN_C = 16
MLA_Q_LORA = 192
MLA_KV_LORA = 128
MLA_NOPE = 64
MLA_ROPE = 32
SWA_HKV = 2
SWA_GROUP = 2
SWA_WINDOW = 128
DIFF_DH = 32
D_FF = 4 * D_MODEL
ROPE_BASE = 10000.0
EPS = 1e-6
NEG_INF = -1e30
LOG2E = math.log2(math.e)

LANES = 128
VMEM_LIMIT = 56 * 1024 * 1024

ZC_DQ, ZC_DK, ZC_SQ, ZC_NQ, ZC_NK, ZC_NV, ZC_DV, ZC_MV = 0, 256, 512, 768, 1024, 1280, 1536, 1792
ZC_MQ, ZC_MK, ZC_SK, ZC_SV = 2048, 2560, 3072, 3200
Z_COLS = 3328
W_R64, W_R32, W_M, W_N, W_COLS = 0, 384, 896, 1280, 2432
T_R64, T_R32, T_B12, T_MQ, T_COLS = 0, 384, 896, 1152, 1664

NA_KROWS = 10
NA_TQ = 2 * GRID_W
NA_NLOC = NA_KROWS * GRID_W
KEY_CHUNK = 256
WINDOW_CHUNK = 512
UNIT_Q = 256
MIX_W = HEADS * HEAD_DIM
MLA_W = HEADS * LANES

TOKEN_TILE = 512
ATTN_TQ = 1024
DIFF_TQ = 1024
ADA_TN = 1536
FF_CHUNK = 1024


def _params(*sem):
    return pltpu.CompilerParams(dimension_semantics=sem, vmem_limit_bytes=VMEM_LIMIT)


def _dot(a, b):
    return jnp.dot(a, b, preferred_element_type=F32)


def _dot_nt(a, b):
    return lax.dot_general(a, b, (((1,), (1,)), ((), ())), preferred_element_type=F32)


def _dot_tn(a, b):
    return lax.dot_general(a, b, (((0,), (0,)), ((), ())), preferred_element_type=F32)


def _rms_scale(x, n):
    return lax.rsqrt(jnp.sum(x * x, axis=-1, keepdims=True) * (1.0 / n) + EPS)


def _ada_kernel(c_ref, w_ref, b_ref, o_ref):
    c = c_ref[...]
    act = (c * (1.0 / (1.0 + jnp.exp(-c)))).astype(BF16)
    o_ref[0] = _dot(act, w_ref[0].astype(BF16)) + b_ref[0]


def _ada_call(c_all, w_ada, b_ada):
    depth, d, n = w_ada.shape
    rows = c_all.shape[0]
    tn = ADA_TN
    return pl.pallas_call(
        _ada_kernel,
        grid=(depth, n // tn),
        in_specs=[
            pl.BlockSpec((rows, d), lambda l, j: (0, 0)),
            pl.BlockSpec((1, d, tn), lambda l, j: (l, 0, j)),
            pl.BlockSpec((1, 1, tn), lambda l, j: (l, 0, j)),
        ],
        out_specs=pl.BlockSpec((1, rows, tn), lambda l, j: (l, 0, j)),
        out_shape=jax.ShapeDtypeStruct((depth, rows, n), F32),
        compiler_params=_params("arbitrary", "arbitrary"),
        name="ada",
    )(c_all, w_ada, b_ada.reshape(depth, 1, n))


def _rope_blocks(z, cos_ref, sin_ref, tcol, half):
    tm = z.shape[0]
    lane = lax.broadcasted_iota(jnp.int32, (tm, LANES), 1)
    first = (lane % (2 * half)) < half
    out = []
    for j in range(z.shape[1] // LANES):
        xb = z[:, j * LANES:(j + 1) * LANES]
        rot = jnp.where(first, pltpu.roll(xb, LANES - half, 1), pltpu.roll(xb, half, 1))
        c0 = tcol + j * LANES
        out.append(xb * cos_ref[:, c0:c0 + LANES] + rot * sin_ref[:, c0:c0 + LANES])
    return out


def _inproj_kernel(x_ref, mod_ref, g_ref, cos_ref, sin_ref, w_ref, gq_ref, gkv_ref,
                   wuq_ref, wukvk_ref, wukvv_ref, o_ref):
    x = x_ref[0]
    mod = mod_ref[0]
    h = x * _rms_scale(x, D_MODEL) * g_ref[...]
    h = (h * (1.0 + mod[1:2]) + mod[0:1]).astype(BF16)

    def put(col, val):
        o_ref[0, :, col:col + val.shape[1]] = val.astype(BF16)

    z64 = _dot(h, w_ref[:, W_R64:W_R32])
    z32 = _dot(h, w_ref[:, W_R32:W_M])
    r = _rope_blocks(z64, cos_ref, sin_ref, T_R64, HEAD_DIM // 2)
    put(ZC_SQ, r[0]); put(ZC_SQ + LANES, r[1]); put(ZC_SK, r[2])
    zm = _dot(h, w_ref[:, W_M:W_N])
    r = _rope_blocks(z32, cos_ref, sin_ref, T_R32, DIFF_DH // 2)
    for j in range(4):
        put(ZC_DQ + j * LANES, r[j])
    zn = _dot(h, w_ref[:, W_N:W_COLS])

    ckv = zm[:, 0:MLA_KV_LORA]
    b12 = zm[:, MLA_KV_LORA:]
    ckv_n = (ckv * _rms_scale(ckv, MLA_KV_LORA) * gkv_ref[...]).astype(BF16)
    lane = lax.broadcasted_iota(jnp.int32, b12.shape, 1)
    cq = jnp.where(lane < MLA_Q_LORA, b12, 0.0)
    cq_n = (cq * _rms_scale(cq, MLA_Q_LORA) * gq_ref[...]).astype(BF16)
    zq = _dot(cq_n, wuq_ref[...])
    put(ZC_NQ, zn[:, 0:256] * (HEAD_DIM ** -0.5 * LOG2E))
    put(ZC_NK, zn[:, 256:1024])
    put(ZC_SV, zn[:, 1024:1152])
    q = _rope_blocks(zq, cos_ref, sin_ref, T_MQ, MLA_ROPE // 2)
    for j in range(HEADS):
        put(ZC_MQ + j * LANES, q[j])
    kr = _rope_blocks(b12[:, LANES:], cos_ref, sin_ref, T_B12 + LANES, MLA_ROPE // 2)[0]
    lane = lax.broadcasted_iota(jnp.int32, kr.shape, 1)
    kr = jnp.where(jnp.where(lane >= MLA_NOPE, lane, LANES) < MLA_NOPE + MLA_ROPE, kr, 0.0)
    kn = _dot(ckv_n, wukvk_ref[...])
    for j in range(HEADS):
        put(ZC_MK + j * LANES, kn[:, j * LANES:(j + 1) * LANES] + kr)
    put(ZC_MV, _dot(ckv_n, wukvv_ref[...]))


def _inproj_call(x3, mod, mod_row, g, cos, sin, table_per_tile, w, gq, gkv, wuq, wukvk, wukvv, tm):
    nb, n, d = x3.shape
    nt = n // tm
    const = lambda t, b: (0, 0)
    tab_map = (lambda t, b: (t, 0)) if table_per_tile else const
    mod_map = (lambda t, b: (b, 0, 0)) if mod_row is None else (lambda t, b: (mod_row, 0, 0))
    return pl.pallas_call(
        _inproj_kernel,
        grid=(nt, nb),
        in_specs=[
            pl.BlockSpec((1, tm, d), lambda t, b: (b, t, 0)),
            pl.BlockSpec((1, 6, d), mod_map),
            pl.BlockSpec((1, d), const),
            pl.BlockSpec((tm, T_COLS), tab_map),
            pl.BlockSpec((tm, T_COLS), tab_map),
            pl.BlockSpec(w.shape, const, pipeline_mode=pl.Buffered(1)),
            pl.BlockSpec(gq.shape, const),
            pl.BlockSpec(gkv.shape, const),
            pl.BlockSpec(wuq.shape, const),
            pl.BlockSpec(wukvk.shape, const),
            pl.BlockSpec(wukvv.shape, const),
        ],
        out_specs=pl.BlockSpec((1, tm, Z_COLS), lambda t, b: (b, t, 0)),
        out_shape=jax.ShapeDtypeStruct((nb, n, Z_COLS), BF16),
        compiler_params=_params("arbitrary", "arbitrary"),
        name="inproj",
    )(x3, mod, g, cos, sin, w, gq, gkv, wuq, wukvk, wukvv)


def _lane_mask(q, lo, width):
    lane = lax.broadcasted_iota(jnp.int32, q.shape, 1)
    keep = jnp.where(lane >= lo, lane, LANES) < lo + width
    return q * jnp.where(keep, 1.0, 0.0).astype(BF16)


def _localmax_softmax(n_units, chunks, score_of, extra_of, pv_of, e_scr):
    outs = [None] * n_units
    mloc, mfin = {}, {}
    n_c = len(chunks)
    for t in range(n_units + 1):
        acc = None
        if t < n_units:
            mloc[t] = []
        for cc in range(n_c + 1):
            if t < n_units and cc < n_c:
                off, rows = chunks[cc]
                s = score_of(t, cc)
                mc = jnp.max(s, axis=0, keepdims=True)
                e_scr[t % 2, off:off + rows, :] = jnp.exp2(s - mc).astype(BF16)
                mloc[t].append(mc)
            if t >= 1 and cc >= 1:
                u, c = t - 1, cc - 1
                off, rows = chunks[c]
                pv = pv_of(u, c, e_scr[u % 2, off:off + rows, :]) * jnp.exp2(mloc[u][c] - mfin[u])
                acc = pv if acc is None else acc + pv
        if t < n_units:
            m = mloc[t][0]
            for mc in mloc[t][1:]:
                m = jnp.maximum(m, mc)
            x = extra_of(t)
            mfin[t] = m if x is None else jnp.maximum(m, x)
        if t >= 1:
            l = acc[HEAD_DIM:HEAD_DIM + 1]
            x = extra_of(t - 1)
            if x is not None:
                l = l + jnp.exp2(x - mfin[t - 1])
            outs[t - 1] = acc[0:HEAD_DIM] * (1.0 / l)
    return outs


def _localmax_batched(n_units, chunks, score_of, extra_of, pv_of, e_scr):
    mloc, lloc, mfin = [], [], []
    for u in range(n_units):
        ms, ls = [], []
        for c, (off, rows) in enumerate(chunks):
            s = score_of(u, c)
            mc = jnp.max(s, axis=0, keepdims=True)
            e = jnp.exp2(s - mc)
            ls.append(jnp.sum(e, axis=0, keepdims=True))
            e_scr[u, off:off + rows, :] = e.astype(BF16)
            ms.append(mc)
        m = ms[0]
        for mc in ms[1:]:
            m = jnp.maximum(m, mc)
        x = extra_of(u)
        mloc.append(ms); lloc.append(ls); mfin.append(m if x is None else jnp.maximum(m, x))
    outs = []
    for u in range(n_units):
        acc = l = None
        for c, (off, rows) in enumerate(chunks):
            alpha = jnp.exp2(mloc[u][c] - mfin[u])
            pv = pv_of(u, c, e_scr[u, off:off + rows, :]) * alpha
            acc = pv if acc is None else acc + pv
            l = lloc[u][c] * alpha if l is None else l + lloc[u][c] * alpha
        x = extra_of(u)
        if x is not None:
            l = l + jnp.exp2(x - mfin[u])
        outs.append(acc * (1.0 / l))
    return outs


def _head_rows(o, h):
    return o[(h % 2) * HEAD_DIM:(h % 2 + 1) * HEAD_DIM]


def _diff_lambda(lp, lam_init):
    a = jnp.sum(lp[0:1] * lp[1:2], axis=-1, keepdims=True)
    b = jnp.sum(lp[2:3] * lp[3:4], axis=-1, keepdims=True)
    return jnp.exp(a) - jnp.exp(b) + lam_init


def _diff_combine(outs, lam):
    res = []
    for h in range(HEADS):
        o = outs[2 * h] - lam * outs[2 * h + 1]
        ms = jnp.sum(o * o, axis=0, keepdims=True) * (1.0 / HEAD_DIM)
        res.append(o * lax.rsqrt(ms + EPS))
    return jnp.concatenate(res, axis=0)


def _key_chunks(total, size):
    return [(o, min(size, total - o)) for o in range(0, total, size)]


def _exp_scratch(nk, tq, slots=2):
    return [pltpu.VMEM((slots, nk, tq), BF16)]


VT_ROWS = HEAD_DIM + 16


def _transpose_values(vt_scr, srcs):
    for ref, off, n in srcs:
        for r in range(0, n, KEY_CHUNK):
            rows = min(KEY_CHUNK, n - r)
            vt = ref[0, r:r + rows, :].astype(F32).T
            for h in range(HEADS):
                vt_scr[h * VT_ROWS:h * VT_ROWS + HEAD_DIM, off + r:off + r + rows] = (
                    vt[h * HEAD_DIM:(h + 1) * HEAD_DIM].astype(BF16))
                vt_scr[h * VT_ROWS + HEAD_DIM:(h + 1) * VT_ROWS, off + r:off + r + rows] = (
                    jnp.ones((VT_ROWS - HEAD_DIM, rows), BF16))


def _mla_kernel(q_ref, k_ref, v_ref, kc_ref, vc_ref, o_ref, e_scr, vt_scr):
    s_len, c_len = k_ref.shape[1], kc_ref.shape[1]

    @pl.when(pl.program_id(1) == 0)
    def _():
        _transpose_values(vt_scr, [(v_ref, 0, s_len), (vc_ref, s_len, c_len)])

    chunks = _key_chunks(s_len, KEY_CHUNK) + [(s_len + o, r) for o, r in _key_chunks(c_len, KEY_CHUNK)]
    units = [(qt, h) for qt in range(q_ref.shape[1] // UNIT_Q) for h in range(HEADS)]
    qs = [q_ref[0, qt * UNIT_Q:(qt + 1) * UNIT_Q, h * LANES:(h + 1) * LANES] for qt, h in units]

    def score_of(u, c):
        off, rows = chunks[c]
        h = units[u][1]
        sl = slice(h * LANES, (h + 1) * LANES)
        if off < s_len:
            return _dot_nt(k_ref[0, off:off + rows, sl], qs[u])
        return _dot_nt(kc_ref[0, off - s_len:off - s_len + rows, sl], qs[u])

    def pv_of(u, c, e):
        h = units[u][1]
        off, rows = chunks[c]
        return _dot(vt_scr[h * VT_ROWS:(h + 1) * VT_ROWS, off:off + rows], e)

    outs = _localmax_softmax(len(units), chunks, score_of, lambda u: None, pv_of, e_scr)
    for qt in range(len(units) // HEADS):
        o = jnp.concatenate(outs[qt * HEADS:(qt + 1) * HEADS], axis=0)
        o_ref[0, qt * UNIT_Q:(qt + 1) * UNIT_Q, :] = o.T.astype(BF16)


def _mla_call(zl, zc, tq):
    b, s, _ = zl.shape
    c = zc.shape[1]
    return pl.pallas_call(
        _mla_kernel,
        grid=(b, s // tq),
        in_specs=[
            pl.BlockSpec((1, tq, MLA_W), lambda i, j: (i, j, ZC_MQ // MLA_W)),
            pl.BlockSpec((1, s, MLA_W), lambda i, j: (i, 0, ZC_MK // MLA_W)),
            pl.BlockSpec((1, s, MIX_W), lambda i, j: (i, 0, ZC_MV // MIX_W)),
            pl.BlockSpec((1, c, MLA_W), lambda i, j: (i, 0, ZC_MK // MLA_W)),
            pl.BlockSpec((1, c, MIX_W), lambda i, j: (i, 0, ZC_MV // MIX_W)),
        ],
        out_specs=pl.BlockSpec((1, tq, MIX_W), lambda i, j: (i, j, 0)),
        out_shape=jax.ShapeDtypeStruct((b, s, MIX_W), BF16),
        scratch_shapes=_exp_scratch(s + c, UNIT_Q) + [pltpu.VMEM((HEADS * VT_ROWS, s + c), BF16)],
        compiler_params=_params("arbitrary", "arbitrary"),
        name="mla",
    )(zl, zl, zl, zc, zc)


def _diff_kernel(lam_init, q_ref, k_ref, v_ref, kc_ref, vc_ref, lp_ref, gain_ref, o_ref,
                 e_scr, vt_scr):
    s_len, c_len = k_ref.shape[1], kc_ref.shape[1]

    @pl.when(pl.program_id(1) == 0)
    def _():
        _transpose_values(vt_scr, [(v_ref, 0, s_len), (vc_ref, s_len, c_len)])

    chunks = _key_chunks(s_len, KEY_CHUNK) + [(s_len + o, r) for o, r in _key_chunks(c_len, KEY_CHUNK)]
    n_sub = 2 * HEADS
    units = [(qt, w) for qt in range(q_ref.shape[1] // UNIT_Q) for w in range(n_sub)]
    qs = [_lane_mask(q_ref[0, qt * UNIT_Q:(qt + 1) * UNIT_Q, (w // 4) * LANES:(w // 4 + 1) * LANES],
                     (w % 4) * DIFF_DH, DIFF_DH) for qt, w in units]

    def score_of(u, c):
        off, rows = chunks[c]
        w = units[u][1]
        sl = slice((w // 4) * LANES, (w // 4 + 1) * LANES)
        if off < s_len:
            return _dot_nt(k_ref[0, off:off + rows, sl], qs[u])
        return _dot_nt(kc_ref[0, off - s_len:off - s_len + rows, sl], qs[u])

    def pv_of(u, c, e):
        h = units[u][1] // 2
        off, rows = chunks[c]
        return _dot(vt_scr[h * VT_ROWS:(h + 1) * VT_ROWS, off:off + rows], e)

    outs = _localmax_softmax(len(units), chunks, score_of, lambda u: None, pv_of, e_scr)
    lam = _diff_lambda(lp_ref[...], lam_init)
    for qt in range(len(units) // n_sub):
        o = _diff_combine(outs[qt * n_sub:(qt + 1) * n_sub], lam)
        o_ref[0, qt * UNIT_Q:(qt + 1) * UNIT_Q, :] = (o.T * gain_ref[...]).astype(BF16)


def _diff_call(zl, zc, lp, gain, lam_init, tq):
    b, s, _ = zl.shape
    c = zc.shape[1]
    return pl.pallas_call(
        functools.partial(_diff_kernel, lam_init),
        grid=(b, s // tq),
        in_specs=[
            pl.BlockSpec((1, tq, MIX_W), lambda i, j: (i, j, ZC_DQ // MIX_W)),
            pl.BlockSpec((1, s, MIX_W), lambda i, j: (i, 0, ZC_DK // MIX_W)),
            pl.BlockSpec((1, s, MIX_W), lambda i, j: (i, 0, ZC_DV // MIX_W)),
            pl.BlockSpec((1, c, MIX_W), lambda i, j: (i, 0, ZC_DK // MIX_W)),
            pl.BlockSpec((1, c, MIX_W), lambda i, j: (i, 0, ZC_DV // MIX_W)),
            pl.BlockSpec(lp.shape, lambda i, j: (0, 0)),
            pl.BlockSpec(gain.shape, lambda i, j: (0, 0)),
        ],
        out_specs=pl.BlockSpec((1, tq, MIX_W), lambda i, j: (i, j, 0)),
        out_shape=jax.ShapeDtypeStruct((b, s, MIX_W), BF16),
        scratch_shapes=_exp_scratch(s + c, UNIT_Q) + [pltpu.VMEM((HEADS * VT_ROWS, s + c), BF16)],
        compiler_params=_params("arbitrary", "arbitrary"),
        name="diff",
    )(zl, zl, zl, zc, zc, lp, gain)


def _swa_kernel(q_ref, k_ref, v_ref, kc_ref, vc_ref, sink_ref, o_ref, e_scr, mask_scr):
    s_len, c_len = k_ref.shape[1], kc_ref.shape[1]
    n_qt = q_ref.shape[1] // UNIT_Q
    win = UNIT_Q + 2 * SWA_WINDOW
    chunks = _key_chunks(win, WINDOW_CHUNK) + [(win + o, r) for o, r in _key_chunks(c_len, WINDOW_CHUNK)]
    starts = []
    for qt in range(n_qt):
        q0 = (pl.program_id(1) * n_qt + qt) * UNIT_Q
        start = pl.multiple_of(jnp.clip(q0 - SWA_WINDOW, 0, s_len - win), SWA_WINDOW)
        kpos = start + lax.broadcasted_iota(jnp.int32, (win, UNIT_Q), 0)
        qpos = q0 + lax.broadcasted_iota(jnp.int32, (win, UNIT_Q), 1)
        mask_scr[qt] = jnp.where(jnp.abs(kpos - qpos) <= SWA_WINDOW, 0.0, NEG_INF)
        starts.append(start)
    units = [(qt, w) for qt in range(n_qt) for w in range(HEADS)]
    qs = [_lane_mask(q_ref[0, qt * UNIT_Q:(qt + 1) * UNIT_Q, (w // SWA_HKV) * LANES:(w // SWA_HKV + 1) * LANES],
                     (w % SWA_HKV) * HEAD_DIM, HEAD_DIM) for qt, w in units]
    head_of = lambda w: (w % SWA_HKV) * SWA_GROUP + w // SWA_HKV

    def score_of(u, c):
        off, rows = chunks[c]
        qt = units[u][0]
        if off < win:
            k = k_ref[0, pl.ds(starts[qt] + off, rows), :]
            return _dot_nt(k, qs[u]) + mask_scr[qt, off:off + rows, :]
        return _dot_nt(kc_ref[0, off - win:off - win + rows, :], qs[u])

    def extra_of(u):
        return jnp.full((1, 1), sink_ref[head_of(units[u][1])] * LOG2E, F32)

    def pv_of(u, c, e):
        qt, w = units[u]
        off, rows = chunks[c]
        if off < win:
            o = _dot_tn(v_ref[0, pl.ds(starts[qt] + off, rows), :], e)
        else:
            o = _dot_tn(vc_ref[0, off - win:off - win + rows, :], e)
        return o[(w % SWA_HKV) * HEAD_DIM:(w % SWA_HKV + 1) * HEAD_DIM]

    outs = _localmax_batched(len(units), chunks, score_of, extra_of, pv_of, e_scr)
    for qt in range(n_qt):
        by_head = {head_of(w): outs[qt * HEADS + w] for w in range(HEADS)}
        o = jnp.concatenate([by_head[h] for h in range(HEADS)], axis=0)
        o_ref[0, qt * UNIT_Q:(qt + 1) * UNIT_Q, :] = o.T.astype(BF16)


def _swa_call(zl, zc, sink, tq):
    b, s, _ = zl.shape
    c = zc.shape[1]
    win = UNIT_Q + 2 * SWA_WINDOW
    return pl.pallas_call(
        _swa_kernel,
        grid=(b, s // tq),
        in_specs=[
            pl.BlockSpec((1, tq, MIX_W), lambda i, j: (i, j, ZC_SQ // MIX_W)),
            pl.BlockSpec((1, s, LANES), lambda i, j: (i, 0, ZC_SK // LANES)),
            pl.BlockSpec((1, s, LANES), lambda i, j: (i, 0, ZC_SV // LANES)),
            pl.BlockSpec((1, c, LANES), lambda i, j: (i, 0, ZC_SK // LANES)),
            pl.BlockSpec((1, c, LANES), lambda i, j: (i, 0, ZC_SV // LANES)),
            pl.BlockSpec(memory_space=pltpu.SMEM),
        ],
        out_specs=pl.BlockSpec((1, tq, MIX_W), lambda i, j: (i, j, 0)),
        out_shape=jax.ShapeDtypeStruct((b, s, MIX_W), BF16),
        scratch_shapes=(_exp_scratch(win + c, UNIT_Q, HEADS * tq // UNIT_Q)
                        + [pltpu.VMEM((tq // UNIT_Q, win, UNIT_Q), F32)]),
        compiler_params=_params("arbitrary", "arbitrary"),
        name="swa",
    )(zl, zl, zl, zc, zc, sink)


def _na_kernel(rows_n, q_ref, k_ref, v_ref, kc_ref, vc_ref, bias_ref, o_ref, e_scr):
    c_len = kc_ref.shape[1]
    n_rb = q_ref.shape[1] // NA_TQ
    nrb = rows_n // 2
    chunks = _key_chunks(NA_NLOC, KEY_CHUNK) + [(NA_NLOC + o, r) for o, r in _key_chunks(c_len, KEY_CHUNK)]
    starts, tids = [], []
    for i in range(n_rb):
        rb = pl.program_id(1) * n_rb + i
        starts.append(pl.multiple_of(
            jnp.clip(2 * rb - NA_WIN_R // 2, 0, rows_n - NA_KROWS) * GRID_W, 2 * GRID_W))
        tids.append(jnp.minimum(rb, 2) + jnp.maximum(rb - (nrb - 3), 0))
    blk = lambda h: slice((h // 2) * LANES, (h // 2 + 1) * LANES)
    units = [(i, h) for i in range(n_rb) for h in range(HEADS)]
    qs = [_lane_mask(q_ref[0, i * NA_TQ:(i + 1) * NA_TQ, blk(h)], (h % 2) * HEAD_DIM, HEAD_DIM)
          for i, h in units]

    def score_of(u, c):
        off, rows = chunks[c]
        i, h = units[u]
        if off < NA_NLOC:
            k = k_ref[0, pl.ds(starts[i] + off, rows), blk(h)]
            return _dot_nt(k, qs[u]) + bias_ref[h, tids[i], off:off + rows, :]
        return _dot_nt(kc_ref[0, off - NA_NLOC:off - NA_NLOC + rows, blk(h)], qs[u])

    def pv_of(u, c, e):
        i, h = units[u]
        off, rows = chunks[c]
        if off < NA_NLOC:
            return _head_rows(_dot_tn(v_ref[0, pl.ds(starts[i] + off, rows), blk(h)], e), h)
        return _head_rows(_dot_tn(vc_ref[0, off - NA_NLOC:off - NA_NLOC + rows, blk(h)], e), h)

    outs = _localmax_batched(len(units), chunks, score_of, lambda u: None, pv_of, e_scr)
    for i in range(n_rb):
        o = jnp.concatenate(outs[i * HEADS:(i + 1) * HEADS], axis=0)
        o_ref[0, i * NA_TQ:(i + 1) * NA_TQ, :] = o.T.astype(BF16)


def _na_call(zl, zc, bias, tq):
    b, s, _ = zl.shape
    c = zc.shape[1]
    rows_n = s // GRID_W
    return pl.pallas_call(
        functools.partial(_na_kernel, rows_n),
        grid=(b, s // tq),
        in_specs=[
            pl.BlockSpec((1, tq, MIX_W), lambda i, j: (i, j, ZC_NQ // MIX_W)),
            pl.BlockSpec((1, s, MIX_W), lambda i, j: (i, 0, ZC_NK // MIX_W)),
            pl.BlockSpec((1, s, MIX_W), lambda i, j: (i, 0, ZC_NV // MIX_W)),
            pl.BlockSpec((1, c, MIX_W), lambda i, j: (i, 0, ZC_NK // MIX_W)),
            pl.BlockSpec((1, c, MIX_W), lambda i, j: (i, 0, ZC_NV // MIX_W)),
            pl.BlockSpec(bias.shape, lambda i, j: (0, 0, 0, 0), pipeline_mode=pl.Buffered(1)),
        ],
        out_specs=pl.BlockSpec((1, tq, MIX_W), lambda i, j: (i, j, 0)),
        out_shape=jax.ShapeDtypeStruct((b, s, MIX_W), BF16),
        scratch_shapes=_exp_scratch(NA_NLOC + c, NA_TQ, HEADS * tq // NA_TQ),
        compiler_params=_params("arbitrary", "arbitrary"),
        name="na",
    )(zl, zl, zl, zc, zc, bias)


def _ctx_kernel(lam_init, z_ref, lp_ref, gain_ref, sink_ref, oa_ref, ob_ref, oc_ref, od_ref,
                e_scr):
    c_len = z_ref.shape[1]

    def blk(col, j=0):
        return z_ref[0, :, col + j * LANES:col + (j + 1) * LANES]

    units = []
    for h in range(HEADS):
        units.append((_lane_mask(blk(ZC_NQ, h // 2), (h % 2) * HEAD_DIM, HEAD_DIM),
                      ZC_NK + (h // 2) * LANES, ZC_NV + (h // 2) * LANES, (h % 2) * HEAD_DIM, None))
    for h in range(HEADS):
        units.append((blk(ZC_MQ, h), ZC_MK + h * LANES, ZC_MV + (h // 2) * LANES,
                      (h % 2) * HEAD_DIM, None))
    for h in range(HEADS):
        hk, g = h // SWA_GROUP, h % SWA_GROUP
        units.append((_lane_mask(blk(ZC_SQ, g), hk * HEAD_DIM, HEAD_DIM), ZC_SK, ZC_SV,
                      hk * HEAD_DIM, h))
    for w in range(2 * HEADS):
        units.append((_lane_mask(blk(ZC_DQ, w // 4), (w % 4) * DIFF_DH, DIFF_DH),
                      ZC_DK + (w // 4) * LANES, ZC_DV + (w // 4) * LANES, ((w // 2) % 2) * HEAD_DIM, None))

    def score_of(u, c):
        q_m, kcol = units[u][0], units[u][1]
        return _dot_nt(z_ref[0, :, kcol:kcol + LANES], q_m)

    def extra_of(u):
        h = units[u][4]
        return None if h is None else jnp.full((1, 1), sink_ref[h] * LOG2E, F32)

    def pv_of(u, c, e):
        vcol, r0 = units[u][2], units[u][3]
        return _dot_tn(z_ref[0, :, vcol:vcol + LANES], e)[r0:r0 + HEAD_DIM]

    outs = _localmax_batched(len(units), [(0, c_len)], score_of, extra_of, pv_of, e_scr)
    for i, ref in enumerate((oa_ref, ob_ref, oc_ref)):
        ref[0] = jnp.concatenate(outs[i * HEADS:(i + 1) * HEADS], axis=0).T.astype(BF16)
    o = _diff_combine(outs[3 * HEADS:], _diff_lambda(lp_ref[...], lam_init))
    od_ref[0] = (o.T * gain_ref[...]).astype(BF16)


def _ctx_call(zc, lp, gain, sink, lam_init):
    b, c, _ = zc.shape
    out = jax.ShapeDtypeStruct((b, c, MIX_W), BF16)
    ospec = pl.BlockSpec((1, c, MIX_W), lambda i: (i, 0, 0))
    return pl.pallas_call(
        functools.partial(_ctx_kernel, lam_init),
        grid=(b,),
        in_specs=[
            pl.BlockSpec((1, c, Z_COLS), lambda i: (i, 0, 0)),
            pl.BlockSpec(lp.shape, lambda i: (0, 0)),
            pl.BlockSpec(gain.shape, lambda i: (0, 0)),
            pl.BlockSpec(memory_space=pltpu.SMEM),
        ],
        out_specs=[ospec] * 4,
        out_shape=[out] * 4,
        scratch_shapes=_exp_scratch(c, c, 5 * HEADS),
        compiler_params=_params("arbitrary"),
        name="ctx_attn",
    )(zc, lp, gain, sink)


def _mlp_kernel(final, x_ref, ma_ref, mb_ref, mc_ref, md_ref, mod_ref, g_ref, gf_ref,
                wout_ref, wup_ref, wdn_ref, o_ref):
    x = x_ref[0]
    mod = mod_ref[0]
    attn = None
    for i, m_ref in enumerate((ma_ref, mb_ref, mc_ref, md_ref)):
        part = _dot(m_ref[0], wout_ref[i * MIX_W:(i + 1) * MIX_W, :])
        attn = part if attn is None else attn + part
    x = x + mod[2:3] * attn
    h = x * _rms_scale(x, D_MODEL) * g_ref[...]
    h = (h * (1.0 + mod[4:5]) + mod[3:4]).astype(BF16)
    acc = None
    ck = FF_CHUNK
    for c in range(D_FF // ck):
        u = jnp.maximum(_dot(h, wup_ref[:, c * ck:(c + 1) * ck]), 0.0)
        part = _dot((u * u).astype(BF16), wdn_ref[c * ck:(c + 1) * ck, :])
        acc = part if acc is None else acc + part
    x = x + mod[5:6] * acc
    if final:
        x = x * _rms_scale(x, D_MODEL) * gf_ref[...]
    o_ref[0] = x


def _mlp_call(x3, mixes, mod, mod_row, g, gf, wout, wup, wdn, tm, final):
    nb, n, d = x3.shape
    const = lambda t, b: (0, 0)
    tok = lambda t, b: (b, t, 0)
    mod_map = (lambda t, b: (b, 0, 0)) if mod_row is None else (lambda t, b: (mod_row, 0, 0))
    single = pl.Buffered(1)
    return pl.pallas_call(
        functools.partial(_mlp_kernel, final),
        grid=(n // tm, nb),
        in_specs=[pl.BlockSpec((1, tm, d), tok)]
        + [pl.BlockSpec((1, tm, MIX_W), tok)] * 4
        + [
            pl.BlockSpec((1, 6, d), mod_map),
            pl.BlockSpec((1, d), const),
            pl.BlockSpec((1, d), const),
            pl.BlockSpec(wout.shape, const, pipeline_mode=single),
            pl.BlockSpec(wup.shape, const, pipeline_mode=single),
            pl.BlockSpec(wdn.shape, const, pipeline_mode=single),
        ],
        out_specs=pl.BlockSpec((1, tm, d), tok),
        out_shape=jax.ShapeDtypeStruct((nb, n, d), F32),
        compiler_params=_params("arbitrary", "arbitrary"),
        name="mlp",
    )(x3, *mixes, mod, g, gf, wout, wup, wdn)


def _rope_tables(s_len):
    t = np.arange(s_len)
    row, col = t // GRID_W, t % GRID_W

    def axial(dim):
        n_freq = dim // 4
        freqs = jnp.asarray(ROPE_BASE, F32) ** (-jnp.arange(n_freq, dtype=F32) / n_freq)
        ang = jnp.concatenate([jnp.asarray(row, F32)[:, None] * freqs,
                               jnp.asarray(col, F32)[:, None] * freqs], axis=-1)
        return jnp.cos(ang), jnp.sin(ang)

    def group(cs, n_groups, scale):
        cos, sin = cs
        c = jnp.concatenate([cos, cos], axis=-1) * scale
        s = jnp.concatenate([-sin, sin], axis=-1) * scale
        return jnp.tile(c, (1, n_groups)), jnp.tile(s, (1, n_groups))

    r64, r32 = axial(HEAD_DIM), axial(MLA_ROPE)
    ones = lambda n, v=1.0: jnp.full((s_len, n), v, F32)
    zeros = lambda n: jnp.zeros((s_len, n), F32)
    mla_scale = (MLA_NOPE + MLA_ROPE) ** -0.5 * LOG2E
    mq_c, mq_s = group(r32, 1, mla_scale)
    parts = [
        group(r64, 4, HEAD_DIM ** -0.5 * LOG2E), group(r64, 2, 1.0),
        group(r32, 8, DIFF_DH ** -0.5 * LOG2E), group(r32, 8, 1.0),
        (ones(MLA_Q_LORA), zeros(MLA_Q_LORA)), group(r32, 1, 1.0), (ones(32), zeros(32)),
    ] + [(jnp.concatenate([ones(MLA_NOPE, mla_scale), mq_c, ones(32, mla_scale)], axis=-1),
          jnp.concatenate([zeros(MLA_NOPE), mq_s, zeros(32)], axis=-1))] * HEADS
    cos = jnp.concatenate([p[0] for p in parts], axis=-1)
    sin = jnp.concatenate([p[1] for p in parts], axis=-1)
    return cos, sin, cos[0:1]


def _permute_w_in(w_in):
    offs = np.cumsum([0, 256, 256, 256, MLA_Q_LORA, MLA_KV_LORA, MLA_ROPE, 256, 128, 128, 256, 256, 256])
    sec = lambda i: w_in[..., offs[i]:offs[i + 1]]
    sq = sec(6)
    sq = sq.reshape(sq.shape[:-1] + (SWA_HKV, SWA_GROUP, HEAD_DIM))
    sq = jnp.swapaxes(sq, -3, -2).reshape(w_in.shape[:-1] + (256,))
    pad = jnp.zeros(w_in.shape[:-1] + (32,), w_in.dtype)
    cols = [sq, sec(7), sec(9), sec(10), sec(4), sec(3), sec(5), pad,
            sec(0), sec(1), sec(2), sec(11), sec(8)]
    return jnp.concatenate(cols, axis=-1).astype(BF16)


def _mla_weights(w_uq, w_ukv):
    depth = w_uq.shape[0]
    uq = w_uq.reshape(depth, MLA_Q_LORA, HEADS, MLA_NOPE + MLA_ROPE)
    uq = jnp.pad(uq, ((0, 0), (0, 256 - MLA_Q_LORA), (0, 0), (0, LANES - MLA_NOPE - MLA_ROPE)))
    uq = uq.reshape(depth, 256, HEADS * LANES).astype(BF16)
    ukv = w_ukv.reshape(depth, MLA_KV_LORA, HEADS, 2, HEAD_DIM)
    uk = jnp.pad(ukv[:, :, :, 0], ((0, 0), (0, 0), (0, 0), (0, LANES - MLA_NOPE)))
    uk = uk.reshape(depth, MLA_KV_LORA, HEADS * LANES).astype(BF16)
    uv = ukv[:, :, :, 1].reshape(depth, MLA_KV_LORA, HEADS * HEAD_DIM).astype(BF16)
    return uq, uk, uv


def _na_bias_tables(rpb, rows_n):
    kr_n = min(NA_WIN_R, rows_n)
    col = np.arange(GRID_W)
    c0 = np.clip(col - NA_WIN_C // 2, 0, GRID_W - NA_WIN_C)
    col_ok_t = ((col[None, :] >= c0[:, None]) & (col[None, :] < c0[:, None] + NA_WIN_C)).T
    dc_idx_t = np.clip(col[:, None] - col[None, :], 1 - NA_WIN_C, NA_WIN_C - 1) + (NA_WIN_C - 1)
    toe = jnp.zeros(rpb.shape[:3] + (GRID_W, GRID_W), F32)
    for d in range(2 * NA_WIN_C - 1):
        toe = jnp.where(dc_idx_t == d, rpb[..., d, None, None].astype(F32), toe)
    toe = jnp.where(col_ok_t, toe * LOG2E, NEG_INF)
    neg = jnp.full(toe.shape[:2] + (GRID_W, GRID_W), NEG_INF, F32)
    nrb = rows_n // 2
    pairs = {}

    def pair(drs):
        if drs not in pairs:
            pairs[drs] = jnp.concatenate([neg if d is None else toe[:, :, d] for d in drs], axis=-1)
        return pairs[drs]

    blocks = []
    for rb in (0, 1, 2, nrb - 2, nrb - 1):
        start = int(np.clip(2 * rb - NA_WIN_R // 2, 0, rows_n - NA_KROWS))
        for j in range(NA_KROWS):
            drs = []
            for i in range(2):
                kr, qr = start + j, 2 * rb + i
                r0 = int(np.clip(qr - kr_n // 2, 0, rows_n - kr_n))
                drs.append(kr - qr + NA_WIN_R - 1 if r0 <= kr < r0 + kr_n else None)
            blocks.append(pair(tuple(drs)))
    tab = jnp.stack(blocks, axis=2)
    return tab.reshape(tab.shape[:2] + (5, NA_NLOC, NA_TQ))


def kernel(x, c, ctx, c_ctx, w_ada, b_ada, norm_attn_g, w_in, na_rpb, mla_q_norm_g, mla_w_uq,
           mla_kv_norm_g, mla_w_ukv, swa_sink, diff_lambda, diff_norm_g, w_out, norm_mlp_g,
           w_up, w_down, final_norm_g):
    b, s, d = x.shape
    c_len = ctx.shape[1]
    depth = w_in.shape[0]
    rows_n = s // GRID_W
    assert d == D_MODEL and s % max(ATTN_TQ, DIFF_TQ, TOKEN_TILE) == 0 and rows_n >= NA_KROWS
    assert c_len % KEY_CHUNK == 0 and (b * c_len) % min(TOKEN_TILE, b * c_len) == 0

    w_in_p = _permute_w_in(w_in)
    uq, uk, uv = _mla_weights(mla_w_uq, mla_w_ukv)
    gq = jnp.pad(mla_q_norm_g, ((0, 0), (0, 256 - MLA_Q_LORA)))[:, None, :]
    gkv = mla_kv_norm_g[:, None, :]
    w_out_b, w_up_b, w_dn_b = w_out.astype(BF16), w_up.astype(BF16), w_down.astype(BF16)
    na_bias = _na_bias_tables(na_rpb, rows_n)
    cos, sin, cos_ctx = _rope_tables(s)
    tm_l, tm_c = TOKEN_TILE, min(TOKEN_TILE, b * c_len)
    cos_c = jnp.broadcast_to(cos_ctx, (tm_c, T_COLS))
    sin_c = jnp.zeros((tm_c, T_COLS), F32)

    c_rows = -(-(b + 1) // 8) * 8
    c_all = jnp.zeros((c_rows, d), F32).at[:b].set(c).at[b].set(c_ctx)
    mod = _ada_call(c_all, w_ada, b_ada).reshape(depth, c_rows, 6, d)

    xl = x
    xc = ctx.reshape(1, b * c_len, d)
    for l in range(depth):
        need_ctx = l < depth - 1
        lam_init = 0.8 - 0.6 * math.exp(-0.3 * l)
        g_attn = norm_attn_g[l][None, :]
        g_mlp = norm_mlp_g[l][None, :]
        gf = final_norm_g[None, :]
        d_gain = (jnp.tile(diff_norm_g[l], HEADS) * (1.0 - lam_init))[None, :]
        proj = (w_in_p[l], gq[l], gkv[l], uq[l], uk[l], uv[l])
        zl = _inproj_call(xl, mod[l], None, g_attn, cos, sin, True, *proj, tm=tm_l)
        zc = _inproj_call(xc, mod[l], b, g_attn, cos_c, sin_c, False, *proj, tm=tm_c)
        zc = zc.reshape(b, c_len, Z_COLS)
        mixes = (
            _na_call(zl, zc, na_bias[l], tq=ATTN_TQ),
            _mla_call(zl, zc, tq=ATTN_TQ),
            _swa_call(zl, zc, swa_sink[l], tq=ATTN_TQ),
            _diff_call(zl, zc, diff_lambda[l], d_gain, lam_init, tq=DIFF_TQ),
        )
        wts = (w_out_b[l], w_up_b[l], w_dn_b[l])
        xl = _mlp_call(xl, mixes, mod[l], None, g_mlp, gf, *wts, tm=tm_l, final=not need_ctx)
        if need_ctx:
            mixes_c = _ctx_call(zc, diff_lambda[l], d_gain, swa_sink[l], lam_init)
            mixes_c = [m.reshape(1, b * c_len, MIX_W) for m in mixes_c]
            xc = _mlp_call(xc, mixes_c, mod[l], b, g_mlp, gf, *wts, tm=tm_c, final=False)
    return xl
```

```python
import functools
import math

import jax
import jax.numpy as jnp
import numpy as np
from jax import lax
from jax.experimental import pallas as pl
from jax.experimental.pallas import tpu as pltpu

F32 = jnp.float32
BF16 = jnp.bfloat16

D_MODEL = 1024
GRID_W = 64
HEADS = 4
HEAD_DIM = 64
NA_WIN_R = 8
NA_WIN_C = 16
MLA_Q_LORA = 192
MLA_KV_LORA = 128
MLA_NOPE = 64
MLA_ROPE = 32
SWA_HKV = 2
SWA_GROUP = 2
SWA_WINDOW = 128
DIFF_DH = 32
D_FF = 4 * D_MODEL
ROPE_BASE = 10000.0
EPS = 1e-6
NEG_INF = -1e30
LOG2E = math.log2(math.e)

LANES = 128
VMEM_LIMIT = 56 * 1024 * 1024

ZC_DQ, ZC_DK, ZC_SQ, ZC_NQ, ZC_NK, ZC_NV, ZC_DV, ZC_MV = 0, 256, 512, 768, 1024, 1280, 1536, 1792
ZC_MQ, ZC_MK, ZC_SK, ZC_SV = 2048, 2560, 3072, 3200
Z_COLS = 3328
W_R64, W_R32, W_M, W_N, W_COLS = 0, 384, 896, 1280, 2432
T_R64, T_R32, T_B12, T_MQ, T_COLS = 0, 384, 896, 1152, 1664

NA_KROWS = 10
NA_TQ = 2 * GRID_W
NA_NLOC = NA_KROWS * GRID_W
KEY_CHUNK = 256
UNIT_Q = 256
MIX_W = HEADS * HEAD_DIM
MLA_W = HEADS * LANES

TOKEN_TILE = 512
ATTN_TQ = 1024
WINDOW_TQ = 2048
WINDOW_CHUNK = 512
DIFF_TQ = 1024
ADA_TN = 1536
FF_CHUNK = 1024


def _params(*sem):
    return pltpu.CompilerParams(dimension_semantics=sem, vmem_limit_bytes=VMEM_LIMIT)


def _dot(a, b):
    return jnp.dot(a, b, preferred_element_type=F32)


def _dot_nt(a, b):
    return lax.dot_general(a, b, (((1,), (1,)), ((), ())), preferred_element_type=F32)


def _dot_tn(a, b):
    return lax.dot_general(a, b, (((0,), (0,)), ((), ())), preferred_element_type=F32)


def _rms_scale(x, n):
    return lax.rsqrt(jnp.sum(x * x, axis=-1, keepdims=True) * (1.0 / n) + EPS)


def _ada_kernel(c_ref, w_ref, b_ref, o_ref):
    c = c_ref[...]
    act = (c * (1.0 / (1.0 + jnp.exp(-c)))).astype(BF16)
    o_ref[0] = _dot(act, w_ref[0].astype(BF16)) + b_ref[0]


def _ada_call(c_all, w_ada, b_ada):
    depth, d, n = w_ada.shape
    rows = c_all.shape[0]
    tn = ADA_TN
    return pl.pallas_call(
        _ada_kernel,
        grid=(depth, n // tn),
        in_specs=[
            pl.BlockSpec((rows, d), lambda l, j: (0, 0)),
            pl.BlockSpec((1, d, tn), lambda l, j: (l, 0, j)),
            pl.BlockSpec((1, 1, tn), lambda l, j: (l, 0, j)),
        ],
        out_specs=pl.BlockSpec((1, rows, tn), lambda l, j: (l, 0, j)),
        out_shape=jax.ShapeDtypeStruct((depth, rows, n), F32),
        compiler_params=_params("arbitrary", "arbitrary"),
        name="ada",
    )(c_all, w_ada, b_ada.reshape(depth, 1, n))


def _rope_blocks(z, cos_ref, sin_ref, tcol, half):
    tm = z.shape[0]
    lane = lax.broadcasted_iota(jnp.int32, (tm, LANES), 1)
    first = (lane % (2 * half)) < half
    out = []
    for j in range(z.shape[1] // LANES):
        xb = z[:, j * LANES:(j + 1) * LANES]
        rot = jnp.where(first, pltpu.roll(xb, LANES - half, 1), pltpu.roll(xb, half, 1))
        c0 = tcol + j * LANES
        out.append(xb * cos_ref[:, c0:c0 + LANES] + rot * sin_ref[:, c0:c0 + LANES])
    return out


def _inproj_kernel(x_ref, mod_ref, g_ref, cos_ref, sin_ref, w_ref, gq_ref, gkv_ref,
                   wuq_ref, wukvk_ref, wukvv_ref, o_ref):
    x = x_ref[0]
    mod = mod_ref[0]
    h = x * _rms_scale(x, D_MODEL) * g_ref[...]
    h = (h * (1.0 + mod[1:2]) + mod[0:1]).astype(BF16)

    def put(col, val):
        o_ref[0, :, col:col + val.shape[1]] = val.astype(BF16)

    z64 = _dot(h, w_ref[:, W_R64:W_R32])
    z32 = _dot(h, w_ref[:, W_R32:W_M])
    r = _rope_blocks(z64, cos_ref, sin_ref, T_R64, HEAD_DIM // 2)
    put(ZC_SQ, r[0]); put(ZC_SQ + LANES, r[1]); put(ZC_SK, r[2])
    zm = _dot(h, w_ref[:, W_M:W_N])
    r = _rope_blocks(z32, cos_ref, sin_ref, T_R32, DIFF_DH // 2)
    for j in range(4):
        put(ZC_DQ + j * LANES, r[j])
    zn = _dot(h, w_ref[:, W_N:W_COLS])

    ckv = zm[:, 0:MLA_KV_LORA]
    b12 = zm[:, MLA_KV_LORA:]
    ckv_n = (ckv * _rms_scale(ckv, MLA_KV_LORA) * gkv_ref[...]).astype(BF16)
    lane = lax.broadcasted_iota(jnp.int32, b12.shape, 1)
    cq = jnp.where(lane < MLA_Q_LORA, b12, 0.0)
    cq_n = (cq * _rms_scale(cq, MLA_Q_LORA) * gq_ref[...]).astype(BF16)
    zq = _dot(cq_n, wuq_ref[...])
    put(ZC_NQ, zn[:, 0:256] * (HEAD_DIM ** -0.5 * LOG2E))
    put(ZC_NK, zn[:, 256:1024])
    put(ZC_SV, zn[:, 1024:1152])
    q = _rope_blocks(zq, cos_ref, sin_ref, T_MQ, MLA_ROPE // 2)
    for j in range(HEADS):
        put(ZC_MQ + j * LANES, q[j])
    kr = _rope_blocks(b12[:, LANES:], cos_ref, sin_ref, T_B12 + LANES, MLA_ROPE // 2)[0]
    lane = lax.broadcasted_iota(jnp.int32, kr.shape, 1)
    kr = jnp.where(jnp.where(lane >= MLA_NOPE, lane, LANES) < MLA_NOPE + MLA_ROPE, kr, 0.0)
    kn = _dot(ckv_n, wukvk_ref[...])
    for j in range(HEADS):
        put(ZC_MK + j * LANES, kn[:, j * LANES:(j + 1) * LANES] + kr)
    put(ZC_MV, _dot(ckv_n, wukvv_ref[...]))


def _inproj_call(x3, mod, mod_row, g, cos, sin, table_per_tile, w, gq, gkv, wuq, wukvk, wukvv, tm):
    nb, n, d = x3.shape
    nt = n // tm
    const = lambda t, b: (0, 0)
    tab_map = (lambda t, b: (t, 0)) if table_per_tile else const
    mod_map = (lambda t, b: (b, 0, 0)) if mod_row is None else (lambda t, b: (mod_row, 0, 0))
    return pl.pallas_call(
        _inproj_kernel,
        grid=(nt, nb),
        in_specs=[
            pl.BlockSpec((1, tm, d), lambda t, b: (b, t, 0)),
            pl.BlockSpec((1, 6, d), mod_map),
            pl.BlockSpec((1, d), const),
            pl.BlockSpec((tm, T_COLS), tab_map),
            pl.BlockSpec((tm, T_COLS), tab_map),
            pl.BlockSpec(w.shape, const, pipeline_mode=pl.Buffered(1)),
            pl.BlockSpec(gq.shape, const),
            pl.BlockSpec(gkv.shape, const),
            pl.BlockSpec(wuq.shape, const),
            pl.BlockSpec(wukvk.shape, const),
            pl.BlockSpec(wukvv.shape, const),
        ],
        out_specs=pl.BlockSpec((1, tm, Z_COLS), lambda t, b: (b, t, 0)),
        out_shape=jax.ShapeDtypeStruct((nb, n, Z_COLS), BF16),
        compiler_params=_params("arbitrary", "arbitrary"),
        name="inproj",
    )(x3, mod, g, cos, sin, w, gq, gkv, wuq, wukvk, wukvv)


def _lane_mask(q, lo, width):
    lane = lax.broadcasted_iota(jnp.int32, q.shape, 1)
    keep = jnp.where(lane >= lo, lane, LANES) < lo + width
    return jnp.where(keep, q.astype(F32), 0.0).astype(BF16)


def _localmax_softmax(n_units, chunks, score_of, extra_of, pv_of, e_scr):
    outs = [None] * n_units
    mloc, mfin = {}, {}
    n_c = len(chunks)
    for t in range(n_units + 1):
        acc = None
        if t < n_units:
            mloc[t] = []
        for cc in range(n_c + 1):
            if t < n_units and cc < n_c:
                off, rows = chunks[cc]
                s = score_of(t, cc)
                mc = jnp.max(s, axis=0, keepdims=True)
                e_scr[t % 2, off:off + rows, :] = jnp.exp2(s - mc).astype(BF16)
                mloc[t].append(mc)
            if t >= 1 and cc >= 1:
                u, c = t - 1, cc - 1
                off, rows = chunks[c]
                pv = pv_of(u, c, e_scr[u % 2, off:off + rows, :]) * jnp.exp2(mloc[u][c] - mfin[u])
                acc = pv if acc is None else acc + pv
        if t < n_units:
            m = mloc[t][0]
            for mc in mloc[t][1:]:
                m = jnp.maximum(m, mc)
            x = extra_of(t)
            mfin[t] = m if x is None else jnp.maximum(m, x)
        if t >= 1:
            l = acc[HEAD_DIM:HEAD_DIM + 1]
            x = extra_of(t - 1)
            if x is not None:
                l = l + jnp.exp2(x - mfin[t - 1])
            outs[t - 1] = acc[0:HEAD_DIM] * (1.0 / l)
    return outs


def _localmax_batched(n_units, chunks, score_of, extra_of, pv_of, e_scr):
    mloc, lloc, mfin = [], [], []
    for u in range(n_units):
        ms, ls = [], []
        for c, (off, rows) in enumerate(chunks):
            s = score_of(u, c)
            mc = jnp.max(s, axis=0, keepdims=True)
            e = jnp.exp2(s - mc)
            ls.append(jnp.sum(e, axis=0, keepdims=True))
            e_scr[u, off:off + rows, :] = e.astype(BF16)
            ms.append(mc)
        m = ms[0]
        for mc in ms[1:]:
            m = jnp.maximum(m, mc)
        x = extra_of(u)
        mloc.append(ms); lloc.append(ls); mfin.append(m if x is None else jnp.maximum(m, x))
    outs = []
    for u in range(n_units):
        acc = l = None
        for c, (off, rows) in enumerate(chunks):
            alpha = jnp.exp2(mloc[u][c] - mfin[u])
            pv = pv_of(u, c, e_scr[u, off:off + rows, :]) * alpha
            acc = pv if acc is None else acc + pv
            l = lloc[u][c] * alpha if l is None else l + lloc[u][c] * alpha
        x = extra_of(u)
        if x is not None:
            l = l + jnp.exp2(x - mfin[u])
        outs.append(acc * (1.0 / l))
    return outs


def _head_rows(o, h):
    return o[(h % 2) * HEAD_DIM:(h % 2 + 1) * HEAD_DIM]


def _diff_lambda(lp, lam_init):
    a = jnp.sum(lp[0:1] * lp[1:2], axis=-1, keepdims=True)
    b = jnp.sum(lp[2:3] * lp[3:4], axis=-1, keepdims=True)
    return jnp.exp(a) - jnp.exp(b) + lam_init


def _diff_combine(outs, lam):
    res = []
    for h in range(HEADS):
        o = outs[2 * h] - lam * outs[2 * h + 1]
        ms = jnp.sum(o * o, axis=0, keepdims=True) * (1.0 / HEAD_DIM)
        res.append(o * lax.rsqrt(ms + EPS))
    return jnp.concatenate(res, axis=0)


def _key_chunks(total, size):
    return [(o, min(size, total - o)) for o in range(0, total, size)]


def _exp_scratch(nk, tq, slots=2):
    return [pltpu.VMEM((slots, nk, tq), BF16)]


VT_ROWS = HEAD_DIM + 16


def _transpose_values(vt_scr, srcs):
    for ref, off, n in srcs:
        for r in range(0, n, KEY_CHUNK):
            rows = min(KEY_CHUNK, n - r)
            vt = ref[0, r:r + rows, :].astype(F32).T
            for h in range(HEADS):
                vt_scr[h * VT_ROWS:h * VT_ROWS + HEAD_DIM, off + r:off + r + rows] = (
                    vt[h * HEAD_DIM:(h + 1) * HEAD_DIM].astype(BF16))
                vt_scr[h * VT_ROWS + HEAD_DIM:(h + 1) * VT_ROWS, off + r:off + r + rows] = (
                    jnp.ones((VT_ROWS - HEAD_DIM, rows), BF16))


def _mla_kernel(q_ref, k_ref, v_ref, kc_ref, vc_ref, o_ref, e_scr, vt_scr):
    s_len, c_len = k_ref.shape[1], kc_ref.shape[1]

    @pl.when(pl.program_id(1) == 0)
    def _():
        _transpose_values(vt_scr, [(v_ref, 0, s_len), (vc_ref, s_len, c_len)])

    chunks = _key_chunks(s_len, KEY_CHUNK) + [(s_len + o, r) for o, r in _key_chunks(c_len, KEY_CHUNK)]
    units = [(qt, h) for qt in range(q_ref.shape[1] // UNIT_Q) for h in range(HEADS)]
    qs = [q_ref[0, qt * UNIT_Q:(qt + 1) * UNIT_Q, h * LANES:(h + 1) * LANES] for qt, h in units]

    def score_of(u, c):
        off, rows = chunks[c]
        h = units[u][1]
        sl = slice(h * LANES, (h + 1) * LANES)
        if off < s_len:
            return _dot_nt(k_ref[0, off:off + rows, sl], qs[u])
        return _dot_nt(kc_ref[0, off - s_len:off - s_len + rows, sl], qs[u])

    def pv_of(u, c, e):
        h = units[u][1]
        off, rows = chunks[c]
        return _dot(vt_scr[h * VT_ROWS:(h + 1) * VT_ROWS, off:off + rows], e)

    outs = _localmax_softmax(len(units), chunks, score_of, lambda u: None, pv_of, e_scr)
    for qt in range(len(units) // HEADS):
        o = jnp.concatenate(outs[qt * HEADS:(qt + 1) * HEADS], axis=0)
        o_ref[0, qt * UNIT_Q:(qt + 1) * UNIT_Q, :] = o.T.astype(BF16)


def _mla_call(zl, zc, tq):
    b, s, _ = zl.shape
    c = zc.shape[1]
    return pl.pallas_call(
        _mla_kernel,
        grid=(b, s // tq),
        in_specs=[
            pl.BlockSpec((1, tq, MLA_W), lambda i, j: (i, j, ZC_MQ // MLA_W)),
            pl.BlockSpec((1, s, MLA_W), lambda i, j: (i, 0, ZC_MK // MLA_W)),
            pl.BlockSpec((1, s, MIX_W), lambda i, j: (i, 0, ZC_MV // MIX_W)),
            pl.BlockSpec((1, c, MLA_W), lambda i, j: (i, 0, ZC_MK // MLA_W)),
            pl.BlockSpec((1, c, MIX_W), lambda i, j: (i, 0, ZC_MV // MIX_W)),
        ],
        out_specs=pl.BlockSpec((1, tq, MIX_W), lambda i, j: (i, j, 0)),
        out_shape=jax.ShapeDtypeStruct((b, s, MIX_W), BF16),
        scratch_shapes=_exp_scratch(s + c, UNIT_Q) + [pltpu.VMEM((HEADS * VT_ROWS, s + c), BF16)],
        compiler_params=_params("arbitrary", "arbitrary"),
        name="mla",
    )(zl, zl, zl, zc, zc)


def _diff_kernel(lam_init, q_ref, k_ref, v_ref, kc_ref, vc_ref, lp_ref, gain_ref, o_ref,
                 e_scr, vt_scr):
    s_len, c_len = k_ref.shape[1], kc_ref.shape[1]

    @pl.when(pl.program_id(1) == 0)
    def _():
        _transpose_values(vt_scr, [(v_ref, 0, s_len), (vc_ref, s_len, c_len)])

    chunks = _key_chunks(s_len, KEY_CHUNK) + [(s_len + o, r) for o, r in _key_chunks(c_len, KEY_CHUNK)]
    n_sub = 2 * HEADS
    units = [(qt, w) for qt in range(q_ref.shape[1] // UNIT_Q) for w in range(n_sub)]
    qs = [_lane_mask(q_ref[0, qt * UNIT_Q:(qt + 1) * UNIT_Q, (w // 4) * LANES:(w // 4 + 1) * LANES],
                     (w % 4) * DIFF_DH, DIFF_DH) for qt, w in units]

    def score_of(u, c):
        off, rows = chunks[c]
        w = units[u][1]
        sl = slice((w // 4) * LANES, (w // 4 + 1) * LANES)
        if off < s_len:
            return _dot_nt(k_ref[0, off:off + rows, sl], qs[u])
        return _dot_nt(kc_ref[0, off - s_len:off - s_len + rows, sl], qs[u])

    def pv_of(u, c, e):
        h = units[u][1] // 2
        off, rows = chunks[c]
        return _dot(vt_scr[h * VT_ROWS:(h + 1) * VT_ROWS, off:off + rows], e)

    outs = _localmax_softmax(len(units), chunks, score_of, lambda u: None, pv_of, e_scr)
    lam = _diff_lambda(lp_ref[...], lam_init)
    for qt in range(len(units) // n_sub):
        o = _diff_combine(outs[qt * n_sub:(qt + 1) * n_sub], lam)
        o_ref[0, qt * UNIT_Q:(qt + 1) * UNIT_Q, :] = (o.T * gain_ref[...]).astype(BF16)


def _diff_call(zl, zc, lp, gain, lam_init, tq):
    b, s, _ = zl.shape
    c = zc.shape[1]
    return pl.pallas_call(
        functools.partial(_diff_kernel, lam_init),
        grid=(b, s // tq),
        in_specs=[
            pl.BlockSpec((1, tq, MIX_W), lambda i, j: (i, j, ZC_DQ // MIX_W)),
            pl.BlockSpec((1, s, MIX_W), lambda i, j: (i, 0, ZC_DK // MIX_W)),
            pl.BlockSpec((1, s, MIX_W), lambda i, j: (i, 0, ZC_DV // MIX_W)),
            pl.BlockSpec((1, c, MIX_W), lambda i, j: (i, 0, ZC_DK // MIX_W)),
            pl.BlockSpec((1, c, MIX_W), lambda i, j: (i, 0, ZC_DV // MIX_W)),
            pl.BlockSpec(lp.shape, lambda i, j: (0, 0)),
            pl.BlockSpec(gain.shape, lambda i, j: (0, 0)),
        ],
        out_specs=pl.BlockSpec((1, tq, MIX_W), lambda i, j: (i, j, 0)),
        out_shape=jax.ShapeDtypeStruct((b, s, MIX_W), BF16),
        scratch_shapes=_exp_scratch(s + c, UNIT_Q) + [pltpu.VMEM((HEADS * VT_ROWS, s + c), BF16)],
        compiler_params=_params("arbitrary", "arbitrary"),
        name="diff",
    )(zl, zl, zl, zc, zc, lp, gain)


def _swa_kernel(q_ref, k_ref, v_ref, kc_ref, vc_ref, sink_ref, o_ref, e_scr, mask_scr):
    s_len, c_len = k_ref.shape[1], kc_ref.shape[1]
    n_qt = q_ref.shape[1] // UNIT_Q
    win = UNIT_Q + 2 * SWA_WINDOW
    chunks = _key_chunks(win, WINDOW_CHUNK) + [(win + o, r) for o, r in _key_chunks(c_len, WINDOW_CHUNK)]
    starts = []
    for qt in range(n_qt):
        q0 = (pl.program_id(1) * n_qt + qt) * UNIT_Q
        start = pl.multiple_of(jnp.clip(q0 - SWA_WINDOW, 0, s_len - win), SWA_WINDOW)
        kpos = start + lax.broadcasted_iota(jnp.int32, (win, UNIT_Q), 0)
        qpos = q0 + lax.broadcasted_iota(jnp.int32, (win, UNIT_Q), 1)
        mask_scr[qt] = jnp.where(jnp.abs(kpos - qpos) <= SWA_WINDOW, 0.0, NEG_INF)
        starts.append(start)
    units = [(qt, w) for qt in range(n_qt) for w in range(HEADS)]
    qs = [_lane_mask(q_ref[0, qt * UNIT_Q:(qt + 1) * UNIT_Q, (w // SWA_HKV) * LANES:(w // SWA_HKV + 1) * LANES],
                     (w % SWA_HKV) * HEAD_DIM, HEAD_DIM) for qt, w in units]
    head_of = lambda w: (w % SWA_HKV) * SWA_GROUP + w // SWA_HKV

    def score_of(u, c):
        off, rows = chunks[c]
        qt = units[u][0]
        if off < win:
            k = k_ref[0, pl.ds(starts[qt] + off, rows), :]
            return _dot_nt(k, qs[u]) + mask_scr[qt, off:off + rows, :]
        return _dot_nt(kc_ref[0, off - win:off - win + rows, :], qs[u])

    def extra_of(u):
        return jnp.full((1, 1), sink_ref[head_of(units[u][1])] * LOG2E, F32)

    def pv_of(u, c, e):
        qt, w = units[u]
        off, rows = chunks[c]
        if off < win:
            o = _dot_tn(v_ref[0, pl.ds(starts[qt] + off, rows), :], e)
        else:
            o = _dot_tn(vc_ref[0, off - win:off - win + rows, :], e)
        return o[(w % SWA_HKV) * HEAD_DIM:(w % SWA_HKV + 1) * HEAD_DIM]

    outs = _localmax_batched(len(units), chunks, score_of, extra_of, pv_of, e_scr)
    for qt in range(n_qt):
        by_head = {head_of(w): outs[qt * HEADS + w] for w in range(HEADS)}
        o = jnp.concatenate([by_head[h] for h in range(HEADS)], axis=0)
        o_ref[0, qt * UNIT_Q:(qt + 1) * UNIT_Q, :] = o.T.astype(BF16)


def _swa_call(zl, zc, sink, tq):
    b, s, _ = zl.shape
    c = zc.shape[1]
    win = UNIT_Q + 2 * SWA_WINDOW
    return pl.pallas_call(
        _swa_kernel,
        grid=(b, s // tq),
        in_specs=[
            pl.BlockSpec((1, tq, MIX_W), lambda i, j: (i, j, ZC_SQ // MIX_W)),
            pl.BlockSpec((1, s, LANES), lambda i, j: (i, 0, ZC_SK // LANES)),
            pl.BlockSpec((1, s, LANES), lambda i, j: (i, 0, ZC_SV // LANES)),
            pl.BlockSpec((1, c, LANES), lambda i, j: (i, 0, ZC_SK // LANES)),
            pl.BlockSpec((1, c, LANES), lambda i, j: (i, 0, ZC_SV // LANES)),
            pl.BlockSpec(memory_space=pltpu.SMEM),
        ],
        out_specs=pl.BlockSpec((1, tq, MIX_W), lambda i, j: (i, j, 0)),
        out_shape=jax.ShapeDtypeStruct((b, s, MIX_W), BF16),
        scratch_shapes=(_exp_scratch(win + c, UNIT_Q, HEADS * tq // UNIT_Q)
                        + [pltpu.VMEM((tq // UNIT_Q, win, UNIT_Q), F32)]),
        compiler_params=_params("arbitrary", "arbitrary"),
        name="swa",
    )(zl, zl, zl, zc, zc, sink)


def _na_kernel(rows_n, q_ref, k_ref, v_ref, kc_ref, vc_ref, bias_ref, o_ref, e_scr):
    c_len = kc_ref.shape[1]
    n_rb = q_ref.shape[1] // NA_TQ
    nrb = rows_n // 2
    chunks = _key_chunks(NA_NLOC, KEY_CHUNK) + [(NA_NLOC + o, r) for o, r in _key_chunks(c_len, KEY_CHUNK)]
    starts, tids = [], []
    for i in range(n_rb):
        rb = pl.program_id(1) * n_rb + i
        starts.append(pl.multiple_of(
            jnp.clip(2 * rb - NA_WIN_R // 2, 0, rows_n - NA_KROWS) * GRID_W, 2 * GRID_W))
        tids.append(jnp.minimum(rb, 2) + jnp.maximum(rb - (nrb - 3), 0))
    blk = lambda h: slice((h // 2) * LANES, (h // 2 + 1) * LANES)
    units = [(i, h) for i in range(n_rb) for h in range(HEADS)]
    qs = [_lane_mask(q_ref[0, i * NA_TQ:(i + 1) * NA_TQ, blk(h)], (h % 2) * HEAD_DIM, HEAD_DIM)
          for i, h in units]

    def score_of(u, c):
        off, rows = chunks[c]
        i, h = units[u]
        if off < NA_NLOC:
            k = k_ref[0, pl.ds(starts[i] + off, rows), blk(h)]
            return _dot_nt(k, qs[u]) + bias_ref[h, tids[i], off:off + rows, :]
        return _dot_nt(kc_ref[0, off - NA_NLOC:off - NA_NLOC + rows, blk(h)], qs[u])

    def pv_of(u, c, e):
        i, h = units[u]
        off, rows = chunks[c]
        if off < NA_NLOC:
            return _head_rows(_dot_tn(v_ref[0, pl.ds(starts[i] + off, rows), blk(h)], e), h)
        return _head_rows(_dot_tn(vc_ref[0, off - NA_NLOC:off - NA_NLOC + rows, blk(h)], e), h)

    outs = _localmax_batched(len(units), chunks, score_of, lambda u: None, pv_of, e_scr)
    for i in range(n_rb):
        o = jnp.concatenate(outs[i * HEADS:(i + 1) * HEADS], axis=0)
        o_ref[0, i * NA_TQ:(i + 1) * NA_TQ, :] = o.T.astype(BF16)


def _na_call(zl, zc, bias, tq):
    b, s, _ = zl.shape
    c = zc.shape[1]
    rows_n = s // GRID_W
    return pl.pallas_call(
        functools.partial(_na_kernel, rows_n),
        grid=(b, s // tq),
        in_specs=[
            pl.BlockSpec((1, tq, MIX_W), lambda i, j: (i, j, ZC_NQ // MIX_W)),
            pl.BlockSpec((1, s, MIX_W), lambda i, j: (i, 0, ZC_NK // MIX_W)),
            pl.BlockSpec((1, s, MIX_W), lambda i, j: (i, 0, ZC_NV // MIX_W)),
            pl.BlockSpec((1, c, MIX_W), lambda i, j: (i, 0, ZC_NK // MIX_W)),
            pl.BlockSpec((1, c, MIX_W), lambda i, j: (i, 0, ZC_NV // MIX_W)),
            pl.BlockSpec(bias.shape, lambda i, j: (0, 0, 0, 0), pipeline_mode=pl.Buffered(1)),
        ],
        out_specs=pl.BlockSpec((1, tq, MIX_W), lambda i, j: (i, j, 0)),
        out_shape=jax.ShapeDtypeStruct((b, s, MIX_W), BF16),
        scratch_shapes=_exp_scratch(NA_NLOC + c, NA_TQ, HEADS * tq // NA_TQ),
        compiler_params=_params("arbitrary", "arbitrary"),
        name="na",
    )(zl, zl, zl, zc, zc, bias)


def _ctx_kernel(lam_init, z_ref, lp_ref, gain_ref, sink_ref, oa_ref, ob_ref, oc_ref, od_ref,
                e_scr):
    c_len = z_ref.shape[1]

    def blk(col, j=0):
        return z_ref[0, :, col + j * LANES:col + (j + 1) * LANES]

    units = []
    for h in range(HEADS):
        units.append((_lane_mask(blk(ZC_NQ, h // 2), (h % 2) * HEAD_DIM, HEAD_DIM),
                      ZC_NK + (h // 2) * LANES, ZC_NV + (h // 2) * LANES, (h % 2) * HEAD_DIM, None))
    for h in range(HEADS):
        units.append((blk(ZC_MQ, h), ZC_MK + h * LANES, ZC_MV + (h // 2) * LANES,
                      (h % 2) * HEAD_DIM, None))
    for h in range(HEADS):
        hk, g = h // SWA_GROUP, h % SWA_GROUP
        units.append((_lane_mask(blk(ZC_SQ, g), hk * HEAD_DIM, HEAD_DIM), ZC_SK, ZC_SV,
                      hk * HEAD_DIM, h))
    for w in range(2 * HEADS):
        units.append((_lane_mask(blk(ZC_DQ, w // 4), (w % 4) * DIFF_DH, DIFF_DH),
                      ZC_DK + (w // 4) * LANES, ZC_DV + (w // 4) * LANES, ((w // 2) % 2) * HEAD_DIM, None))

    def score_of(u, c):
        q_m, kcol = units[u][0], units[u][1]
        return _dot_nt(z_ref[0, :, kcol:kcol + LANES], q_m)

    def extra_of(u):
        h = units[u][4]
        return None if h is None else jnp.full((1, 1), sink_ref[h] * LOG2E, F32)

    def pv_of(u, c, e):
        vcol, r0 = units[u][2], units[u][3]
        return _dot_tn(z_ref[0, :, vcol:vcol + LANES], e)[r0:r0 + HEAD_DIM]

    outs = _localmax_batched(len(units), [(0, c_len)], score_of, extra_of, pv_of, e_scr)
    for i, ref in enumerate((oa_ref, ob_ref, oc_ref)):
        ref[0] = jnp.concatenate(outs[i * HEADS:(i + 1) * HEADS], axis=0).T.astype(BF16)
    o = _diff_combine(outs[3 * HEADS:], _diff_lambda(lp_ref[...], lam_init))
    od_ref[0] = (o.T * gain_ref[...]).astype(BF16)


def _ctx_call(zc, lp, gain, sink, lam_init):
    b, c, _ = zc.shape
    out = jax.ShapeDtypeStruct((b, c, MIX_W), BF16)
    ospec = pl.BlockSpec((1, c, MIX_W), lambda i: (i, 0, 0))
    return pl.pallas_call(
        functools.partial(_ctx_kernel, lam_init),
        grid=(b,),
        in_specs=[
            pl.BlockSpec((1, c, Z_COLS), lambda i: (i, 0, 0)),
            pl.BlockSpec(lp.shape, lambda i: (0, 0)),
            pl.BlockSpec(gain.shape, lambda i: (0, 0)),
            pl.BlockSpec(memory_space=pltpu.SMEM),
        ],
        out_specs=[ospec] * 4,
        out_shape=[out] * 4,
        scratch_shapes=_exp_scratch(c, c, 5 * HEADS),
        compiler_params=_params("arbitrary"),
        name="ctx_attn",
    )(zc, lp, gain, sink)


def _mlp_kernel(final, x_ref, ma_ref, mb_ref, mc_ref, md_ref, mod_ref, g_ref, gf_ref,
                wout_ref, wup_ref, wdn_ref, o_ref):
    x = x_ref[0]
    mod = mod_ref[0]
    attn = None
    for i, m_ref in enumerate((ma_ref, mb_ref, mc_ref, md_ref)):
        part = _dot(m_ref[0], wout_ref[i * MIX_W:(i + 1) * MIX_W, :])
        attn = part if attn is None else attn + part
    x = x + mod[2:3] * attn
    h = x * _rms_scale(x, D_MODEL) * g_ref[...]
    h = (h * (1.0 + mod[4:5]) + mod[3:4]).astype(BF16)
    acc = None
    ck = FF_CHUNK
    for c in range(D_FF // ck):
        u = jnp.maximum(_dot(h, wup_ref[:, c * ck:(c + 1) * ck]), 0.0)
        part = _dot((u * u).astype(BF16), wdn_ref[c * ck:(c + 1) * ck, :])
        acc = part if acc is None else acc + part
    x = x + mod[5:6] * acc
    if final:
        x = x * _rms_scale(x, D_MODEL) * gf_ref[...]
    o_ref[0] = x


def _mlp_call(x3, mixes, mod, mod_row, g, gf, wout, wup, wdn, tm, final):
    nb, n, d = x3.shape
    const = lambda t, b: (0, 0)
    tok = lambda t, b: (b, t, 0)
    mod_map = (lambda t, b: (b, 0, 0)) if mod_row is None else (lambda t, b: (mod_row, 0, 0))
    single = pl.Buffered(1)
    return pl.pallas_call(
        functools.partial(_mlp_kernel, final),
        grid=(n // tm, nb),
        in_specs=[pl.BlockSpec((1, tm, d), tok)]
        + [pl.BlockSpec((1, tm, MIX_W), tok)] * 4
        + [
            pl.BlockSpec((1, 6, d), mod_map),
            pl.BlockSpec((1, d), const),
            pl.BlockSpec((1, d), const),
            pl.BlockSpec(wout.shape, const, pipeline_mode=single),
            pl.BlockSpec(wup.shape, const, pipeline_mode=single),
            pl.BlockSpec(wdn.shape, const, pipeline_mode=single),
        ],
        out_specs=pl.BlockSpec((1, tm, d), tok),
        out_shape=jax.ShapeDtypeStruct((nb, n, d), F32),
        compiler_params=_params("arbitrary", "arbitrary"),
        name="mlp",
    )(x3, *mixes, mod, g, gf, wout, wup, wdn)


def _rope_tables(s_len):
    t = np.arange(s_len)
    row, col = t // GRID_W, t % GRID_W

    def axial(dim):
        n_freq = dim // 4
        freqs = jnp.asarray(ROPE_BASE, F32) ** (-jnp.arange(n_freq, dtype=F32) / n_freq)
        ang = jnp.concatenate([jnp.asarray(row, F32)[:, None] * freqs,
                               jnp.asarray(col, F32)[:, None] * freqs], axis=-1)
        return jnp.cos(ang), jnp.sin(ang)

    def group(cs, n_groups, scale):
        cos, sin = cs
        c = jnp.concatenate([cos, cos], axis=-1) * scale
        s = jnp.concatenate([-sin, sin], axis=-1) * scale
        return jnp.tile(c, (1, n_groups)), jnp.tile(s, (1, n_groups))

    r64, r32 = axial(HEAD_DIM), axial(MLA_ROPE)
    ones = lambda n, v=1.0: jnp.full((s_len, n), v, F32)
    zeros = lambda n: jnp.zeros((s_len, n), F32)
    mla_scale = (MLA_NOPE + MLA_ROPE) ** -0.5 * LOG2E
    mq_c, mq_s = group(r32, 1, mla_scale)
    parts = [
        group(r64, 4, HEAD_DIM ** -0.5 * LOG2E), group(r64, 2, 1.0),
        group(r32, 8, DIFF_DH ** -0.5 * LOG2E), group(r32, 8, 1.0),
        (ones(MLA_Q_LORA), zeros(MLA_Q_LORA)), group(r32, 1, 1.0), (ones(32), zeros(32)),
    ] + [(jnp.concatenate([ones(MLA_NOPE, mla_scale), mq_c, ones(32, mla_scale)], axis=-1),
          jnp.concatenate([zeros(MLA_NOPE), mq_s, zeros(32)], axis=-1))] * HEADS
    cos = jnp.concatenate([p[0] for p in parts], axis=-1)
    sin = jnp.concatenate([p[1] for p in parts], axis=-1)
    return cos, sin, cos[0:1]


def _permute_w_in(w_in):
    offs = np.cumsum([0, 256, 256, 256, MLA_Q_LORA, MLA_KV_LORA, MLA_ROPE, 256, 128, 128, 256, 256, 256])
    sec = lambda i: w_in[..., offs[i]:offs[i + 1]]
    sq = sec(6)
    sq = sq.reshape(sq.shape[:-1] + (SWA_HKV, SWA_GROUP, HEAD_DIM))
    sq = jnp.swapaxes(sq, -3, -2).reshape(w_in.shape[:-1] + (256,))
    pad = jnp.zeros(w_in.shape[:-1] + (32,), w_in.dtype)
    cols = [sq, sec(7), sec(9), sec(10), sec(4), sec(3), sec(5), pad,
            sec(0), sec(1), sec(2), sec(11), sec(8)]
    return jnp.concatenate(cols, axis=-1).astype(BF16)


def _mla_weights(w_uq, w_ukv):
    depth = w_uq.shape[0]
    uq = w_uq.reshape(depth, MLA_Q_LORA, HEADS, MLA_NOPE + MLA_ROPE)
    uq = jnp.pad(uq, ((0, 0), (0, 256 - MLA_Q_LORA), (0, 0), (0, LANES - MLA_NOPE - MLA_ROPE)))
    uq = uq.reshape(depth, 256, HEADS * LANES).astype(BF16)
    ukv = w_ukv.reshape(depth, MLA_KV_LORA, HEADS, 2, HEAD_DIM)
    uk = jnp.pad(ukv[:, :, :, 0], ((0, 0), (0, 0), (0, 0), (0, LANES - MLA_NOPE)))
    uk = uk.reshape(depth, MLA_KV_LORA, HEADS * LANES).astype(BF16)
    uv = ukv[:, :, :, 1].reshape(depth, MLA_KV_LORA, HEADS * HEAD_DIM).astype(BF16)
    return uq, uk, uv


def _na_bias_tables(rpb, rows_n):
    kr_n = min(NA_WIN_R, rows_n)
    col = np.arange(GRID_W)
    c0 = np.clip(col - NA_WIN_C // 2, 0, GRID_W - NA_WIN_C)
    col_ok_t = ((col[None, :] >= c0[:, None]) & (col[None, :] < c0[:, None] + NA_WIN_C)).T
    dc_idx_t = np.clip(col[:, None] - col[None, :], 1 - NA_WIN_C, NA_WIN_C - 1) + (NA_WIN_C - 1)
    toe = jnp.zeros(rpb.shape[:3] + (GRID_W, GRID_W), F32)
    for d in range(2 * NA_WIN_C - 1):
        toe = jnp.where(dc_idx_t == d, rpb[..., d, None, None].astype(F32), toe)
    toe = jnp.where(col_ok_t, toe * LOG2E, NEG_INF)
    neg = jnp.full(toe.shape[:2] + (GRID_W, GRID_W), NEG_INF, F32)
    nrb = rows_n // 2
    pairs = {}

    def pair(drs):
        if drs not in pairs:
            pairs[drs] = jnp.concatenate([neg if d is None else toe[:, :, d] for d in drs], axis=-1)
        return pairs[drs]

    blocks = []
    for rb in (0, 1, 2, nrb - 2, nrb - 1):
        start = int(np.clip(2 * rb - NA_WIN_R // 2, 0, rows_n - NA_KROWS))
        for j in range(NA_KROWS):
            drs = []
            for i in range(2):
                kr, qr = start + j, 2 * rb + i
                r0 = int(np.clip(qr - kr_n // 2, 0, rows_n - kr_n))
                drs.append(kr - qr + NA_WIN_R - 1 if r0 <= kr < r0 + kr_n else None)
            blocks.append(pair(tuple(drs)))
    tab = jnp.stack(blocks, axis=2)
    return tab.reshape(tab.shape[:2] + (5, NA_NLOC, NA_TQ))


def kernel(x, c, ctx, c_ctx, w_ada, b_ada, norm_attn_g, w_in, na_rpb, mla_q_norm_g, mla_w_uq,
           mla_kv_norm_g, mla_w_ukv, swa_sink, diff_lambda, diff_norm_g, w_out, norm_mlp_g,
           w_up, w_down, final_norm_g):
    b, s, d = x.shape
    c_len = ctx.shape[1]
    depth = w_in.shape[0]
    rows_n = s // GRID_W
    assert d == D_MODEL and s % max(ATTN_TQ, DIFF_TQ, WINDOW_TQ, TOKEN_TILE) == 0 and rows_n >= NA_KROWS
    assert c_len % KEY_CHUNK == 0 and (b * c_len) % min(TOKEN_TILE, b * c_len) == 0

    w_in_p = _permute_w_in(w_in)
    uq, uk, uv = _mla_weights(mla_w_uq, mla_w_ukv)
    gq = jnp.pad(mla_q_norm_g, ((0, 0), (0, 256 - MLA_Q_LORA)))[:, None, :]
    gkv = mla_kv_norm_g[:, None, :]
    w_out_b, w_up_b, w_dn_b = w_out.astype(BF16), w_up.astype(BF16), w_down.astype(BF16)
    na_bias = _na_bias_tables(na_rpb, rows_n)
    cos, sin, cos_ctx = _rope_tables(s)
    tm_l, tm_c = TOKEN_TILE, min(TOKEN_TILE, b * c_len)
    cos_c = jnp.broadcast_to(cos_ctx, (tm_c, T_COLS))
    sin_c = jnp.zeros((tm_c, T_COLS), F32)

    c_rows = -(-(b + 1) // 8) * 8
    c_all = jnp.zeros((c_rows, d), F32).at[:b].set(c).at[b].set(c_ctx)
    mod = _ada_call(c_all, w_ada, b_ada).reshape(depth, c_rows, 6, d)

    xl = x
    xc = ctx.reshape(1, b * c_len, d)
    for l in range(depth):
        need_ctx = l < depth - 1
        lam_init = 0.8 - 0.6 * math.exp(-0.3 * l)
        g_attn = norm_attn_g[l][None, :]
        g_mlp = norm_mlp_g[l][None, :]
        gf = final_norm_g[None, :]
        d_gain = (jnp.tile(diff_norm_g[l], HEADS) * (1.0 - lam_init))[None, :]
        proj = (w_in_p[l], gq[l], gkv[l], uq[l], uk[l], uv[l])
        zl = _inproj_call(xl, mod[l], None, g_attn, cos, sin, True, *proj, tm=tm_l)
        zc = _inproj_call(xc, mod[l], b, g_attn, cos_c, sin_c, False, *proj, tm=tm_c)
        zc = zc.reshape(b, c_len, Z_COLS)
        mixes = (
            _na_call(zl, zc, na_bias[l], tq=WINDOW_TQ),
            _mla_call(zl, zc, tq=ATTN_TQ),
            _swa_call(zl, zc, swa_sink[l], tq=WINDOW_TQ),
            _diff_call(zl, zc, diff_lambda[l], d_gain, lam_init, tq=DIFF_TQ),
        )
        wts = (w_out_b[l], w_up_b[l], w_dn_b[l])
        xl = _mlp_call(xl, mixes, mod[l], None, g_mlp, gf, *wts, tm=tm_l, final=not need_ctx)
        if need_ctx:
            mixes_c = _ctx_call(zc, diff_lambda[l], d_gain, swa_sink[l], lam_init)
            mixes_c = [m.reshape(1, b * c_len, MIX_W) for m in mixes_c]
            xc = _mlp_call(xc, mixes_c, mod[l], b, g_mlp, gf, *wts, tm=tm_c, final=False)
    return xl
```

```python
import functools
import math

import jax
import jax.numpy as jnp
import numpy as np
from jax import lax
from jax.experimental import pallas as pl
from jax.experimental.pallas import tpu as pltpu

F32 = jnp.float32
BF16 = jnp.bfloat16

D_MODEL = 1024
GRID_W = 64
HEADS = 4
HEAD_DIM = 64
NA_WIN_R = 8
NA_WIN_C = 16
MLA_Q_LORA = 192
MLA_KV_LORA = 128
MLA_NOPE = 64
MLA_ROPE = 32
SWA_HKV = 2
SWA_GROUP = 2
SWA_WINDOW = 128
DIFF_DH = 32
D_FF = 4 * D_MODEL
ROPE_BASE = 10000.0
EPS = 1e-6
NEG_INF = -1e30
LOG2E = math.log2(math.e)

LANES = 128
VMEM_LIMIT = 56 * 1024 * 1024

ZC_DQ, ZC_DK, ZC_SQ, ZC_NQ, ZC_NK, ZC_NV, ZC_DV, ZC_MV = 0, 256, 512, 768, 1024, 1280, 1536, 1792
ZC_MQ, ZC_MK, ZC_SK, ZC_SV = 2048, 2560, 3072, 3200
Z_COLS = 3328
W_R64, W_R32, W_M, W_N, W_COLS = 0, 384, 896, 1280, 2432
T_R64, T_R32, T_B12, T_MQ, T_COLS = 0, 384, 896, 1152, 1664

NA_KROWS = 10
NA_TQ = 2 * GRID_W
NA_NLOC = NA_KROWS * GRID_W
KEY_CHUNK = 256
UNIT_Q = 256
MIX_W = HEADS * HEAD_DIM
MLA_W = HEADS * LANES

TOKEN_TILE = 512
ATTN_TQ = 2048
WINDOW_TQ = 2048
WINDOW_CHUNK = 512
DIFF_TQ = 1024
ADA_TN = 1536
FF_CHUNK = 1024


def _params(*sem):
    return pltpu.CompilerParams(dimension_semantics=sem, vmem_limit_bytes=VMEM_LIMIT)


def _dot(a, b):
    return jnp.dot(a, b, preferred_element_type=F32)


def _dot_nt(a, b):
    return lax.dot_general(a, b, (((1,), (1,)), ((), ())), preferred_element_type=F32)


def _dot_tn(a, b):
    return lax.dot_general(a, b, (((0,), (0,)), ((), ())), preferred_element_type=F32)


def _rms_scale(x, n):
    return lax.rsqrt(jnp.sum(x * x, axis=-1, keepdims=True) * (1.0 / n) + EPS)


def _ada_kernel(c_ref, w_ref, b_ref, o_ref):
    c = c_ref[...]
    act = (c * (1.0 / (1.0 + jnp.exp(-c)))).astype(BF16)
    o_ref[0] = _dot(act, w_ref[0].astype(BF16)) + b_ref[0]


def _ada_call(c_all, w_ada, b_ada):
    depth, d, n = w_ada.shape
    rows = c_all.shape[0]
    tn = ADA_TN
    return pl.pallas_call(
        _ada_kernel,
        grid=(depth, n // tn),
        in_specs=[
            pl.BlockSpec((rows, d), lambda l, j: (0, 0)),
            pl.BlockSpec((1, d, tn), lambda l, j: (l, 0, j)),
            pl.BlockSpec((1, 1, tn), lambda l, j: (l, 0, j)),
        ],
        out_specs=pl.BlockSpec((1, rows, tn), lambda l, j: (l, 0, j)),
        out_shape=jax.ShapeDtypeStruct((depth, rows, n), F32),
        compiler_params=_params("arbitrary", "arbitrary"),
        name="ada",
    )(c_all, w_ada, b_ada.reshape(depth, 1, n))


def _rope_blocks(z, cos_ref, sin_ref, tcol, half):
    tm = z.shape[0]
    lane = lax.broadcasted_iota(jnp.int32, (tm, LANES), 1)
    first = (lane % (2 * half)) < half
    out = []
    for j in range(z.shape[1] // LANES):
        xb = z[:, j * LANES:(j + 1) * LANES]
        rot = jnp.where(first, pltpu.roll(xb, LANES - half, 1), pltpu.roll(xb, half, 1))
        c0 = tcol + j * LANES
        out.append(xb * cos_ref[:, c0:c0 + LANES] + rot * sin_ref[:, c0:c0 + LANES])
    return out


def _inproj_kernel(x_ref, mod_ref, g_ref, cos_ref, sin_ref, w_ref, gq_ref, gkv_ref,
                   wuq_ref, wukvk_ref, wukvv_ref, o_ref):
    x = x_ref[0]
    mod = mod_ref[0]
    h = x * _rms_scale(x, D_MODEL) * g_ref[...]
    h = (h * (1.0 + mod[1:2]) + mod[0:1]).astype(BF16)

    def put(col, val):
        o_ref[0, :, col:col + val.shape[1]] = val.astype(BF16)

    z64 = _dot(h, w_ref[:, W_R64:W_R32])
    z32 = _dot(h, w_ref[:, W_R32:W_M])
    r = _rope_blocks(z64, cos_ref, sin_ref, T_R64, HEAD_DIM // 2)
    put(ZC_SQ, r[0]); put(ZC_SQ + LANES, r[1]); put(ZC_SK, r[2])
    zm = _dot(h, w_ref[:, W_M:W_N])
    r = _rope_blocks(z32, cos_ref, sin_ref, T_R32, DIFF_DH // 2)
    for j in range(4):
        put(ZC_DQ + j * LANES, r[j])
    zn = _dot(h, w_ref[:, W_N:W_COLS])

    ckv = zm[:, 0:MLA_KV_LORA]
    b12 = zm[:, MLA_KV_LORA:]
    ckv_n = (ckv * _rms_scale(ckv, MLA_KV_LORA) * gkv_ref[...]).astype(BF16)
    lane = lax.broadcasted_iota(jnp.int32, b12.shape, 1)
    cq = jnp.where(lane < MLA_Q_LORA, b12, 0.0)
    cq_n = (cq * _rms_scale(cq, MLA_Q_LORA) * gq_ref[...]).astype(BF16)
    zq = _dot(cq_n, wuq_ref[...])
    put(ZC_NQ, zn[:, 0:256] * (HEAD_DIM ** -0.5 * LOG2E))
    put(ZC_NK, zn[:, 256:1024])
    put(ZC_SV, zn[:, 1024:1152])
    q = _rope_blocks(zq, cos_ref, sin_ref, T_MQ, MLA_ROPE // 2)
    for j in range(HEADS):
        put(ZC_MQ + j * LANES, q[j])
    kr = _rope_blocks(b12[:, LANES:], cos_ref, sin_ref, T_B12 + LANES, MLA_ROPE // 2)[0]
    lane = lax.broadcasted_iota(jnp.int32, kr.shape, 1)
    kr = jnp.where(jnp.where(lane >= MLA_NOPE, lane, LANES) < MLA_NOPE + MLA_ROPE, kr, 0.0)
    kn = _dot(ckv_n, wukvk_ref[...])
    for j in range(HEADS):
        put(ZC_MK + j * LANES, kn[:, j * LANES:(j + 1) * LANES] + kr)
    put(ZC_MV, _dot(ckv_n, wukvv_ref[...]))


def _inproj_call(x3, mod, mod_row, g, cos, sin, table_per_tile, w, gq, gkv, wuq, wukvk, wukvv, tm):
    nb, n, d = x3.shape
    nt = n // tm
    const = lambda t, b: (0, 0)
    tab_map = (lambda t, b: (t, 0)) if table_per_tile else const
    mod_map = (lambda t, b: (b, 0, 0)) if mod_row is None else (lambda t, b: (mod_row, 0, 0))
    return pl.pallas_call(
        _inproj_kernel,
        grid=(nt, nb),
        in_specs=[
            pl.BlockSpec((1, tm, d), lambda t, b: (b, t, 0)),
            pl.BlockSpec((1, 6, d), mod_map),
            pl.BlockSpec((1, d), const),
            pl.BlockSpec((tm, T_COLS), tab_map),
            pl.BlockSpec((tm, T_COLS), tab_map),
            pl.BlockSpec(w.shape, const, pipeline_mode=pl.Buffered(1)),
            pl.BlockSpec(gq.shape, const),
            pl.BlockSpec(gkv.shape, const),
            pl.BlockSpec(wuq.shape, const),
            pl.BlockSpec(wukvk.shape, const),
            pl.BlockSpec(wukvv.shape, const),
        ],
        out_specs=pl.BlockSpec((1, tm, Z_COLS), lambda t, b: (b, t, 0)),
        out_shape=jax.ShapeDtypeStruct((nb, n, Z_COLS), BF16),
        compiler_params=_params("arbitrary", "arbitrary"),
        name="inproj",
    )(x3, mod, g, cos, sin, w, gq, gkv, wuq, wukvk, wukvv)


def _lane_mask(q, lo, width):
    lane = lax.broadcasted_iota(jnp.int32, q.shape, 1)
    keep = jnp.where(lane >= lo, lane, LANES) < lo + width
    return jnp.where(keep, q.astype(F32), 0.0).astype(BF16)


def _localmax_softmax(n_units, chunks, score_of, extra_of, pv_of, e_scr):
    outs = [None] * n_units
    mloc, mfin = {}, {}
    n_c = len(chunks)
    for t in range(n_units + 1):
        acc = None
        if t < n_units:
            mloc[t] = []
        for cc in range(n_c + 1):
            if t < n_units and cc < n_c:
                off, rows = chunks[cc]
                s = score_of(t, cc)
                mc = jnp.max(s, axis=0, keepdims=True)
                e_scr[t % 2, off:off + rows, :] = jnp.exp2(s - mc).astype(BF16)
                mloc[t].append(mc)
            if t >= 1 and cc >= 1:
                u, c = t - 1, cc - 1
                off, rows = chunks[c]
                pv = pv_of(u, c, e_scr[u % 2, off:off + rows, :]) * jnp.exp2(mloc[u][c] - mfin[u])
                acc = pv if acc is None else acc + pv
        if t < n_units:
            m = mloc[t][0]
            for mc in mloc[t][1:]:
                m = jnp.maximum(m, mc)
            x = extra_of(t)
            mfin[t] = m if x is None else jnp.maximum(m, x)
        if t >= 1:
            l = acc[HEAD_DIM:HEAD_DIM + 1]
            x = extra_of(t - 1)
            if x is not None:
                l = l + jnp.exp2(x - mfin[t - 1])
            outs[t - 1] = acc[0:HEAD_DIM] * (1.0 / l)
    return outs


def _localmax_batched(n_units, chunks, score_of, extra_of, pv_of, e_scr):
    mloc, lloc, mfin = [], [], []
    for u in range(n_units):
        ms, ls = [], []
        for c, (off, rows) in enumerate(chunks):
            s = score_of(u, c)
            mc = jnp.max(s, axis=0, keepdims=True)
            e = jnp.exp2(s - mc)
            ls.append(jnp.sum(e, axis=0, keepdims=True))
            e_scr[u, off:off + rows, :] = e.astype(BF16)
            ms.append(mc)
        m = ms[0]
        for mc in ms[1:]:
            m = jnp.maximum(m, mc)
        x = extra_of(u)
        mloc.append(ms); lloc.append(ls); mfin.append(m if x is None else jnp.maximum(m, x))
    outs = []
    for u in range(n_units):
        acc = l = None
        for c, (off, rows) in enumerate(chunks):
            alpha = jnp.exp2(mloc[u][c] - mfin[u])
            pv = pv_of(u, c, e_scr[u, off:off + rows, :]) * alpha
            acc = pv if acc is None else acc + pv
            l = lloc[u][c] * alpha if l is None else l + lloc[u][c] * alpha
        x = extra_of(u)
        if x is not None:
            l = l + jnp.exp2(x - mfin[u])
        outs.append(acc * (1.0 / l))
    return outs


def _head_rows(o, h):
    return o[(h % 2) * HEAD_DIM:(h % 2 + 1) * HEAD_DIM]


def _diff_lambda(lp, lam_init):
    a = jnp.sum(lp[0:1] * lp[1:2], axis=-1, keepdims=True)
    b = jnp.sum(lp[2:3] * lp[3:4], axis=-1, keepdims=True)
    return jnp.exp(a) - jnp.exp(b) + lam_init


def _diff_combine(outs, lam):
    res = []
    for h in range(HEADS):
        o = outs[2 * h] - lam * outs[2 * h + 1]
        ms = jnp.sum(o * o, axis=0, keepdims=True) * (1.0 / HEAD_DIM)
        res.append(o * lax.rsqrt(ms + EPS))
    return jnp.concatenate(res, axis=0)


def _key_chunks(total, size):
    return [(o, min(size, total - o)) for o in range(0, total, size)]


def _exp_scratch(nk, tq, slots=2):
    return [pltpu.VMEM((slots, nk, tq), BF16)]


VT_ROWS = HEAD_DIM + 16


def _transpose_values(vt_scr, srcs):
    for ref, off, n in srcs:
        for r in range(0, n, KEY_CHUNK):
            rows = min(KEY_CHUNK, n - r)
            vt = ref[0, r:r + rows, :].astype(F32).T
            for h in range(HEADS):
                vt_scr[h * VT_ROWS:h * VT_ROWS + HEAD_DIM, off + r:off + r + rows] = (
                    vt[h * HEAD_DIM:(h + 1) * HEAD_DIM].astype(BF16))
                vt_scr[h * VT_ROWS + HEAD_DIM:(h + 1) * VT_ROWS, off + r:off + r + rows] = (
                    jnp.ones((VT_ROWS - HEAD_DIM, rows), BF16))


def _mla_kernel(q_ref, k_ref, v_ref, kc_ref, vc_ref, o_ref, e_scr, vt_scr):
    s_len, c_len = k_ref.shape[1], kc_ref.shape[1]

    @pl.when(pl.program_id(1) == 0)
    def _():
        _transpose_values(vt_scr, [(v_ref, 0, s_len), (vc_ref, s_len, c_len)])

    chunks = _key_chunks(s_len, KEY_CHUNK) + [(s_len + o, r) for o, r in _key_chunks(c_len, KEY_CHUNK)]
    units = [(qt, h) for qt in range(q_ref.shape[1] // UNIT_Q) for h in range(HEADS)]
    qs = [q_ref[0, qt * UNIT_Q:(qt + 1) * UNIT_Q, h * LANES:(h + 1) * LANES] for qt, h in units]

    def score_of(u, c):
        off, rows = chunks[c]
        h = units[u][1]
        sl = slice(h * LANES, (h + 1) * LANES)
        if off < s_len:
            return _dot_nt(k_ref[0, off:off + rows, sl], qs[u])
        return _dot_nt(kc_ref[0, off - s_len:off - s_len + rows, sl], qs[u])

    def pv_of(u, c, e):
        h = units[u][1]
        off, rows = chunks[c]
        return _dot(vt_scr[h * VT_ROWS:(h + 1) * VT_ROWS, off:off + rows], e)

    outs = _localmax_softmax(len(units), chunks, score_of, lambda u: None, pv_of, e_scr)
    for qt in range(len(units) // HEADS):
        o = jnp.concatenate(outs[qt * HEADS:(qt + 1) * HEADS], axis=0)
        o_ref[0, qt * UNIT_Q:(qt + 1) * UNIT_Q, :] = o.T.astype(BF16)


def _mla_call(zl, zc, tq):
    b, s, _ = zl.shape
    c = zc.shape[1]
    return pl.pallas_call(
        _mla_kernel,
        grid=(b, s // tq),
        in_specs=[
            pl.BlockSpec((1, tq, MLA_W), lambda i, j: (i, j, ZC_MQ // MLA_W)),
            pl.BlockSpec((1, s, MLA_W), lambda i, j: (i, 0, ZC_MK // MLA_W)),
            pl.BlockSpec((1, s, MIX_W), lambda i, j: (i, 0, ZC_MV // MIX_W)),
            pl.BlockSpec((1, c, MLA_W), lambda i, j: (i, 0, ZC_MK // MLA_W)),
            pl.BlockSpec((1, c, MIX_W), lambda i, j: (i, 0, ZC_MV // MIX_W)),
        ],
        out_specs=pl.BlockSpec((1, tq, MIX_W), lambda i, j: (i, j, 0)),
        out_shape=jax.ShapeDtypeStruct((b, s, MIX_W), BF16),
        scratch_shapes=_exp_scratch(s + c, UNIT_Q) + [pltpu.VMEM((HEADS * VT_ROWS, s + c), BF16)],
        compiler_params=_params("arbitrary", "arbitrary"),
        name="mla",
    )(zl, zl, zl, zc, zc)


def _diff_kernel(lam_init, q_ref, k_ref, v_ref, kc_ref, vc_ref, lp_ref, gain_ref, o_ref,
                 e_scr, vt_scr):
    s_len, c_len = k_ref.shape[1], kc_ref.shape[1]

    @pl.when(pl.program_id(1) == 0)
    def _():
        _transpose_values(vt_scr, [(v_ref, 0, s_len), (vc_ref, s_len, c_len)])

    chunks = _key_chunks(s_len, KEY_CHUNK) + [(s_len + o, r) for o, r in _key_chunks(c_len, KEY_CHUNK)]
    n_sub = 2 * HEADS
    units = [(qt, w) for qt in range(q_ref.shape[1] // UNIT_Q) for w in range(n_sub)]
    qs = [_lane_mask(q_ref[0, qt * UNIT_Q:(qt + 1) * UNIT_Q, (w // 4) * LANES:(w // 4 + 1) * LANES],
                     (w % 4) * DIFF_DH, DIFF_DH) for qt, w in units]

    def score_of(u, c):
        off, rows = chunks[c]
        w = units[u][1]
        sl = slice((w // 4) * LANES, (w // 4 + 1) * LANES)
        if off < s_len:
            return _dot_nt(k_ref[0, off:off + rows, sl], qs[u])
        return _dot_nt(kc_ref[0, off - s_len:off - s_len + rows, sl], qs[u])

    def pv_of(u, c, e):
        h = units[u][1] // 2
        off, rows = chunks[c]
        return _dot(vt_scr[h * VT_ROWS:(h + 1) * VT_ROWS, off:off + rows], e)

    outs = _localmax_softmax(len(units), chunks, score_of, lambda u: None, pv_of, e_scr)
    lam = _diff_lambda(lp_ref[...], lam_init)
    for qt in range(len(units) // n_sub):
        o = _diff_combine(outs[qt * n_sub:(qt + 1) * n_sub], lam)
        o_ref[0, qt * UNIT_Q:(qt + 1) * UNIT_Q, :] = (o.T * gain_ref[...]).astype(BF16)


def _diff_call(zl, zc, lp, gain, lam_init, tq):
    b, s, _ = zl.shape
    c = zc.shape[1]
    return pl.pallas_call(
        functools.partial(_diff_kernel, lam_init),
        grid=(b, s // tq),
        in_specs=[
            pl.BlockSpec((1, tq, MIX_W), lambda i, j: (i, j, ZC_DQ // MIX_W)),
            pl.BlockSpec((1, s, MIX_W), lambda i, j: (i, 0, ZC_DK // MIX_W)),
            pl.BlockSpec((1, s, MIX_W), lambda i, j: (i, 0, ZC_DV // MIX_W)),
            pl.BlockSpec((1, c, MIX_W), lambda i, j: (i, 0, ZC_DK // MIX_W)),
            pl.BlockSpec((1, c, MIX_W), lambda i, j: (i, 0, ZC_DV // MIX_W)),
            pl.BlockSpec(lp.shape, lambda i, j: (0, 0)),
            pl.BlockSpec(gain.shape, lambda i, j: (0, 0)),
        ],
        out_specs=pl.BlockSpec((1, tq, MIX_W), lambda i, j: (i, j, 0)),
        out_shape=jax.ShapeDtypeStruct((b, s, MIX_W), BF16),
        scratch_shapes=_exp_scratch(s + c, UNIT_Q) + [pltpu.VMEM((HEADS * VT_ROWS, s + c), BF16)],
        compiler_params=_params("arbitrary", "arbitrary"),
        name="diff",
    )(zl, zl, zl, zc, zc, lp, gain)


def _swa_kernel(q_ref, k_ref, v_ref, kc_ref, vc_ref, sink_ref, o_ref, e_scr, mask_scr):
    s_len, c_len = k_ref.shape[1], kc_ref.shape[1]
    n_qt = q_ref.shape[1] // UNIT_Q
    win = UNIT_Q + 2 * SWA_WINDOW
    chunks = _key_chunks(win, WINDOW_CHUNK) + [(win + o, r) for o, r in _key_chunks(c_len, WINDOW_CHUNK)]
    starts = []
    for qt in range(n_qt):
        q0 = (pl.program_id(1) * n_qt + qt) * UNIT_Q
        start = pl.multiple_of(jnp.clip(q0 - SWA_WINDOW, 0, s_len - win), SWA_WINDOW)
        kpos = start + lax.broadcasted_iota(jnp.int32, (win, UNIT_Q), 0)
        qpos = q0 + lax.broadcasted_iota(jnp.int32, (win, UNIT_Q), 1)
        mask_scr[qt] = jnp.where(jnp.abs(kpos - qpos) <= SWA_WINDOW, 0.0, NEG_INF)
        starts.append(start)
    units = [(qt, w) for qt in range(n_qt) for w in range(HEADS)]
    qs = [_lane_mask(q_ref[0, qt * UNIT_Q:(qt + 1) * UNIT_Q, (w // SWA_HKV) * LANES:(w // SWA_HKV + 1) * LANES],
                     (w % SWA_HKV) * HEAD_DIM, HEAD_DIM) for qt, w in units]
    head_of = lambda w: (w % SWA_HKV) * SWA_GROUP + w // SWA_HKV

    def score_of(u, c):
        off, rows = chunks[c]
        qt = units[u][0]
        if off < win:
            k = k_ref[0, pl.ds(starts[qt] + off, rows), :]
            return _dot_nt(k, qs[u]) + mask_scr[qt, off:off + rows, :]
        return _dot_nt(kc_ref[0, off - win:off - win + rows, :], qs[u])

    def extra_of(u):
        return jnp.full((1, 1), sink_ref[head_of(units[u][1])] * LOG2E, F32)

    def pv_of(u, c, e):
        qt, w = units[u]
        off, rows = chunks[c]
        if off < win:
            o = _dot_tn(v_ref[0, pl.ds(starts[qt] + off, rows), :], e)
        else:
            o = _dot_tn(vc_ref[0, off - win:off - win + rows, :], e)
        return o[(w % SWA_HKV) * HEAD_DIM:(w % SWA_HKV + 1) * HEAD_DIM]

    outs = _localmax_batched(len(units), chunks, score_of, extra_of, pv_of, e_scr)
    for qt in range(n_qt):
        by_head = {head_of(w): outs[qt * HEADS + w] for w in range(HEADS)}
        o = jnp.concatenate([by_head[h] for h in range(HEADS)], axis=0)
        o_ref[0, qt * UNIT_Q:(qt + 1) * UNIT_Q, :] = o.T.astype(BF16)


def _swa_call(zl, zc, sink, tq):
    b, s, _ = zl.shape
    c = zc.shape[1]
    win = UNIT_Q + 2 * SWA_WINDOW
    return pl.pallas_call(
        _swa_kernel,
        grid=(b, s // tq),
        in_specs=[
            pl.BlockSpec((1, tq, MIX_W), lambda i, j: (i, j, ZC_SQ // MIX_W)),
            pl.BlockSpec((1, s, LANES), lambda i, j: (i, 0, ZC_SK // LANES)),
            pl.BlockSpec((1, s, LANES), lambda i, j: (i, 0, ZC_SV // LANES)),
            pl.BlockSpec((1, c, LANES), lambda i, j: (i, 0, ZC_SK // LANES)),
            pl.BlockSpec((1, c, LANES), lambda i, j: (i, 0, ZC_SV // LANES)),
            pl.BlockSpec(memory_space=pltpu.SMEM),
        ],
        out_specs=pl.BlockSpec((1, tq, MIX_W), lambda i, j: (i, j, 0)),
        out_shape=jax.ShapeDtypeStruct((b, s, MIX_W), BF16),
        scratch_shapes=(_exp_scratch(win + c, UNIT_Q, HEADS * tq // UNIT_Q)
                        + [pltpu.VMEM((tq // UNIT_Q, win, UNIT_Q), F32)]),
        compiler_params=_params("arbitrary", "arbitrary"),
        name="swa",
    )(zl, zl, zl, zc, zc, sink)


def _na_kernel(rows_n, q_ref, k_ref, v_ref, kc_ref, vc_ref, bias_ref, o_ref, e_scr):
    c_len = kc_ref.shape[1]
    n_rb = q_ref.shape[1] // NA_TQ
    nrb = rows_n // 2
    chunks = _key_chunks(NA_NLOC, KEY_CHUNK) + [(NA_NLOC + o, r) for o, r in _key_chunks(c_len, KEY_CHUNK)]
    starts, tids = [], []
    for i in range(n_rb):
        rb = pl.program_id(1) * n_rb + i
        starts.append(pl.multiple_of(
            jnp.clip(2 * rb - NA_WIN_R // 2, 0, rows_n - NA_KROWS) * GRID_W, 2 * GRID_W))
        tids.append(jnp.minimum(rb, 2) + jnp.maximum(rb - (nrb - 3), 0))
    blk = lambda h: slice((h // 2) * LANES, (h // 2 + 1) * LANES)
    units = [(i, h) for i in range(n_rb) for h in range(HEADS)]
    qs = [_lane_mask(q_ref[0, i * NA_TQ:(i + 1) * NA_TQ, blk(h)], (h % 2) * HEAD_DIM, HEAD_DIM)
          for i, h in units]

    def score_of(u, c):
        off, rows = chunks[c]
        i, h = units[u]
        if off < NA_NLOC:
            k = k_ref[0, pl.ds(starts[i] + off, rows), blk(h)]
            return _dot_nt(k, qs[u]) + bias_ref[h, tids[i], off:off + rows, :]
        return _dot_nt(kc_ref[0, off - NA_NLOC:off - NA_NLOC + rows, blk(h)], qs[u])

    def pv_of(u, c, e):
        i, h = units[u]
        off, rows = chunks[c]
        if off < NA_NLOC:
            return _head_rows(_dot_tn(v_ref[0, pl.ds(starts[i] + off, rows), blk(h)], e), h)
        return _head_rows(_dot_tn(vc_ref[0, off - NA_NLOC:off - NA_NLOC + rows, blk(h)], e), h)

    outs = _localmax_batched(len(units), chunks, score_of, lambda u: None, pv_of, e_scr)
    for i in range(n_rb):
        o = jnp.concatenate(outs[i * HEADS:(i + 1) * HEADS], axis=0)
        o_ref[0, i * NA_TQ:(i + 1) * NA_TQ, :] = o.T.astype(BF16)


def _na_call(zl, zc, bias, tq):
    b, s, _ = zl.shape
    c = zc.shape[1]
    rows_n = s // GRID_W
    return pl.pallas_call(
        functools.partial(_na_kernel, rows_n),
        grid=(b, s // tq),
        in_specs=[
            pl.BlockSpec((1, tq, MIX_W), lambda i, j: (i, j, ZC_NQ // MIX_W)),
            pl.BlockSpec((1, s, MIX_W), lambda i, j: (i, 0, ZC_NK // MIX_W)),
            pl.BlockSpec((1, s, MIX_W), lambda i, j: (i, 0, ZC_NV // MIX_W)),
            pl.BlockSpec((1, c, MIX_W), lambda i, j: (i, 0, ZC_NK // MIX_W)),
            pl.BlockSpec((1, c, MIX_W), lambda i, j: (i, 0, ZC_NV // MIX_W)),
            pl.BlockSpec(bias.shape, lambda i, j: (0, 0, 0, 0), pipeline_mode=pl.Buffered(1)),
        ],
        out_specs=pl.BlockSpec((1, tq, MIX_W), lambda i, j: (i, j, 0)),
        out_shape=jax.ShapeDtypeStruct((b, s, MIX_W), BF16),
        scratch_shapes=_exp_scratch(NA_NLOC + c, NA_TQ, HEADS * tq // NA_TQ),
        compiler_params=_params("arbitrary", "arbitrary"),
        name="na",
    )(zl, zl, zl, zc, zc, bias)


def _ctx_kernel(lam_init, z_ref, lp_ref, gain_ref, sink_ref, oa_ref, ob_ref, oc_ref, od_ref,
                e_scr):
    c_len = z_ref.shape[1]

    def blk(col, j=0):
        return z_ref[0, :, col + j * LANES:col + (j + 1) * LANES]

    units = []
    for h in range(HEADS):
        units.append((_lane_mask(blk(ZC_NQ, h // 2), (h % 2) * HEAD_DIM, HEAD_DIM),
                      ZC_NK + (h // 2) * LANES, ZC_NV + (h // 2) * LANES, (h % 2) * HEAD_DIM, None))
    for h in range(HEADS):
        units.append((blk(ZC_MQ, h), ZC_MK + h * LANES, ZC_MV + (h // 2) * LANES,
                      (h % 2) * HEAD_DIM, None))
    for h in range(HEADS):
        hk, g = h // SWA_GROUP, h % SWA_GROUP
        units.append((_lane_mask(blk(ZC_SQ, g), hk * HEAD_DIM, HEAD_DIM), ZC_SK, ZC_SV,
                      hk * HEAD_DIM, h))
    for w in range(2 * HEADS):
        units.append((_lane_mask(blk(ZC_DQ, w // 4), (w % 4) * DIFF_DH, DIFF_DH),
                      ZC_DK + (w // 4) * LANES, ZC_DV + (w // 4) * LANES, ((w // 2) % 2) * HEAD_DIM, None))

    def score_of(u, c):
        q_m, kcol = units[u][0], units[u][1]
        return _dot_nt(z_ref[0, :, kcol:kcol + LANES], q_m)

    def extra_of(u):
        h = units[u][4]
        return None if h is None else jnp.full((1, 1), sink_ref[h] * LOG2E, F32)

    def pv_of(u, c, e):
        vcol, r0 = units[u][2], units[u][3]
        return _dot_tn(z_ref[0, :, vcol:vcol + LANES], e)[r0:r0 + HEAD_DIM]

    outs = _localmax_batched(len(units), [(0, c_len)], score_of, extra_of, pv_of, e_scr)
    for i, ref in enumerate((oa_ref, ob_ref, oc_ref)):
        ref[0] = jnp.concatenate(outs[i * HEADS:(i + 1) * HEADS], axis=0).T.astype(BF16)
    o = _diff_combine(outs[3 * HEADS:], _diff_lambda(lp_ref[...], lam_init))
    od_ref[0] = (o.T * gain_ref[...]).astype(BF16)


def _ctx_call(zc, lp, gain, sink, lam_init):
    b, c, _ = zc.shape
    out = jax.ShapeDtypeStruct((b, c, MIX_W), BF16)
    ospec = pl.BlockSpec((1, c, MIX_W), lambda i: (i, 0, 0))
    return pl.pallas_call(
        functools.partial(_ctx_kernel, lam_init),
        grid=(b,),
        in_specs=[
            pl.BlockSpec((1, c, Z_COLS), lambda i: (i, 0, 0)),
            pl.BlockSpec(lp.shape, lambda i: (0, 0)),
            pl.BlockSpec(gain.shape, lambda i: (0, 0)),
            pl.BlockSpec(memory_space=pltpu.SMEM),
        ],
        out_specs=[ospec] * 4,
        out_shape=[out] * 4,
        scratch_shapes=_exp_scratch(c, c, 5 * HEADS),
        compiler_params=_params("arbitrary"),
        name="ctx_attn",
    )(zc, lp, gain, sink)


def _mlp_kernel(final, x_ref, ma_ref, mb_ref, mc_ref, md_ref, mod_ref, g_ref, gf_ref,
                wout_ref, wup_ref, wdn_ref, o_ref):
    x = x_ref[0]
    mod = mod_ref[0]
    attn = None
    for i, m_ref in enumerate((ma_ref, mb_ref, mc_ref, md_ref)):
        part = _dot(m_ref[0], wout_ref[i * MIX_W:(i + 1) * MIX_W, :])
        attn = part if attn is None else attn + part
    x = x + mod[2:3] * attn
    h = x * _rms_scale(x, D_MODEL) * g_ref[...]
    h = (h * (1.0 + mod[4:5]) + mod[3:4]).astype(BF16)
    acc = None
    ck = FF_CHUNK
    for c in range(D_FF // ck):
        u = jnp.maximum(_dot(h, wup_ref[:, c * ck:(c + 1) * ck]), 0.0)
        part = _dot((u * u).astype(BF16), wdn_ref[c * ck:(c + 1) * ck, :])
        acc = part if acc is None else acc + part
    x = x + mod[5:6] * acc
    if final:
        x = x * _rms_scale(x, D_MODEL) * gf_ref[...]
    o_ref[0] = x


def _mlp_call(x3, mixes, mod, mod_row, g, gf, wout, wup, wdn, tm, final):
    nb, n, d = x3.shape
    const = lambda t, b: (0, 0)
    tok = lambda t, b: (b, t, 0)
    mod_map = (lambda t, b: (b, 0, 0)) if mod_row is None else (lambda t, b: (mod_row, 0, 0))
    single = pl.Buffered(1)
    return pl.pallas_call(
        functools.partial(_mlp_kernel, final),
        grid=(n // tm, nb),
        in_specs=[pl.BlockSpec((1, tm, d), tok)]
        + [pl.BlockSpec((1, tm, MIX_W), tok)] * 4
        + [
            pl.BlockSpec((1, 6, d), mod_map),
            pl.BlockSpec((1, d), const),
            pl.BlockSpec((1, d), const),
            pl.BlockSpec(wout.shape, const, pipeline_mode=single),
            pl.BlockSpec(wup.shape, const, pipeline_mode=single),
            pl.BlockSpec(wdn.shape, const, pipeline_mode=single),
        ],
        out_specs=pl.BlockSpec((1, tm, d), tok),
        out_shape=jax.ShapeDtypeStruct((nb, n, d), F32),
        compiler_params=_params("arbitrary", "arbitrary"),
        name="mlp",
    )(x3, *mixes, mod, g, gf, wout, wup, wdn)


def _rope_tables(s_len):
    t = np.arange(s_len)
    row, col = t // GRID_W, t % GRID_W

    def axial(dim):
        n_freq = dim // 4
        freqs = jnp.asarray(ROPE_BASE, F32) ** (-jnp.arange(n_freq, dtype=F32) / n_freq)
        ang = jnp.concatenate([jnp.asarray(row, F32)[:, None] * freqs,
                               jnp.asarray(col, F32)[:, None] * freqs], axis=-1)
        return jnp.cos(ang), jnp.sin(ang)

    def group(cs, n_groups, scale):
        cos, sin = cs
        c = jnp.concatenate([cos, cos], axis=-1) * scale
        s = jnp.concatenate([-sin, sin], axis=-1) * scale
        return jnp.tile(c, (1, n_groups)), jnp.tile(s, (1, n_groups))

    r64, r32 = axial(HEAD_DIM), axial(MLA_ROPE)
    ones = lambda n, v=1.0: jnp.full((s_len, n), v, F32)
    zeros = lambda n: jnp.zeros((s_len, n), F32)
    mla_scale = (MLA_NOPE + MLA_ROPE) ** -0.5 * LOG2E
    mq_c, mq_s = group(r32, 1, mla_scale)
    parts = [
        group(r64, 4, HEAD_DIM ** -0.5 * LOG2E), group(r64, 2, 1.0),
        group(r32, 8, DIFF_DH ** -0.5 * LOG2E), group(r32, 8, 1.0),
        (ones(MLA_Q_LORA), zeros(MLA_Q_LORA)), group(r32, 1, 1.0), (ones(32), zeros(32)),
    ] + [(jnp.concatenate([ones(MLA_NOPE, mla_scale), mq_c, ones(32, mla_scale)], axis=-1),
          jnp.concatenate([zeros(MLA_NOPE), mq_s, zeros(32)], axis=-1))] * HEADS
    cos = jnp.concatenate([p[0] for p in parts], axis=-1)
    sin = jnp.concatenate([p[1] for p in parts], axis=-1)
    return cos, sin, cos[0:1]


def _permute_w_in(w_in):
    offs = np.cumsum([0, 256, 256, 256, MLA_Q_LORA, MLA_KV_LORA, MLA_ROPE, 256, 128, 128, 256, 256, 256])
    sec = lambda i: w_in[..., offs[i]:offs[i + 1]]
    sq = sec(6)
    sq = sq.reshape(sq.shape[:-1] + (SWA_HKV, SWA_GROUP, HEAD_DIM))
    sq = jnp.swapaxes(sq, -3, -2).reshape(w_in.shape[:-1] + (256,))
    pad = jnp.zeros(w_in.shape[:-1] + (32,), w_in.dtype)
    cols = [sq, sec(7), sec(9), sec(10), sec(4), sec(3), sec(5), pad,
            sec(0), sec(1), sec(2), sec(11), sec(8)]
    return jnp.concatenate(cols, axis=-1).astype(BF16)


def _mla_weights(w_uq, w_ukv):
    depth = w_uq.shape[0]
    uq = w_uq.reshape(depth, MLA_Q_LORA, HEADS, MLA_NOPE + MLA_ROPE)
    uq = jnp.pad(uq, ((0, 0), (0, 256 - MLA_Q_LORA), (0, 0), (0, LANES - MLA_NOPE - MLA_ROPE)))
    uq = uq.reshape(depth, 256, HEADS * LANES).astype(BF16)
    ukv = w_ukv.reshape(depth, MLA_KV_LORA, HEADS, 2, HEAD_DIM)
    uk = jnp.pad(ukv[:, :, :, 0], ((0, 0), (0, 0), (0, 0), (0, LANES - MLA_NOPE)))
    uk = uk.reshape(depth, MLA_KV_LORA, HEADS * LANES).astype(BF16)
    uv = ukv[:, :, :, 1].reshape(depth, MLA_KV_LORA, HEADS * HEAD_DIM).astype(BF16)
    return uq, uk, uv


def _na_bias_tables(rpb, rows_n):
    kr_n = min(NA_WIN_R, rows_n)
    col = np.arange(GRID_W)
    c0 = np.clip(col - NA_WIN_C // 2, 0, GRID_W - NA_WIN_C)
    col_ok_t = ((col[None, :] >= c0[:, None]) & (col[None, :] < c0[:, None] + NA_WIN_C)).T
    dc_idx_t = np.clip(col[:, None] - col[None, :], 1 - NA_WIN_C, NA_WIN_C - 1) + (NA_WIN_C - 1)
    toe = jnp.zeros(rpb.shape[:3] + (GRID_W, GRID_W), F32)
    for d in range(2 * NA_WIN_C - 1):
        toe = jnp.where(dc_idx_t == d, rpb[..., d, None, None].astype(F32), toe)
    toe = jnp.where(col_ok_t, toe * LOG2E, NEG_INF)
    neg = jnp.full(toe.shape[:2] + (GRID_W, GRID_W), NEG_INF, F32)
    nrb = rows_n // 2
    pairs = {}

    def pair(drs):
        if drs not in pairs:
            pairs[drs] = jnp.concatenate([neg if d is None else toe[:, :, d] for d in drs], axis=-1)
        return pairs[drs]

    blocks = []
    for rb in (0, 1, 2, nrb - 2, nrb - 1):
        start = int(np.clip(2 * rb - NA_WIN_R // 2, 0, rows_n - NA_KROWS))
        for j in range(NA_KROWS):
            drs = []
            for i in range(2):
                kr, qr = start + j, 2 * rb + i
                r0 = int(np.clip(qr - kr_n // 2, 0, rows_n - kr_n))
                drs.append(kr - qr + NA_WIN_R - 1 if r0 <= kr < r0 + kr_n else None)
            blocks.append(pair(tuple(drs)))
    tab = jnp.stack(blocks, axis=2)
    return tab.reshape(tab.shape[:2] + (5, NA_NLOC, NA_TQ))


def kernel(x, c, ctx, c_ctx, w_ada, b_ada, norm_attn_g, w_in, na_rpb, mla_q_norm_g, mla_w_uq,
           mla_kv_norm_g, mla_w_ukv, swa_sink, diff_lambda, diff_norm_g, w_out, norm_mlp_g,
           w_up, w_down, final_norm_g):
    b, s, d = x.shape
    c_len = ctx.shape[1]
    depth = w_in.shape[0]
    rows_n = s // GRID_W
    assert d == D_MODEL and s % max(ATTN_TQ, DIFF_TQ, WINDOW_TQ, TOKEN_TILE) == 0 and rows_n >= NA_KROWS
    assert c_len % KEY_CHUNK == 0 and (b * c_len) % min(TOKEN_TILE, b * c_len) == 0

    w_in_p = _permute_w_in(w_in)
    uq, uk, uv = _mla_weights(mla_w_uq, mla_w_ukv)
    gq = jnp.pad(mla_q_norm_g, ((0, 0), (0, 256 - MLA_Q_LORA)))[:, None, :]
    gkv = mla_kv_norm_g[:, None, :]
    w_out_b, w_up_b, w_dn_b = w_out.astype(BF16), w_up.astype(BF16), w_down.astype(BF16)
    na_bias = _na_bias_tables(na_rpb, rows_n)
    cos, sin, cos_ctx = _rope_tables(s)
    tm_l, tm_c = TOKEN_TILE, min(TOKEN_TILE, b * c_len)
    cos_c = jnp.broadcast_to(cos_ctx, (tm_c, T_COLS))
    sin_c = jnp.zeros((tm_c, T_COLS), F32)

    c_rows = -(-(b + 1) // 8) * 8
    c_all = jnp.zeros((c_rows, d), F32).at[:b].set(c).at[b].set(c_ctx)
    mod = _ada_call(c_all, w_ada, b_ada).reshape(depth, c_rows, 6, d)

    xl = x
    xc = ctx.reshape(1, b * c_len, d)
    for l in range(depth):
        need_ctx = l < depth - 1
        lam_init = 0.8 - 0.6 * math.exp(-0.3 * l)
        g_attn = norm_attn_g[l][None, :]
        g_mlp = norm_mlp_g[l][None, :]
        gf = final_norm_g[None, :]
        d_gain = (jnp.tile(diff_norm_g[l], HEADS) * (1.0 - lam_init))[None, :]
        proj = (w_in_p[l], gq[l], gkv[l], uq[l], uk[l], uv[l])
        zl = _inproj_call(xl, mod[l], None, g_attn, cos, sin, True, *proj, tm=tm_l)
        zc = _inproj_call(xc, mod[l], b, g_attn, cos_c, sin_c, False, *proj, tm=tm_c)
        zc = zc.reshape(b, c_len, Z_COLS)
        mixes = (
            _na_call(zl, zc, na_bias[l], tq=WINDOW_TQ),
            _mla_call(zl, zc, tq=ATTN_TQ),
            _swa_call(zl, zc, swa_sink[l], tq=WINDOW_TQ),
            _diff_call(zl, zc, diff_lambda[l], d_gain, lam_init, tq=DIFF_TQ),
        )
        wts = (w_out_b[l], w_up_b[l], w_dn_b[l])
        xl = _mlp_call(xl, mixes, mod[l], None, g_mlp, gf, *wts, tm=tm_l, final=not need_ctx)
        if need_ctx:
            mixes_c = _ctx_call(zc, diff_lambda[l], d_gain, swa_sink[l], lam_init)
            mixes_c = [m.reshape(1, b * c_len, MIX_W) for m in mixes_c]
            xc = _mlp_call(xc, mixes_c, mod[l], b, g_mlp, gf, *wts, tm=tm_c, final=False)
    return xl
```
